```python
import jax, jax.numpy as jnp
from jax import lax
import numpy as np


D_MODEL = 1024
BATCH = 16
SEQ = 2048
DEPTH = 4
DEC_BATCH = 32
DEC_SEQ = 2048
PAST_LEN = 128

PLE_DIM = 256
N_MIXERS = 2
N_MLA_LAYERS = (DEPTH + 1) // 2
N_GDN_LAYERS = DEPTH // 2
MLA_HEADS = 8
MLA_NOPE_DIM = 128
MLA_ROPE_DIM = 64
MLA_V_DIM = 128
MLA_Q_LORA = 256
MLA_KV_LORA = 128
MLA_IN_DIM = MLA_Q_LORA + MLA_KV_LORA + MLA_ROPE_DIM
ROPE_THETA = 10000.0
Q_BLOCK = 128
GDN_HEADS = 8
GDN_DK = 128
GDN_DV = 128
GDN_CONV = 5
GDN_CHUNK = 64
GDN_QKV_DIM = 2 * GDN_HEADS * GDN_DK + GDN_HEADS * GDN_DV
GDN_IN_DIM = GDN_QKV_DIM + GDN_HEADS * GDN_DV + 4 * GDN_HEADS
N_EXPERTS = 16
D_EXPERT = 1024
EC_CAPACITY_FACTOR = 2
NORM_EPS = 1e-6

kernel_name = 'hybrid_mla_gdn_ec_encoder'


def rmsnorm(x, g):
    x32 = x.astype(jnp.float32)
    y = x32 * lax.rsqrt(jnp.mean(x32 * x32, axis=-1, keepdims=True) + NORM_EPS)
    return (y * g.astype(jnp.float32)).astype(x.dtype)


def l2norm(x):
    x32 = x.astype(jnp.float32)
    return x32 * lax.rsqrt(jnp.sum(x32 * x32, axis=-1, keepdims=True) + NORM_EPS)


def rope_tables(seq):
    pos = jnp.arange(seq, dtype=jnp.float32)
    inv = ROPE_THETA ** (-jnp.arange(0, MLA_ROPE_DIM, 2, dtype=jnp.float32) / MLA_ROPE_DIM)
    ang = pos[:, None] * inv[None, :]
    return jnp.cos(ang), jnp.sin(ang)


def apply_rope(x, cos, sin):
    half = x.shape[-1] // 2
    x1, x2 = x[..., :half], x[..., half:]
    return jnp.concatenate([x1 * cos - x2 * sin, x2 * cos + x1 * sin], axis=-1)


def mla_mixer(h, w_in, q_norm_g, w_qb, kv_norm_g, w_kvb, w_out):
    B, S, _ = h.shape
    H = MLA_HEADS
    lat = h @ w_in
    q_lat = lat[..., :MLA_Q_LORA]
    kv_lat = lat[..., MLA_Q_LORA:MLA_Q_LORA + MLA_KV_LORA]
    k_rope = lat[..., MLA_Q_LORA + MLA_KV_LORA:]
    q = (rmsnorm(q_lat, q_norm_g) @ w_qb).reshape(B, S, H, MLA_NOPE_DIM + MLA_ROPE_DIM)
    q_nope, q_rope = q[..., :MLA_NOPE_DIM], q[..., MLA_NOPE_DIM:]
    kv = (rmsnorm(kv_lat, kv_norm_g) @ w_kvb).reshape(B, S, H, MLA_NOPE_DIM + MLA_V_DIM)
    k_nope, v = kv[..., :MLA_NOPE_DIM], kv[..., MLA_NOPE_DIM:]
    cos, sin = rope_tables(S)
    cos, sin = cos.astype(h.dtype), sin.astype(h.dtype)
    q_rope = apply_rope(q_rope, cos[:, None, :], sin[:, None, :])
    k_rope = apply_rope(k_rope, cos, sin)
    scale = (MLA_NOPE_DIM + MLA_ROPE_DIM) ** -0.5
    nb = S // Q_BLOCK
    qn_b = q_nope.reshape(B, nb, Q_BLOCK, H, MLA_NOPE_DIM).transpose(1, 0, 2, 3, 4)
    qr_b = q_rope.reshape(B, nb, Q_BLOCK, H, MLA_ROPE_DIM).transpose(1, 0, 2, 3, 4)

    def query_block(args):
        qn, qr = args
        s = jnp.einsum('bqhd,bkhd->bhqk', qn, k_nope) + jnp.einsum('bqhr,bkr->bhqk', qr, k_rope)
        pr = jax.nn.softmax(s.astype(jnp.float32) * scale, axis=-1).astype(v.dtype)
        return jnp.einsum('bhqk,bkhd->bqhd', pr, v)

    o = lax.map(query_block, (qn_b, qr_b))
    o = o.transpose(1, 0, 2, 3, 4).reshape(B, S, H * MLA_V_DIM)
    return o @ w_out


def centred_short_conv(x, w):
    S = x.shape[1]
    pad = GDN_CONV // 2
    xp = jnp.pad(x, ((0, 0), (pad, pad), (0, 0)))
    out = xp[:, 0:S, :] * w[0]
    for j in range(1, GDN_CONV):
        out = out + xp[:, j:j + S, :] * w[j]
    return out


def chunk_gated_delta(q, k, v, beta, g):
    B, S, H, DK = q.shape
    DV = v.shape[-1]
    C = GDN_CHUNK
    N = S // C

    def to_chunks(t):
        return t.reshape(B, N, C, H, t.shape[-1]).transpose(0, 3, 1, 2, 4)

    q, k, v = to_chunks(q), to_chunks(k), to_chunks(v)
    beta = beta.reshape(B, N, C, H).transpose(0, 3, 1, 2)
    g = g.reshape(B, N, C, H).transpose(0, 3, 1, 2)
    gc = jnp.cumsum(g, axis=-1)
    idx = jnp.arange(C)
    incl = idx[:, None] >= idx[None, :]
    strict = idx[:, None] > idx[None, :]
    diff = gc[..., :, None] - gc[..., None, :]
    decay = jnp.where(incl, jnp.exp(jnp.where(incl, diff, 0.0)), 0.0)
    kb = k * beta[..., None]
    vb = v * beta[..., None]
    L = jnp.where(strict, jnp.einsum('bhncd,bhnsd->bhncs', kb, k) * decay, 0.0)
    eye = jnp.eye(C, dtype=jnp.float32)
    T = lax.linalg.triangular_solve(eye + L, jnp.broadcast_to(eye, L.shape), left_side=True, lower=True)
    u = T @ vb
    w = T @ (kb * jnp.exp(gc)[..., None])
    a_intra = jnp.einsum('bhncd,bhnsd->bhncs', q, k) * decay
    q_g = q * jnp.exp(gc)[..., None]
    k_g = k * jnp.exp(gc[..., -1:] - gc)[..., None]
    g_last = jnp.exp(gc[..., -1])
    xs = (jnp.moveaxis(u, 2, 0), jnp.moveaxis(w, 2, 0), jnp.moveaxis(a_intra, 2, 0),
          jnp.moveaxis(q_g, 2, 0), jnp.moveaxis(k_g, 2, 0), jnp.moveaxis(g_last, 2, 0))

    def step(state, inp):
        u_n, w_n, a_n, qg_n, kg_n, gl_n = inp
        v_new = u_n - w_n @ state
        o_n = qg_n @ state + a_n @ v_new
        state = state * gl_n[..., None, None] + jnp.swapaxes(kg_n, -1, -2) @ v_new
        return state, o_n

    state0 = jnp.zeros((B, H, DK, DV), jnp.float32)
    _, o = lax.scan(step, state0, xs)
    return o.transpose(1, 0, 3, 2, 4).reshape(B, S, H, DV)


def gdn_mixer(h, w_in, conv_w, A_log, dt_bias, norm_g, w_out):
    B, S, _ = h.shape
    H = GDN_HEADS
    proj = h @ w_in
    qkv = proj[..., :GDN_QKV_DIM]
    z = proj[..., GDN_QKV_DIM:GDN_QKV_DIM + H * GDN_DV]
    b = proj[..., GDN_QKV_DIM + H * GDN_DV:GDN_QKV_DIM + H * GDN_DV + 2 * H]
    a = proj[..., GDN_QKV_DIM + H * GDN_DV + 2 * H:]
    qkv = jax.nn.silu(centred_short_conv(qkv, conv_w))
    q = l2norm(qkv[..., :H * GDN_DK].reshape(B, S, H, GDN_DK)) * (GDN_DK ** -0.5)
    k = l2norm(qkv[..., H * GDN_DK:2 * H * GDN_DK].reshape(B, S, H, GDN_DK))
    v = qkv[..., 2 * H * GDN_DK:].reshape(B, S, H, GDN_DV).astype(jnp.float32)
    beta = jax.nn.sigmoid(b.astype(jnp.float32)).reshape(B, S, 2, H)
    g = -jnp.exp(A_log.astype(jnp.float32)) * jax.nn.softplus(
        a.astype(jnp.float32).reshape(B, S, 2, H) + dt_bias.astype(jnp.float32))
    o_fwd = chunk_gated_delta(q, k, v, beta[:, :, 0], g[:, :, 0])
    o_bwd = jnp.flip(chunk_gated_delta(jnp.flip(q, 1), jnp.flip(k, 1), jnp.flip(v, 1),
                                       jnp.flip(beta[:, :, 1], 1), jnp.flip(g[:, :, 1], 1)), 1)
    o = (o_fwd + o_bwd).astype(h.dtype)
    o = rmsnorm(o, norm_g) * jax.nn.silu(z.reshape(B, S, H, GDN_DV))
    return o.reshape(B, S, H * GDN_DV) @ w_out


def expert_choice_moe(h, w_router, w_gate, w_up, w_down):
    B, S, D = h.shape
    T = B * S
    cap = EC_CAPACITY_FACTOR * T // N_EXPERTS
    xt = h.reshape(T, D)
    aff = jax.nn.softmax((xt @ w_router).astype(jnp.float32), axis=-1)
    gate, idx = lax.top_k(aff.T, cap)
    xe = jnp.take(xt, idx, axis=0)
    hid = jax.nn.silu(jnp.einsum('ecd,edf->ecf', xe, w_gate)) * jnp.einsum('ecd,edf->ecf', xe, w_up)
    ye = jnp.einsum('ecf,efd->ecd', hid, w_down) * gate[..., None].astype(h.dtype)
    out = jnp.zeros_like(xt).at[idx.reshape(-1)].add(ye.reshape(-1, D))
    return out.reshape(B, S, D)


def encoder_trunk(x, p, W):
    h = x
    for i in range(DEPTH):
        hn = rmsnorm(h, W['norm_mix_g'][i])
        j = i // N_MIXERS
        if i % N_MIXERS == 0:
            h = h + mla_mixer(hn, W['mla_w_in'][j], W['mla_q_norm_g'][j], W['mla_w_qb'][j],
                              W['mla_kv_norm_g'][j], W['mla_w_kvb'][j], W['mla_w_out'][j])
        else:
            h = h + gdn_mixer(hn, W['gdn_w_in'][j], W['gdn_conv_w'][j], W['gdn_A_log'][j],
                              W['gdn_dt_bias'][j], W['gdn_norm_g'][j], W['gdn_w_out'][j])
        h = h + expert_choice_moe(rmsnorm(h, W['norm_ffn_g'][i]), W['moe_w_router'][i],
                                  W['moe_w_gate'][i], W['moe_w_up'][i], W['moe_w_down'][i])
        gate = jax.nn.sigmoid(rmsnorm(h, W['norm_ple_g'][i]) @ W['ple_w_gate'][i])
        h = h + (p[i] @ W['ple_w_proj'][i]) * gate
    return rmsnorm(h, W['final_norm_g'])


def setup_inputs(seed: int = 0) -> dict:
    key = jax.random.key(seed)
    ks = jax.random.split(key, 26)
    f32 = jnp.float32

    def nrm(k, shape, fan_in):
        return jax.random.normal(k, shape, f32) * (fan_in ** -0.5)

    def gain(k, shape):
        return 1.0 + 0.05 * jax.random.normal(k, shape, f32)

    dt = jnp.exp(jax.random.uniform(ks[17], (N_GDN_LAYERS, 2, GDN_HEADS), f32,
                                    float(np.log(1e-3)), float(np.log(1e-1))))
    return {
        'x_prompt': jax.random.normal(ks[0], (BATCH, SEQ, D_MODEL), f32),
        'x_sample': jax.random.normal(ks[1], (DEC_BATCH, DEC_SEQ, D_MODEL), f32),
        'p_prompt': jax.random.normal(ks[2], (DEPTH, BATCH, SEQ, PLE_DIM), f32),
        'p_sample': jax.random.normal(ks[3], (DEPTH, DEC_BATCH, DEC_SEQ, PLE_DIM), f32),
        'norm_mix_g': gain(ks[4], (DEPTH, D_MODEL)),
        'norm_ffn_g': gain(ks[5], (DEPTH, D_MODEL)),
        'norm_ple_g': gain(ks[6], (DEPTH, D_MODEL)),
        'final_norm_g': gain(ks[7], (D_MODEL,)),
        'mla_w_in': nrm(ks[8], (N_MLA_LAYERS, D_MODEL, MLA_IN_DIM), D_MODEL),
        'mla_q_norm_g': gain(ks[9], (N_MLA_LAYERS, MLA_Q_LORA)),
        'mla_w_qb': nrm(ks[10], (N_MLA_LAYERS, MLA_Q_LORA, MLA_HEADS * (MLA_NOPE_DIM + MLA_ROPE_DIM)), MLA_Q_LORA),
        'mla_kv_norm_g': gain(ks[11], (N_MLA_LAYERS, MLA_KV_LORA)),
        'mla_w_kvb': nrm(ks[12], (N_MLA_LAYERS, MLA_KV_LORA, MLA_HEADS * (MLA_NOPE_DIM + MLA_V_DIM)), MLA_KV_LORA),
        'mla_w_out': nrm(ks[13], (N_MLA_LAYERS, MLA_HEADS * MLA_V_DIM, D_MODEL), MLA_HEADS * MLA_V_DIM),
        'gdn_w_in': nrm(ks[14], (N_GDN_LAYERS, D_MODEL, GDN_IN_DIM), D_MODEL),
        'gdn_conv_w': nrm(ks[15], (N_GDN_LAYERS, GDN_CONV, GDN_QKV_DIM), GDN_CONV),
        'gdn_A_log': jnp.log(jax.random.uniform(ks[16], (N_GDN_LAYERS, 2, GDN_HEADS), f32, 1.0, 16.0)),
        'gdn_dt_bias': dt + jnp.log(-jnp.expm1(-dt)),
        'gdn_norm_g': gain(ks[18], (N_GDN_LAYERS, GDN_DV)),
        'gdn_w_out': nrm(ks[19], (N_GDN_LAYERS, GDN_HEADS * GDN_DV, D_MODEL), GDN_HEADS * GDN_DV),
        'moe_w_router': nrm(ks[20], (DEPTH, D_MODEL, N_EXPERTS), D_MODEL),
        'moe_w_gate': nrm(ks[21], (DEPTH, N_EXPERTS, D_MODEL, D_EXPERT), D_MODEL),
        'moe_w_up': nrm(ks[22], (DEPTH, N_EXPERTS, D_MODEL, D_EXPERT), D_MODEL),
        'moe_w_down': nrm(ks[23], (DEPTH, N_EXPERTS, D_EXPERT, D_MODEL), D_EXPERT),
        'ple_w_proj': nrm(ks[24], (DEPTH, PLE_DIM, D_MODEL), PLE_DIM),
        'ple_w_gate': nrm(ks[25], (DEPTH, D_MODEL, D_MODEL), D_MODEL),
    }


def reference(x_prompt, x_sample, p_prompt, p_sample, norm_mix_g, norm_ffn_g, norm_ple_g, final_norm_g,
              mla_w_in, mla_q_norm_g, mla_w_qb, mla_kv_norm_g, mla_w_kvb, mla_w_out,
              gdn_w_in, gdn_conv_w, gdn_A_log, gdn_dt_bias, gdn_norm_g, gdn_w_out,
              moe_w_router, moe_w_gate, moe_w_up, moe_w_down, ple_w_proj, ple_w_gate):
    W = {
        'norm_mix_g': norm_mix_g, 'norm_ffn_g': norm_ffn_g, 'norm_ple_g': norm_ple_g,
        'final_norm_g': final_norm_g,
        'mla_w_in': mla_w_in, 'mla_q_norm_g': mla_q_norm_g, 'mla_w_qb': mla_w_qb,
        'mla_kv_norm_g': mla_kv_norm_g, 'mla_w_kvb': mla_w_kvb, 'mla_w_out': mla_w_out,
        'gdn_w_in': gdn_w_in, 'gdn_conv_w': gdn_conv_w, 'gdn_A_log': gdn_A_log,
        'gdn_dt_bias': gdn_dt_bias, 'gdn_norm_g': gdn_norm_g, 'gdn_w_out': gdn_w_out,
        'moe_w_router': moe_w_router, 'moe_w_gate': moe_w_gate, 'moe_w_up': moe_w_up,
        'moe_w_down': moe_w_down, 'ple_w_proj': ple_w_proj, 'ple_w_gate': ple_w_gate,
    }
    y_prompt = encoder_trunk(x_prompt, p_prompt, W)
    y_sample = encoder_trunk(x_sample, p_sample, W)
    return (y_prompt, y_sample)
```

```python
import functools

import jax
import jax.numpy as jnp
from jax import lax
from jax.experimental import pallas as pl
from jax.experimental.pallas import tpu as pltpu

D_MODEL = 1024
DEPTH = 4
PLE_DIM = 256
N_MIXERS = 2
MLA_HEADS = 8
MLA_NOPE_DIM = 128
MLA_ROPE_DIM = 64
MLA_V_DIM = 128
MLA_Q_LORA = 256
MLA_KV_LORA = 128
ROPE_THETA = 10000.0
Q_BLOCK = 128
GDN_HEADS = 8
GDN_DK = 128
GDN_DV = 128
GDN_CONV = 5
GDN_CHUNK = 64
GDN_QKV_DIM = 2 * GDN_HEADS * GDN_DK + GDN_HEADS * GDN_DV
N_EXPERTS = 16
EC_CAPACITY_FACTOR = 2
NORM_EPS = 1e-6


def rmsnorm(x, g):
    x32 = x.astype(jnp.float32)
    y = x32 * lax.rsqrt(jnp.mean(x32 * x32, axis=-1, keepdims=True) + NORM_EPS)
    return (y * g.astype(jnp.float32)).astype(x.dtype)


def _rmsnorm_kernel(x_ref, g_ref, o_ref):
    x = x_ref[...]
    ms = jnp.mean(x * x, axis=-1, keepdims=True)
    o_ref[...] = x * lax.rsqrt(ms + NORM_EPS) * g_ref[...]


def pallas_rmsnorm(x, g, block_rows=1024):
    T, D = x.shape
    return pl.pallas_call(
        _rmsnorm_kernel,
        grid=(T // block_rows,),
        in_specs=[pl.BlockSpec((block_rows, D), lambda i: (i, 0)),
                  pl.BlockSpec((1, D), lambda i: (0, 0))],
        out_specs=pl.BlockSpec((block_rows, D), lambda i: (i, 0)),
        out_shape=jax.ShapeDtypeStruct((T, D), x.dtype),
        name="final_rmsnorm",
    )(x, g.reshape(1, D))


def l2norm(x):
    x32 = x.astype(jnp.float32)
    return x32 * lax.rsqrt(jnp.sum(x32 * x32, axis=-1, keepdims=True) + NORM_EPS)


def rope_tables(seq):
    pos = jnp.arange(seq, dtype=jnp.float32)
    inv = ROPE_THETA ** (-jnp.arange(0, MLA_ROPE_DIM, 2, dtype=jnp.float32) / MLA_ROPE_DIM)
    ang = pos[:, None] * inv[None, :]
    return jnp.cos(ang), jnp.sin(ang)


def apply_rope(x, cos, sin):
    half = x.shape[-1] // 2
    x1, x2 = x[..., :half], x[..., half:]
    return jnp.concatenate([x1 * cos - x2 * sin, x2 * cos + x1 * sin], axis=-1)


def mla_mixer(h, w_in, q_norm_g, w_qb, kv_norm_g, w_kvb, w_out):
    B, S, _ = h.shape
    H = MLA_HEADS
    lat = h @ w_in
    q_lat = lat[..., :MLA_Q_LORA]
    kv_lat = lat[..., MLA_Q_LORA:MLA_Q_LORA + MLA_KV_LORA]
    k_rope = lat[..., MLA_Q_LORA + MLA_KV_LORA:]
    q = (rmsnorm(q_lat, q_norm_g) @ w_qb).reshape(B, S, H, MLA_NOPE_DIM + MLA_ROPE_DIM)
    q_nope, q_rope = q[..., :MLA_NOPE_DIM], q[..., MLA_NOPE_DIM:]
    kv = (rmsnorm(kv_lat, kv_norm_g) @ w_kvb).reshape(B, S, H, MLA_NOPE_DIM + MLA_V_DIM)
    k_nope, v = kv[..., :MLA_NOPE_DIM], kv[..., MLA_NOPE_DIM:]
    cos, sin = rope_tables(S)
    q_rope = apply_rope(q_rope, cos[:, None, :], sin[:, None, :])
    k_rope = apply_rope(k_rope, cos, sin)
    scale = (MLA_NOPE_DIM + MLA_ROPE_DIM) ** -0.5
    nb = S // Q_BLOCK
    qn_b = q_nope.reshape(B, nb, Q_BLOCK, H, MLA_NOPE_DIM).transpose(1, 0, 2, 3, 4)
    qr_b = q_rope.reshape(B, nb, Q_BLOCK, H, MLA_ROPE_DIM).transpose(1, 0, 2, 3, 4)

    def query_block(args):
        qn, qr = args
        s = jnp.einsum('bqhd,bkhd->bhqk', qn, k_nope) + jnp.einsum('bqhr,bkr->bhqk', qr, k_rope)
        pr = jax.nn.softmax(s.astype(jnp.float32) * scale, axis=-1).astype(v.dtype)
        return jnp.einsum('bhqk,bkhd->bqhd', pr, v)

    o = lax.map(query_block, (qn_b, qr_b))
    o = o.transpose(1, 0, 2, 3, 4).reshape(B, S, H * MLA_V_DIM)
    return o @ w_out


def centred_short_conv(x, w):
    S = x.shape[1]
    pad = GDN_CONV // 2
    xp = jnp.pad(x, ((0, 0), (pad, pad), (0, 0)))
    out = xp[:, 0:S, :] * w[0]
    for j in range(1, GDN_CONV):
        out = out + xp[:, j:j + S, :] * w[j]
    return out


def chunk_gated_delta(q, k, v, beta, g):
    B, S, H, DK = q.shape
    DV = v.shape[-1]
    C = GDN_CHUNK
    N = S // C

    def to_chunks(t):
        return t.reshape(B, N, C, H, t.shape[-1]).transpose(0, 3, 1, 2, 4)

    q, k, v = to_chunks(q), to_chunks(k), to_chunks(v)
    beta = beta.reshape(B, N, C, H).transpose(0, 3, 1, 2)
    g = g.reshape(B, N, C, H).transpose(0, 3, 1, 2)
    gc = jnp.cumsum(g, axis=-1)
    idx = jnp.arange(C)
    incl = idx[:, None] >= idx[None, :]
    strict = idx[:, None] > idx[None, :]
    diff = gc[..., :, None] - gc[..., None, :]
    decay = jnp.where(incl, jnp.exp(jnp.where(incl, diff, 0.0)), 0.0)
    kb = k * beta[..., None]
    vb = v * beta[..., None]
    L = jnp.where(strict, jnp.einsum('bhncd,bhnsd->bhncs', kb, k) * decay, 0.0)
    eye = jnp.eye(C, dtype=jnp.float32)
    T = lax.linalg.triangular_solve(eye + L, jnp.broadcast_to(eye, L.shape), left_side=True, lower=True)
    u = T @ vb
    w = T @ (kb * jnp.exp(gc)[..., None])
    a_intra = jnp.einsum('bhncd,bhnsd->bhncs', q, k) * decay
    q_g = q * jnp.exp(gc)[..., None]
    k_g = k * jnp.exp(gc[..., -1:] - gc)[..., None]
    g_last = jnp.exp(gc[..., -1])
    xs = (jnp.moveaxis(u, 2, 0), jnp.moveaxis(w, 2, 0), jnp.moveaxis(a_intra, 2, 0),
          jnp.moveaxis(q_g, 2, 0), jnp.moveaxis(k_g, 2, 0), jnp.moveaxis(g_last, 2, 0))

    def step(state, inp):
        u_n, w_n, a_n, qg_n, kg_n, gl_n = inp
        v_new = u_n - w_n @ state
        o_n = qg_n @ state + a_n @ v_new
        state = state * gl_n[..., None, None] + jnp.swapaxes(kg_n, -1, -2) @ v_new
        return state, o_n

    state0 = jnp.zeros((B, H, DK, DV), jnp.float32)
    _, o = lax.scan(step, state0, xs)
    return o.transpose(1, 0, 3, 2, 4).reshape(B, S, H, DV)


def gdn_mixer(h, w_in, conv_w, A_log, dt_bias, norm_g, w_out):
    B, S, _ = h.shape
    H = GDN_HEADS
    proj = h @ w_in
    qkv = proj[..., :GDN_QKV_DIM]
    z = proj[..., GDN_QKV_DIM:GDN_QKV_DIM + H * GDN_DV]
    b = proj[..., GDN_QKV_DIM + H * GDN_DV:GDN_QKV_DIM + H * GDN_DV + 2 * H]
    a = proj[..., GDN_QKV_DIM + H * GDN_DV + 2 * H:]
    qkv = jax.nn.silu(centred_short_conv(qkv, conv_w))
    q = l2norm(qkv[..., :H * GDN_DK].reshape(B, S, H, GDN_DK)) * (GDN_DK ** -0.5)
    k = l2norm(qkv[..., H * GDN_DK:2 * H * GDN_DK].reshape(B, S, H, GDN_DK))
    v = qkv[..., 2 * H * GDN_DK:].reshape(B, S, H, GDN_DV).astype(jnp.float32)
    beta = jax.nn.sigmoid(b.astype(jnp.float32)).reshape(B, S, 2, H)
    g = -jnp.exp(A_log.astype(jnp.float32)) * jax.nn.softplus(
        a.astype(jnp.float32).reshape(B, S, 2, H) + dt_bias.astype(jnp.float32))
    o_fwd = chunk_gated_delta(q, k, v, beta[:, :, 0], g[:, :, 0])
    o_bwd = jnp.flip(chunk_gated_delta(jnp.flip(q, 1), jnp.flip(k, 1), jnp.flip(v, 1),
                                       jnp.flip(beta[:, :, 1], 1), jnp.flip(g[:, :, 1], 1)), 1)
    o = (o_fwd + o_bwd).astype(h.dtype)
    o = rmsnorm(o, norm_g) * jax.nn.silu(z.reshape(B, S, H, GDN_DV))
    return o.reshape(B, S, H * GDN_DV) @ w_out


def expert_choice_moe(h, w_router, w_gate, w_up, w_down):
    B, S, D = h.shape
    T = B * S
    cap = EC_CAPACITY_FACTOR * T // N_EXPERTS
    xt = h.reshape(T, D)
    aff = jax.nn.softmax((xt @ w_router).astype(jnp.float32), axis=-1)
    gate, idx = lax.top_k(aff.T, cap)
    xe = jnp.take(xt, idx, axis=0)
    hid = jax.nn.silu(jnp.einsum('ecd,edf->ecf', xe, w_gate)) * jnp.einsum('ecd,edf->ecf', xe, w_up)
    ye = jnp.einsum('ecf,efd->ecd', hid, w_down) * gate[..., None].astype(h.dtype)
    out = jnp.zeros_like(xt).at[idx.reshape(-1)].add(ye.reshape(-1, D))
    return out.reshape(B, S, D)


def encoder_trunk(x, p, W):
    h = x
    for i in range(DEPTH):
        hn = rmsnorm(h, W['norm_mix_g'][i])
        j = i // N_MIXERS
        if i % N_MIXERS == 0:
            h = h + mla_mixer(hn, W['mla_w_in'][j], W['mla_q_norm_g'][j], W['mla_w_qb'][j],
                              W['mla_kv_norm_g'][j], W['mla_w_kvb'][j], W['mla_w_out'][j])
        else:
            h = h + gdn_mixer(hn, W['gdn_w_in'][j], W['gdn_conv_w'][j], W['gdn_A_log'][j],
                              W['gdn_dt_bias'][j], W['gdn_norm_g'][j], W['gdn_w_out'][j])
        h = h + expert_choice_moe(rmsnorm(h, W['norm_ffn_g'][i]), W['moe_w_router'][i],
                                  W['moe_w_gate'][i], W['moe_w_up'][i], W['moe_w_down'][i])
        gate = jax.nn.sigmoid(rmsnorm(h, W['norm_ple_g'][i]) @ W['ple_w_gate'][i])
        h = h + (p[i] @ W['ple_w_proj'][i]) * gate
    B, S, D = h.shape
    return pallas_rmsnorm(h.reshape(B * S, D), W['final_norm_g']).reshape(B, S, D)


def kernel(x_prompt, x_sample, p_prompt, p_sample, norm_mix_g, norm_ffn_g, norm_ple_g, final_norm_g,
           mla_w_in, mla_q_norm_g, mla_w_qb, mla_kv_norm_g, mla_w_kvb, mla_w_out,
           gdn_w_in, gdn_conv_w, gdn_A_log, gdn_dt_bias, gdn_norm_g, gdn_w_out,
           moe_w_router, moe_w_gate, moe_w_up, moe_w_down, ple_w_proj, ple_w_gate):
    W = {
        'norm_mix_g': norm_mix_g, 'norm_ffn_g': norm_ffn_g, 'norm_ple_g': norm_ple_g,
        'final_norm_g': final_norm_g,
        'mla_w_in': mla_w_in, 'mla_q_norm_g': mla_q_norm_g, 'mla_w_qb': mla_w_qb,
        'mla_kv_norm_g': mla_kv_norm_g, 'mla_w_kvb': mla_w_kvb, 'mla_w_out': mla_w_out,
        'gdn_w_in': gdn_w_in, 'gdn_conv_w': gdn_conv_w, 'gdn_A_log': gdn_A_log,
        'gdn_dt_bias': gdn_dt_bias, 'gdn_norm_g': gdn_norm_g, 'gdn_w_out': gdn_w_out,
        'moe_w_router': moe_w_router, 'moe_w_gate': moe_w_gate, 'moe_w_up': moe_w_up,
        'moe_w_down': moe_w_down, 'ple_w_proj': ple_w_proj, 'ple_w_gate': ple_w_gate,
    }
    y_prompt = encoder_trunk(x_prompt, p_prompt, W)
    y_sample = encoder_trunk(x_sample, p_sample, W)
    return (y_prompt, y_sample)
```

```python
import functools

import jax
import jax.numpy as jnp
from jax import lax
from jax.experimental import pallas as pl
from jax.experimental.pallas import tpu as pltpu

D_MODEL = 1024
DEPTH = 4
PLE_DIM = 256
N_MIXERS = 2
MLA_HEADS = 8
MLA_NOPE_DIM = 128
MLA_ROPE_DIM = 64
MLA_V_DIM = 128
MLA_Q_LORA = 256
MLA_KV_LORA = 128
ROPE_THETA = 10000.0
Q_BLOCK = 128
GDN_HEADS = 8
GDN_DK = 128
GDN_DV = 128
GDN_CONV = 5
GDN_CHUNK = 64
GDN_QKV_DIM = 2 * GDN_HEADS * GDN_DK + GDN_HEADS * GDN_DV
N_EXPERTS = 16
EC_CAPACITY_FACTOR = 2
NORM_EPS = 1e-6


def rmsnorm(x, g):
    x32 = x.astype(jnp.float32)
    y = x32 * lax.rsqrt(jnp.mean(x32 * x32, axis=-1, keepdims=True) + NORM_EPS)
    return (y * g.astype(jnp.float32)).astype(x.dtype)


def _rmsnorm_kernel(x_ref, g_ref, o_ref):
    x = x_ref[...]
    ms = jnp.mean(x * x, axis=-1, keepdims=True)
    o_ref[...] = x * lax.rsqrt(ms + NORM_EPS) * g_ref[...]


def pallas_rmsnorm(x, g, block_rows=1024):
    T, D = x.shape
    return pl.pallas_call(
        _rmsnorm_kernel,
        grid=(T // block_rows,),
        in_specs=[pl.BlockSpec((block_rows, D), lambda i: (i, 0)),
                  pl.BlockSpec((1, D), lambda i: (0, 0))],
        out_specs=pl.BlockSpec((block_rows, D), lambda i: (i, 0)),
        out_shape=jax.ShapeDtypeStruct((T, D), x.dtype),
        name="final_rmsnorm",
    )(x, g.reshape(1, D))


def l2norm(x):
    x32 = x.astype(jnp.float32)
    return x32 * lax.rsqrt(jnp.sum(x32 * x32, axis=-1, keepdims=True) + NORM_EPS)


def rope_tables(seq):
    pos = jnp.arange(seq, dtype=jnp.float32)
    inv = ROPE_THETA ** (-jnp.arange(0, MLA_ROPE_DIM, 2, dtype=jnp.float32) / MLA_ROPE_DIM)
    ang = pos[:, None] * inv[None, :]
    return jnp.cos(ang), jnp.sin(ang)


def apply_rope(x, cos, sin):
    half = x.shape[-1] // 2
    x1, x2 = x[..., :half], x[..., half:]
    return jnp.concatenate([x1 * cos - x2 * sin, x2 * cos + x1 * sin], axis=-1)


def mla_mixer(h, w_in, q_norm_g, w_qb, kv_norm_g, w_kvb, w_out):
    B, S, _ = h.shape
    H = MLA_HEADS
    lat = h @ w_in
    q_lat = lat[..., :MLA_Q_LORA]
    kv_lat = lat[..., MLA_Q_LORA:MLA_Q_LORA + MLA_KV_LORA]
    k_rope = lat[..., MLA_Q_LORA + MLA_KV_LORA:]
    q = (rmsnorm(q_lat, q_norm_g) @ w_qb).reshape(B, S, H, MLA_NOPE_DIM + MLA_ROPE_DIM)
    q_nope, q_rope = q[..., :MLA_NOPE_DIM], q[..., MLA_NOPE_DIM:]
    kv = (rmsnorm(kv_lat, kv_norm_g) @ w_kvb).reshape(B, S, H, MLA_NOPE_DIM + MLA_V_DIM)
    k_nope, v = kv[..., :MLA_NOPE_DIM], kv[..., MLA_NOPE_DIM:]
    cos, sin = rope_tables(S)
    q_rope = apply_rope(q_rope, cos[:, None, :], sin[:, None, :])
    k_rope = apply_rope(k_rope, cos, sin)
    scale = (MLA_NOPE_DIM + MLA_ROPE_DIM) ** -0.5
    nb = S // Q_BLOCK
    qn_b = q_nope.reshape(B, nb, Q_BLOCK, H, MLA_NOPE_DIM).transpose(1, 0, 2, 3, 4)
    qr_b = q_rope.reshape(B, nb, Q_BLOCK, H, MLA_ROPE_DIM).transpose(1, 0, 2, 3, 4)

    def query_block(args):
        qn, qr = args
        s = jnp.einsum('bqhd,bkhd->bhqk', qn, k_nope) + jnp.einsum('bqhr,bkr->bhqk', qr, k_rope)
        pr = jax.nn.softmax(s.astype(jnp.float32) * scale, axis=-1).astype(v.dtype)
        return jnp.einsum('bhqk,bkhd->bqhd', pr, v)

    o = lax.map(query_block, (qn_b, qr_b))
    o = o.transpose(1, 0, 2, 3, 4).reshape(B, S, H * MLA_V_DIM)
    return o @ w_out


def centred_short_conv(x, w):
    S = x.shape[1]
    pad = GDN_CONV // 2
    xp = jnp.pad(x, ((0, 0), (pad, pad), (0, 0)))
    out = xp[:, 0:S, :] * w[0]
    for j in range(1, GDN_CONV):
        out = out + xp[:, j:j + S, :] * w[j]
    return out


def _bdot(a, b):
    return jnp.dot(a.astype(jnp.bfloat16), b.astype(jnp.bfloat16), preferred_element_type=jnp.float32)


def _bdot_nt(a, b):
    return lax.dot_general(a.astype(jnp.bfloat16), b.astype(jnp.bfloat16),
                           (((1,), (1,)), ((), ())), preferred_element_type=jnp.float32)


def _unit_triangular_inverses(Ls, row, col):
    C = Ls[0].shape[0]
    eye = (row == col).astype(jnp.float32)

    def same_block(s):
        return (row // s) == (col // s)

    blk8 = same_block(8)
    L8 = [jnp.where(blk8, L, 0.0) for L in Ls]
    P2 = [_bdot(a, a) for a in L8]
    P4 = [_bdot(a, a) for a in P2]
    X = [eye - a for a in L8]
    X = [x + _bdot(x, p) for x, p in zip(X, P2)]
    X = [x + _bdot(x, p) for x, p in zip(X, P4)]
    s = 8
    while s < C:
        pair = same_block(2 * s) & jnp.logical_not(same_block(s))
        XC = [_bdot(x, jnp.where(pair, L, 0.0)) for x, L in zip(X, Ls)]
        X = [x - _bdot(xc, x) for x, xc in zip(X, XC)]
        s *= 2
    return X


def _gdn_chunk_kernel(qf_ref, kf_ref, vf_ref, gcf_ref, bcf_ref, grf_ref,
                      qb_ref, kb_ref, vb_ref, gcb_ref, bcb_ref, grb_ref,
                      of_ref, ob_ref, state_ref):
    H, C, DK, DV = GDN_HEADS, GDN_CHUNK, GDN_DK, GDN_DV

    @pl.when(pl.program_id(1) == 0)
    def _():
        state_ref[...] = jnp.zeros_like(state_ref)

    row = lax.broadcasted_iota(jnp.int32, (C, C), 0)
    col = lax.broadcasted_iota(jnp.int32, (C, C), 1)
    dirs = ((0, qf_ref, kf_ref, vf_ref, gcf_ref, bcf_ref, grf_ref, of_ref),
            (1, qb_ref, kb_ref, vb_ref, gcb_ref, bcb_ref, grb_ref, ob_ref))
    chains = []
    for d, q_ref, k_ref, v_ref, gc_ref, bc_ref, gr_ref, o_ref in dirs:
        if d == 0:
            incl, strict, last = row >= col, row > col, C - 1
        else:
            incl, strict, last = row <= col, row < col, 0
        tri = incl.astype(jnp.float32)
        gc_cols = jnp.dot(tri, gc_ref[0], precision=lax.Precision.HIGHEST,
                          preferred_element_type=jnp.float32)
        gc_rows = jnp.dot(gr_ref[0, 0], tri.T, precision=lax.Precision.HIGHEST,
                          preferred_element_type=jnp.float32)
        beta_cols = bc_ref[0]
        for h in range(H):
            c = d * H + h
            chains.append(dict(
                c=c, incl=incl, strict=strict, o_ref=o_ref, sl=slice(h * DK, (h + 1) * DK),
                q_ref=q_ref, k_ref=k_ref, v_ref=v_ref,
                beta=beta_cols[:, c:c + 1], gcc=gc_cols[:, c:c + 1], gcr=gc_rows[c:c + 1, :],
                gl=gc_cols[last:last + 1, c:c + 1]))

    for ch in chains:
        ch['decay'] = jnp.where(ch['incl'], jnp.exp(jnp.where(ch['incl'], ch['gcc'] - ch['gcr'], 0.0)), 0.0)
        ch['eg'] = jnp.exp(ch['gcc'])
    for ch in chains:
        k = ch['k_ref'][0, :, ch['sl']]
        q = ch['q_ref'][0, :, ch['sl']]
        kb = k * ch['beta']
        ch['kbeg'] = kb * ch['eg']
        ch['s1'] = _bdot_nt(jnp.concatenate([kb, q], axis=0), k)
    Ls = [jnp.where(ch['strict'], ch['s1'][:C] * ch['decay'], 0.0) for ch in chains]
    Ts = _unit_triangular_inverses(Ls, row, col)
    for ch, T in zip(chains, Ts):
        v = ch['v_ref'][0, :, ch['sl']]
        ch['uw'] = _bdot(T, jnp.concatenate([v * ch['beta'], ch['kbeg']], axis=1))
    for ch in chains:
        q = ch['q_ref'][0, :, ch['sl']]
        w = ch['uw'][:, DV:]
        ch['ws'] = _bdot(jnp.concatenate([w, q * ch['eg']], axis=0), state_ref[ch['c']])
    for ch in chains:
        ch['v_new'] = ch['uw'][:, :DV] - ch['ws'][:C]
        a_intra = ch['s1'][C:] * ch['decay']
        ch['o_ref'][0, :, ch['sl']] = ch['ws'][C:] + _bdot(a_intra, ch['v_new'])
    for ch in chains:
        k = ch['k_ref'][0, :, ch['sl']]
        k_g = k * jnp.exp(ch['gl'] - ch['gcc'])
        c = ch['c']
        state_ref[c] = state_ref[c] * jnp.exp(ch['gl']) + _bdot(k_g.T, ch['v_new'])


def gdn_bidirectional_delta(q, k, v, beta, g):
    B, S, HD = q.shape
    C, H = GDN_CHUNK, GDN_HEADS
    N = S // C
    g_rows = g.reshape(B, N, C, 2 * H).transpose(0, 1, 3, 2)

    def fwd3(b, n):
        return (b, n, 0)

    def bwd3(b, n):
        return (b, N - 1 - n, 0)

    def fwd4(b, n):
        return (b, n, 0, 0)

    def bwd4(b, n):
        return (b, N - 1 - n, 0, 0)

    def specs(i3, i4):
        big = pl.BlockSpec((1, C, HD), i3)
        small = pl.BlockSpec((1, C, 2 * H), i3)
        return [big, big, big, small, small, pl.BlockSpec((1, 1, 2 * H, C), i4)]

    out = jax.ShapeDtypeStruct((B, S, HD), jnp.float32)
    return pl.pallas_call(
        _gdn_chunk_kernel,
        grid=(B, N),
        in_specs=specs(fwd3, fwd4) + specs(bwd3, bwd4),
        out_specs=[pl.BlockSpec((1, C, HD), fwd3), pl.BlockSpec((1, C, HD), bwd3)],
        out_shape=[out, out],
        scratch_shapes=[pltpu.VMEM((2 * H, GDN_DK, GDN_DV), jnp.float32)],
        compiler_params=pltpu.CompilerParams(dimension_semantics=("parallel", "arbitrary")),
        name="gdn_chunk_delta",
    )(q, k, v, g, beta, g_rows, q, k, v, g, beta, g_rows)


def gdn_mixer(h, w_in, conv_w, A_log, dt_bias, norm_g, w_out):
    B, S, _ = h.shape
    H = GDN_HEADS
    proj = h @ w_in
    qkv = proj[..., :GDN_QKV_DIM]
    z = proj[..., GDN_QKV_DIM:GDN_QKV_DIM + H * GDN_DV]
    b = proj[..., GDN_QKV_DIM + H * GDN_DV:GDN_QKV_DIM + H * GDN_DV + 2 * H]
    a = proj[..., GDN_QKV_DIM + H * GDN_DV + 2 * H:]
    qkv = jax.nn.silu(centred_short_conv(qkv, conv_w))
    q = l2norm(qkv[..., :H * GDN_DK].reshape(B, S, H, GDN_DK)) * (GDN_DK ** -0.5)
    k = l2norm(qkv[..., H * GDN_DK:2 * H * GDN_DK].reshape(B, S, H, GDN_DK))
    v = qkv[..., 2 * H * GDN_DK:].astype(jnp.float32)
    beta = jax.nn.sigmoid(b.astype(jnp.float32))
    g = -jnp.exp(A_log.astype(jnp.float32)) * jax.nn.softplus(
        a.astype(jnp.float32).reshape(B, S, 2, H) + dt_bias.astype(jnp.float32))
    o_fwd, o_bwd = gdn_bidirectional_delta(q.reshape(B, S, H * GDN_DK), k.reshape(B, S, H * GDN_DK), v,
                                           beta, g.reshape(B, S, 2 * H))
    o = (o_fwd + o_bwd).astype(h.dtype).reshape(B, S, H, GDN_DV)
    o = rmsnorm(o, norm_g) * jax.nn.silu(z.reshape(B, S, H, GDN_DV))
    return o.reshape(B, S, H * GDN_DV) @ w_out


def expert_choice_moe(h, w_router, w_gate, w_up, w_down):
    B, S, D = h.shape
    T = B * S
    cap = EC_CAPACITY_FACTOR * T // N_EXPERTS
    xt = h.reshape(T, D)
    aff = jax.nn.softmax((xt @ w_router).astype(jnp.float32), axis=-1)
    gate, idx = lax.top_k(aff.T, cap)
    xe = jnp.take(xt, idx, axis=0)
    hid = jax.nn.silu(jnp.einsum('ecd,edf->ecf', xe, w_gate)) * jnp.einsum('ecd,edf->ecf', xe, w_up)
    ye = jnp.einsum('ecf,efd->ecd', hid, w_down) * gate[..., None].astype(h.dtype)
    out = jnp.zeros_like(xt).at[idx.reshape(-1)].add(ye.reshape(-1, D))
    return out.reshape(B, S, D)


def encoder_trunk(x, p, W):
    h = x
    for i in range(DEPTH):
        hn = rmsnorm(h, W['norm_mix_g'][i])
        j = i // N_MIXERS
        if i % N_MIXERS == 0:
            h = h + mla_mixer(hn, W['mla_w_in'][j], W['mla_q_norm_g'][j], W['mla_w_qb'][j],
                              W['mla_kv_norm_g'][j], W['mla_w_kvb'][j], W['mla_w_out'][j])
        else:
            h = h + gdn_mixer(hn, W['gdn_w_in'][j], W['gdn_conv_w'][j], W['gdn_A_log'][j],
                              W['gdn_dt_bias'][j], W['gdn_norm_g'][j], W['gdn_w_out'][j])
        h = h + expert_choice_moe(rmsnorm(h, W['norm_ffn_g'][i]), W['moe_w_router'][i],
                                  W['moe_w_gate'][i], W['moe_w_up'][i], W['moe_w_down'][i])
        gate = jax.nn.sigmoid(rmsnorm(h, W['norm_ple_g'][i]) @ W['ple_w_gate'][i])
        h = h + (p[i] @ W['ple_w_proj'][i]) * gate
    B, S, D = h.shape
    return pallas_rmsnorm(h.reshape(B * S, D), W['final_norm_g']).reshape(B, S, D)


def kernel(x_prompt, x_sample, p_prompt, p_sample, norm_mix_g, norm_ffn_g, norm_ple_g, final_norm_g,
           mla_w_in, mla_q_norm_g, mla_w_qb, mla_kv_norm_g, mla_w_kvb, mla_w_out,
           gdn_w_in, gdn_conv_w, gdn_A_log, gdn_dt_bias, gdn_norm_g, gdn_w_out,
           moe_w_router, moe_w_gate, moe_w_up, moe_w_down, ple_w_proj, ple_w_gate):
    W = {
        'norm_mix_g': norm_mix_g, 'norm_ffn_g': norm_ffn_g, 'norm_ple_g': norm_ple_g,
        'final_norm_g': final_norm_g,
        'mla_w_in': mla_w_in, 'mla_q_norm_g': mla_q_norm_g, 'mla_w_qb': mla_w_qb,
        'mla_kv_norm_g': mla_kv_norm_g, 'mla_w_kvb': mla_w_kvb, 'mla_w_out': mla_w_out,
        'gdn_w_in': gdn_w_in, 'gdn_conv_w': gdn_conv_w, 'gdn_A_log': gdn_A_log,
        'gdn_dt_bias': gdn_dt_bias, 'gdn_norm_g': gdn_norm_g, 'gdn_w_out': gdn_w_out,
        'moe_w_router': moe_w_router, 'moe_w_gate': moe_w_gate, 'moe_w_up': moe_w_up,
        'moe_w_down': moe_w_down, 'ple_w_proj': ple_w_proj, 'ple_w_gate': ple_w_gate,
    }
    y_prompt = encoder_trunk(x_prompt, p_prompt, W)
    y_sample = encoder_trunk(x_sample, p_sample, W)
    return (y_prompt, y_sample)
```

```python
import functools

import jax
import jax.numpy as jnp
from jax import lax
from jax.experimental import pallas as pl
from jax.experimental.pallas import tpu as pltpu

D_MODEL = 1024
DEPTH = 4
PLE_DIM = 256
N_MIXERS = 2
MLA_HEADS = 8
MLA_NOPE_DIM = 128
MLA_ROPE_DIM = 64
MLA_V_DIM = 128
MLA_Q_LORA = 256
MLA_KV_LORA = 128
ROPE_THETA = 10000.0
Q_BLOCK = 128
GDN_HEADS = 8
GDN_DK = 128
GDN_DV = 128
GDN_CONV = 5
GDN_CHUNK = 64
GDN_QKV_DIM = 2 * GDN_HEADS * GDN_DK + GDN_HEADS * GDN_DV
N_EXPERTS = 16
EC_CAPACITY_FACTOR = 2
NORM_EPS = 1e-6


def rope_tables(seq):
    pos = jnp.arange(seq, dtype=jnp.float32)
    inv = ROPE_THETA ** (-jnp.arange(0, MLA_ROPE_DIM, 2, dtype=jnp.float32) / MLA_ROPE_DIM)
    ang = pos[:, None] * inv[None, :]
    return jnp.cos(ang), jnp.sin(ang)


LANES = 128
ROW_TILE = 512
ATTN_Q_TILE = 512
MLA_QK_PAD = 2 * LANES


def _rms(x, g):
    return x * lax.rsqrt(jnp.mean(x * x, axis=-1, keepdims=True) + NORM_EPS) * g


def _mla_proj_kernel(h_ref, g_ref, w_in_ref, qg_ref, w_qb_ref, kvg_ref, w_k_ref, w_v_ref,
                     cos_ref, sin_ref, q_ref, k_ref, v_ref):
    H, NP = MLA_HEADS, MLA_NOPE_DIM
    scale = (MLA_NOPE_DIM + MLA_ROPE_DIM) ** -0.5
    hn = _rms(h_ref[...], g_ref[...])
    lat = _bdot(hn, w_in_ref[...])
    q_lat = lat[:, :MLA_Q_LORA]
    kv_lat = lat[:, MLA_Q_LORA:MLA_Q_LORA + MLA_KV_LORA]
    o = MLA_Q_LORA + MLA_KV_LORA
    cos, sin = cos_ref[...], sin_ref[...]
    k_rope = (lat[:, o:o + LANES] * cos + lat[:, o + LANES:o + 2 * LANES] * sin).astype(jnp.bfloat16)
    q = _bdot(_rms(q_lat, qg_ref[...]), w_qb_ref[...]) * scale
    kvn = _rms(kv_lat, kvg_ref[...])
    k_nope = _bdot(kvn, w_k_ref[...]).astype(jnp.bfloat16)
    v_ref[...] = _bdot(kvn, w_v_ref[...]).astype(jnp.bfloat16)
    for hd in range(H):
        a = q[:, (H + hd) * NP:(H + hd + 1) * NP]
        b = q[:, (2 * H + hd) * NP:(2 * H + hd + 1) * NP]
        q_ref[:, hd * MLA_QK_PAD:hd * MLA_QK_PAD + NP] = q[:, hd * NP:(hd + 1) * NP].astype(jnp.bfloat16)
        q_ref[:, hd * MLA_QK_PAD + NP:(hd + 1) * MLA_QK_PAD] = (a * cos + b * sin).astype(jnp.bfloat16)
        k_ref[:, hd * MLA_QK_PAD:hd * MLA_QK_PAD + NP] = k_nope[:, hd * NP:(hd + 1) * NP]
        k_ref[:, hd * MLA_QK_PAD + NP:(hd + 1) * MLA_QK_PAD] = k_rope


def _mla_attn_kernel(q_ref, k_ref, v_ref, o_ref):
    s = _bdot_nt(q_ref[0], k_ref[0])
    p = jnp.exp(s - jnp.max(s, axis=-1, keepdims=True))
    denom = jnp.sum(p, axis=-1, keepdims=True)
    o_ref[0] = (_bdot(p, v_ref[0]) / denom).astype(o_ref.dtype)


def _matmul_residual_kernel(x_ref, w_ref, h_ref, o_ref):
    o_ref[...] = h_ref[...] + _bdot(x_ref[...], w_ref[...])


def matmul_residual(x, w, h, name):
    T, K = x.shape
    D = w.shape[1]
    return pl.pallas_call(
        _matmul_residual_kernel,
        grid=(T // ROW_TILE,),
        in_specs=[pl.BlockSpec((ROW_TILE, K), lambda i: (i, 0)),
                  pl.BlockSpec((K, D), lambda i: (0, 0)),
                  pl.BlockSpec((ROW_TILE, D), lambda i: (i, 0))],
        out_specs=pl.BlockSpec((ROW_TILE, D), lambda i: (i, 0)),
        out_shape=jax.ShapeDtypeStruct((T, D), jnp.float32),
        name=name,
    )(x, w, h)


def _rope_swapped_columns(w):
    half = MLA_ROPE_DIM // 2
    pad = jnp.zeros(w.shape[:-1] + (LANES - MLA_ROPE_DIM,), w.dtype)
    plain = jnp.concatenate([w, pad], axis=-1)
    swapped = jnp.concatenate([w[..., half:], w[..., :half], pad], axis=-1)
    return plain, swapped


def mla_layer(h, norm_g, w_in, q_norm_g, w_qb, kv_norm_g, w_kvb, w_out):
    B, S, D = h.shape
    T = B * S
    H, NP, R = MLA_HEADS, MLA_NOPE_DIM, MLA_ROPE_DIM
    bf = jnp.bfloat16
    o = MLA_Q_LORA + MLA_KV_LORA
    kr_plain, kr_swapped = _rope_swapped_columns(w_in[:, o:])
    w_in_ext = jnp.concatenate([w_in[:, :o], kr_plain, kr_swapped], axis=1).astype(bf)
    w_qb_h = w_qb.reshape(MLA_Q_LORA, H, NP + R)
    qr_plain, qr_swapped = _rope_swapped_columns(w_qb_h[:, :, NP:])
    w_qb_ext = jnp.concatenate([w_qb_h[:, :, :NP].reshape(MLA_Q_LORA, H * NP),
                                qr_plain.reshape(MLA_Q_LORA, H * LANES),
                                qr_swapped.reshape(MLA_Q_LORA, H * LANES)], axis=1).astype(bf)
    w_kvb_h = w_kvb.reshape(MLA_KV_LORA, H, NP + MLA_V_DIM)
    w_k = w_kvb_h[:, :, :NP].reshape(MLA_KV_LORA, H * NP).astype(bf)
    w_v = w_kvb_h[:, :, NP:].reshape(MLA_KV_LORA, H * MLA_V_DIM).astype(bf)
    cos, sin = rope_tables(S)
    zeros = jnp.zeros((S, LANES - R), jnp.float32)
    cos_pad = jnp.concatenate([cos, cos, zeros], axis=1)
    sin_pad = jnp.concatenate([-sin, sin, zeros], axis=1)

    tiles_per_seq = S // ROW_TILE
    row = lambda i: (i, 0)
    fixed = lambda i: (0, 0)
    pos = lambda i: (i % tiles_per_seq, 0)
    q_cat, k_cat, v = pl.pallas_call(
        _mla_proj_kernel,
        grid=(T // ROW_TILE,),
        in_specs=[pl.BlockSpec((ROW_TILE, D), row),
                  pl.BlockSpec((1, D), fixed),
                  pl.BlockSpec(w_in_ext.shape, fixed),
                  pl.BlockSpec((1, MLA_Q_LORA), fixed),
                  pl.BlockSpec(w_qb_ext.shape, fixed),
                  pl.BlockSpec((1, MLA_KV_LORA), fixed),
                  pl.BlockSpec(w_k.shape, fixed),
                  pl.BlockSpec(w_v.shape, fixed),
                  pl.BlockSpec((ROW_TILE, LANES), pos),
                  pl.BlockSpec((ROW_TILE, LANES), pos)],
        out_specs=[pl.BlockSpec((ROW_TILE, H * MLA_QK_PAD), row),
                   pl.BlockSpec((ROW_TILE, H * MLA_QK_PAD), row),
                   pl.BlockSpec((ROW_TILE, H * MLA_V_DIM), row)],
        out_shape=[jax.ShapeDtypeStruct((T, H * MLA_QK_PAD), bf),
                   jax.ShapeDtypeStruct((T, H * MLA_QK_PAD), bf),
                   jax.ShapeDtypeStruct((T, H * MLA_V_DIM), bf)],
        name="mla_proj",
    )(h.reshape(T, D), norm_g.reshape(1, D), w_in_ext, q_norm_g.reshape(1, -1), w_qb_ext,
      kv_norm_g.reshape(1, -1), w_k, w_v, cos_pad, sin_pad)

    o_attn = pl.pallas_call(
        _mla_attn_kernel,
        grid=(B, H, S // ATTN_Q_TILE),
        in_specs=[pl.BlockSpec((1, ATTN_Q_TILE, MLA_QK_PAD), lambda b, hd, i: (b, i, hd)),
                  pl.BlockSpec((1, S, MLA_QK_PAD), lambda b, hd, i: (b, 0, hd)),
                  pl.BlockSpec((1, S, MLA_V_DIM), lambda b, hd, i: (b, 0, hd))],
        out_specs=pl.BlockSpec((1, ATTN_Q_TILE, MLA_V_DIM), lambda b, hd, i: (b, i, hd)),
        out_shape=jax.ShapeDtypeStruct((B, S, H * MLA_V_DIM), bf),
        compiler_params=pltpu.CompilerParams(
            dimension_semantics=("parallel", "parallel", "arbitrary")),
        name="mla_attention",
    )(q_cat.reshape(B, S, -1), k_cat.reshape(B, S, -1), v.reshape(B, S, -1))

    h_new = matmul_residual(o_attn.reshape(T, H * MLA_V_DIM), w_out.astype(bf), h.reshape(T, D),
                            name="mla_out_proj")
    return h_new.reshape(B, S, D)


def _bdot(a, b):
    return jnp.dot(a.astype(jnp.bfloat16), b.astype(jnp.bfloat16), preferred_element_type=jnp.float32)


def _bdot_nt(a, b):
    return lax.dot_general(a.astype(jnp.bfloat16), b.astype(jnp.bfloat16),
                           (((1,), (1,)), ((), ())), preferred_element_type=jnp.float32)


def _unit_triangular_inverses(Ls, row, col):
    C = Ls[0].shape[0]
    eye = (row == col).astype(jnp.float32)

    def same_block(s):
        return (row // s) == (col // s)

    blk8 = same_block(8)
    L8 = [jnp.where(blk8, L, 0.0) for L in Ls]
    P2 = [_bdot(a, a) for a in L8]
    P4 = [_bdot(a, a) for a in P2]
    X = [eye - a for a in L8]
    X = [x + _bdot(x, p) for x, p in zip(X, P2)]
    X = [x + _bdot(x, p) for x, p in zip(X, P4)]
    s = 8
    while s < C:
        pair = same_block(2 * s) & jnp.logical_not(same_block(s))
        XC = [_bdot(x, jnp.where(pair, L, 0.0)) for x, L in zip(X, Ls)]
        X = [x - _bdot(xc, x) for x, xc in zip(X, XC)]
        s *= 2
    return X


def _gdn_chunk_kernel(qf_ref, kf_ref, vf_ref, gcf_ref, bcf_ref, grf_ref,
                      qb_ref, kb_ref, vb_ref, gcb_ref, bcb_ref, grb_ref,
                      of_ref, ob_ref, state_ref):
    H, C, DK, DV = GDN_HEADS, GDN_CHUNK, GDN_DK, GDN_DV

    @pl.when(pl.program_id(1) == 0)
    def _():
        state_ref[...] = jnp.zeros_like(state_ref)

    row = lax.broadcasted_iota(jnp.int32, (C, C), 0)
    col = lax.broadcasted_iota(jnp.int32, (C, C), 1)
    dirs = ((0, qf_ref, kf_ref, vf_ref, gcf_ref, bcf_ref, grf_ref, of_ref),
            (1, qb_ref, kb_ref, vb_ref, gcb_ref, bcb_ref, grb_ref, ob_ref))
    chains = []
    for d, q_ref, k_ref, v_ref, gc_ref, bc_ref, gr_ref, o_ref in dirs:
        if d == 0:
            incl, strict, last = row >= col, row > col, C - 1
        else:
            incl, strict, last = row <= col, row < col, 0
        tri = incl.astype(jnp.float32)
        gc_cols = jnp.dot(tri, gc_ref[0], precision=lax.Precision.HIGHEST,
                          preferred_element_type=jnp.float32)
        gc_rows = jnp.dot(gr_ref[0, 0], tri.T, precision=lax.Precision.HIGHEST,
                          preferred_element_type=jnp.float32)
        beta_cols = bc_ref[0]
        for h in range(H):
            c = d * H + h
            chains.append(dict(
                c=c, incl=incl, strict=strict, o_ref=o_ref, sl=slice(h * DK, (h + 1) * DK),
                q_ref=q_ref, k_ref=k_ref, v_ref=v_ref,
                beta=beta_cols[:, c:c + 1], gcc=gc_cols[:, c:c + 1], gcr=gc_rows[c:c + 1, :],
                gl=gc_cols[last:last + 1, c:c + 1]))

    for ch in chains:
        ch['decay'] = jnp.where(ch['incl'], jnp.exp(jnp.where(ch['incl'], ch['gcc'] - ch['gcr'], 0.0)), 0.0)
        ch['eg'] = jnp.exp(ch['gcc'])
    for ch in chains:
        k = ch['k_ref'][0, :, ch['sl']]
        q = ch['q_ref'][0, :, ch['sl']]
        kb = k * ch['beta']
        ch['kbeg'] = kb * ch['eg']
        ch['s1'] = _bdot_nt(jnp.concatenate([kb, q], axis=0), k)
    Ls = [jnp.where(ch['strict'], ch['s1'][:C] * ch['decay'], 0.0) for ch in chains]
    Ts = _unit_triangular_inverses(Ls, row, col)
    for ch, T in zip(chains, Ts):
        v = ch['v_ref'][0, :, ch['sl']]
        ch['uw'] = _bdot(T, jnp.concatenate([v * ch['beta'], ch['kbeg']], axis=1))
    for ch in chains:
        q = ch['q_ref'][0, :, ch['sl']]
        w = ch['uw'][:, DV:]
        ch['ws'] = _bdot(jnp.concatenate([w, q * ch['eg']], axis=0), state_ref[ch['c']])
    for ch in chains:
        ch['v_new'] = ch['uw'][:, :DV] - ch['ws'][:C]
        a_intra = ch['s1'][C:] * ch['decay']
        ch['o_ref'][0, :, ch['sl']] = ch['ws'][C:] + _bdot(a_intra, ch['v_new'])
    for ch in chains:
        k = ch['k_ref'][0, :, ch['sl']]
        k_g = k * jnp.exp(ch['gl'] - ch['gcc'])
        c = ch['c']
        state_ref[c] = state_ref[c] * jnp.exp(ch['gl']) + _bdot(k_g.T, ch['v_new'])


def gdn_bidirectional_delta(q, k, v, beta, g):
    B, S, HD = q.shape
    C, H = GDN_CHUNK, GDN_HEADS
    N = S // C
    g_rows = g.reshape(B, N, C, 2 * H).transpose(0, 1, 3, 2)

    def fwd3(b, n):
        return (b, n, 0)

    def bwd3(b, n):
        return (b, N - 1 - n, 0)

    def fwd4(b, n):
        return (b, n, 0, 0)

    def bwd4(b, n):
        return (b, N - 1 - n, 0, 0)

    def specs(i3, i4):
        big = pl.BlockSpec((1, C, HD), i3)
        small = pl.BlockSpec((1, C, 2 * H), i3)
        return [big, big, big, small, small, pl.BlockSpec((1, 1, 2 * H, C), i4)]

    out = jax.ShapeDtypeStruct((B, S, HD), jnp.float32)
    return pl.pallas_call(
        _gdn_chunk_kernel,
        grid=(B, N),
        in_specs=specs(fwd3, fwd4) + specs(bwd3, bwd4),
        out_specs=[pl.BlockSpec((1, C, HD), fwd3), pl.BlockSpec((1, C, HD), bwd3)],
        out_shape=[out, out],
        scratch_shapes=[pltpu.VMEM((2 * H, GDN_DK, GDN_DV), jnp.float32)],
        compiler_params=pltpu.CompilerParams(dimension_semantics=("parallel", "arbitrary")),
        name="gdn_chunk_delta",
    )(q, k, v, g, beta, g_rows, q, k, v, g, beta, g_rows)


GDN_ROW_TILE = 256
HALO = 8
VMEM_LIMIT_BYTES = 48 * 1024 * 1024


def _gdn_in_kernel(h_ref, hp_ref, hn_ref, g_ref, wqkv_ref, wz_ref, wba_ref, cw_ref, alog_ref, dtb_ref,
                   q_ref, k_ref, v_ref, z_ref, beta_ref, gl_ref, *, tiles_per_seq):
    H, DK = GDN_HEADS, GDN_DK
    TM = h_ref.shape[0]
    n = TM + 2 * HALO
    i = pl.program_id(0) % tiles_per_seq
    first_row = jnp.where(i == 0, HALO, 0)
    end_row = jnp.where(i == tiles_per_seq - 1, HALO + TM, n)
    hx = jnp.concatenate([hp_ref[...], h_ref[...], hn_ref[...]], axis=0)
    hn = _rms(hx, g_ref[...]).astype(jnp.bfloat16)
    pre = _bdot(hn, wqkv_ref[...])
    r = lax.broadcasted_iota(jnp.int32, (n, 1), 0)
    pre = jnp.where((r >= first_row) & (r < end_row), pre, 0.0)
    pad = GDN_CONV // 2
    acc = None
    for j in range(GDN_CONV):
        shifted = pre if j == pad else pltpu.roll(pre, (pad - j) % n, 0)
        term = shifted[HALO:HALO + TM] * cw_ref[j:j + 1, :]
        acc = term if acc is None else acc + term
    qkv = _silu(acc)
    for hd in range(H):
        for off, ref, scale in ((0, q_ref, DK ** -0.5), (H * DK, k_ref, 1.0)):
            x = qkv[:, off + hd * DK:off + (hd + 1) * DK]
            ref[:, hd * DK:(hd + 1) * DK] = x * (lax.rsqrt(jnp.sum(x * x, axis=-1, keepdims=True) + NORM_EPS) * scale)
    v_ref[...] = qkv[:, 2 * H * DK:]
    hc = hn[HALO:HALO + TM]
    z_ref[...] = _bdot(hc, wz_ref[...])
    ba = _bdot(hc, wba_ref[...])
    beta_ref[...] = _sigmoid(ba[:, :2 * H])
    a = ba[:, 2 * H:] + dtb_ref[...]
    softplus = jnp.maximum(a, 0.0) + jnp.log(1.0 + jnp.exp(-jnp.abs(a)))
    gl_ref[...] = -jnp.exp(alog_ref[...]) * softplus


def _gdn_out_kernel(of_ref, ob_ref, z_ref, g_ref, w_ref, h_ref, o_ref):
    H, DV = GDN_HEADS, GDN_DV
    o = of_ref[...] + ob_ref[...]
    g = g_ref[...]
    parts = []
    for hd in range(H):
        x = o[:, hd * DV:(hd + 1) * DV]
        parts.append(_rms(x, g))
    y = jnp.concatenate(parts, axis=1) * _silu(z_ref[...])
    o_ref[...] = h_ref[...] + _bdot(y, w_ref[...])


def gdn_layer(h, norm_g, w_in, conv_w, A_log, dt_bias, out_norm_g, w_out):
    B, S, D = h.shape
    T = B * S
    H = GDN_HEADS
    HD = H * GDN_DK
    bf = jnp.bfloat16
    TM = GDN_ROW_TILE
    tiles_per_seq = S // TM
    hb = TM // HALO
    n_halo = T // HALO
    o = GDN_QKV_DIM + H * GDN_DV
    w_qkv = w_in[:, :GDN_QKV_DIM].astype(bf)
    w_z = w_in[:, GDN_QKV_DIM:o].astype(bf)
    w_ba = w_in[:, o:].astype(bf)
    cw = jnp.concatenate([conv_w, jnp.zeros((HALO - GDN_CONV, GDN_QKV_DIM), conv_w.dtype)], axis=0)
    row = lambda i: (i, 0)
    fixed = lambda i: (0, 0)
    f32 = jnp.float32
    ht = h.reshape(T, D)
    q, k, v, z, beta, gl = pl.pallas_call(
        functools.partial(_gdn_in_kernel, tiles_per_seq=tiles_per_seq),
        grid=(T // TM,),
        in_specs=[pl.BlockSpec((TM, D), row),
                  pl.BlockSpec((HALO, D), lambda i: (jnp.maximum(i * hb - 1, 0), 0)),
                  pl.BlockSpec((HALO, D), lambda i: (jnp.minimum((i + 1) * hb, n_halo - 1), 0)),
                  pl.BlockSpec((1, D), fixed),
                  pl.BlockSpec(w_qkv.shape, fixed), pl.BlockSpec(w_z.shape, fixed),
                  pl.BlockSpec(w_ba.shape, fixed), pl.BlockSpec(cw.shape, fixed),
                  pl.BlockSpec((1, 2 * H), fixed), pl.BlockSpec((1, 2 * H), fixed)],
        out_specs=[pl.BlockSpec((TM, HD), row), pl.BlockSpec((TM, HD), row), pl.BlockSpec((TM, HD), row),
                   pl.BlockSpec((TM, HD), row), pl.BlockSpec((TM, 2 * H), row), pl.BlockSpec((TM, 2 * H), row)],
        out_shape=[jax.ShapeDtypeStruct((T, HD), f32)] * 4 + [jax.ShapeDtypeStruct((T, 2 * H), f32)] * 2,
        compiler_params=pltpu.CompilerParams(vmem_limit_bytes=VMEM_LIMIT_BYTES),
        name="gdn_in_proj",
    )(ht, ht, ht, norm_g.reshape(1, D), w_qkv, w_z, w_ba, cw,
      A_log.reshape(1, 2 * H).astype(f32), dt_bias.reshape(1, 2 * H).astype(f32))

    o_fwd, o_bwd = gdn_bidirectional_delta(q.reshape(B, S, HD), k.reshape(B, S, HD), v.reshape(B, S, HD),
                                           beta.reshape(B, S, 2 * H), gl.reshape(B, S, 2 * H))

    h_new = pl.pallas_call(
        _gdn_out_kernel,
        grid=(T // ROW_TILE,),
        in_specs=[pl.BlockSpec((ROW_TILE, HD), row)] * 3 +
                 [pl.BlockSpec((1, GDN_DV), fixed), pl.BlockSpec((HD, D), fixed), pl.BlockSpec((ROW_TILE, D), row)],
        out_specs=pl.BlockSpec((ROW_TILE, D), row),
        out_shape=jax.ShapeDtypeStruct((T, D), f32),
        name="gdn_out_proj",
    )(o_fwd.reshape(T, HD), o_bwd.reshape(T, HD), z, out_norm_g.reshape(1, GDN_DV), w_out.astype(bf), ht)
    return h_new.reshape(B, S, D)


def _sigmoid(x):
    return 1.0 / (1.0 + jnp.exp(-x))


def _silu(x):
    return x * _sigmoid(x)


def _moe_router_kernel(h_ref, g_ref, w_ref, xn_ref, aff_ref):
    xn = _rms(h_ref[...], g_ref[...])
    xn_ref[...] = xn.astype(xn_ref.dtype)
    logits = _bdot(xn, w_ref[...])
    e = jnp.exp(logits - jnp.max(logits, axis=-1, keepdims=True))
    aff_ref[...] = e / jnp.sum(e, axis=-1, keepdims=True)


def _moe_ffn_kernel(x_ref, wg_ref, wu_ref, wd_ref, gate_ref, y_ref):
    x = x_ref[0]
    hid = _silu(_bdot(x, wg_ref[0])) * _bdot(x, wu_ref[0])
    y_ref[0] = _bdot(hid, wd_ref[0]) * gate_ref[0]


def moe_layer(h, norm_g, w_router, w_gate, w_up, w_down):
    T, D = h.shape
    E, F = N_EXPERTS, w_gate.shape[-1]
    cap = EC_CAPACITY_FACTOR * T // E
    bf = jnp.bfloat16
    row = lambda i: (i, 0)
    fixed = lambda i: (0, 0)
    xn, aff = pl.pallas_call(
        _moe_router_kernel,
        grid=(T // ROW_TILE,),
        in_specs=[pl.BlockSpec((ROW_TILE, D), row), pl.BlockSpec((1, D), fixed),
                  pl.BlockSpec((D, E), fixed)],
        out_specs=[pl.BlockSpec((ROW_TILE, D), row), pl.BlockSpec((ROW_TILE, E), row)],
        out_shape=[jax.ShapeDtypeStruct((T, D), bf), jax.ShapeDtypeStruct((T, E), jnp.float32)],
        name="moe_router",
    )(h, norm_g.reshape(1, D), w_router.astype(bf))
    gate, idx = lax.top_k(aff.T, cap)
    xe = jnp.take(xn, idx, axis=0)
    ye = pl.pallas_call(
        _moe_ffn_kernel,
        grid=(E, cap // ROW_TILE),
        in_specs=[pl.BlockSpec((1, ROW_TILE, D), lambda e, i: (e, i, 0)),
                  pl.BlockSpec((1, D, F), lambda e, i: (e, 0, 0)),
                  pl.BlockSpec((1, D, F), lambda e, i: (e, 0, 0)),
                  pl.BlockSpec((1, F, D), lambda e, i: (e, 0, 0)),
                  pl.BlockSpec((1, ROW_TILE, 1), lambda e, i: (e, i, 0))],
        out_specs=pl.BlockSpec((1, ROW_TILE, D), lambda e, i: (e, i, 0)),
        out_shape=jax.ShapeDtypeStruct((E, cap, D), jnp.float32),
        compiler_params=pltpu.CompilerParams(dimension_semantics=("parallel", "arbitrary")),
        name="moe_expert_ffn",
    )(xe, w_gate.astype(bf), w_up.astype(bf), w_down.astype(bf), gate[..., None])
    return h.at[idx.reshape(-1)].add(ye.reshape(-1, D))


def _ple_kernel(h_ref, p_ref, g_ref, wg_ref, wp_ref, o_ref):
    h = h_ref[...]
    gate = _sigmoid(_bdot(_rms(h, g_ref[...]), wg_ref[...]))
    o_ref[...] = h + _bdot(p_ref[...], wp_ref[...]) * gate


def _ple_final_kernel(h_ref, p_ref, g_ref, wg_ref, wp_ref, fg_ref, o_ref):
    h = h_ref[...]
    gate = _sigmoid(_bdot(_rms(h, g_ref[...]), wg_ref[...]))
    o_ref[...] = _rms(h + _bdot(p_ref[...], wp_ref[...]) * gate, fg_ref[...])


def ple_layer(h, p, norm_g, w_gate, w_proj, final_g=None):
    T, D = h.shape
    P = p.shape[1]
    bf = jnp.bfloat16
    row = lambda i: (i, 0)
    fixed = lambda i: (0, 0)
    in_specs = [pl.BlockSpec((ROW_TILE, D), row), pl.BlockSpec((ROW_TILE, P), row),
                pl.BlockSpec((1, D), fixed), pl.BlockSpec((D, D), fixed), pl.BlockSpec((P, D), fixed)]
    args = [h, p, norm_g.reshape(1, D), w_gate.astype(bf), w_proj.astype(bf)]
    body = _ple_kernel
    if final_g is not None:
        in_specs.append(pl.BlockSpec((1, D), fixed))
        args.append(final_g.reshape(1, D))
        body = _ple_final_kernel
    return pl.pallas_call(
        body,
        grid=(T // ROW_TILE,),
        in_specs=in_specs,
        out_specs=pl.BlockSpec((ROW_TILE, D), row),
        out_shape=jax.ShapeDtypeStruct((T, D), jnp.float32),
        name="ple_gate",
    )(*args)


def encoder_trunk(x, p, W):
    h = x
    for i in range(DEPTH):
        j = i // N_MIXERS
        if i % N_MIXERS == 0:
            h = mla_layer(h, W['norm_mix_g'][i], W['mla_w_in'][j], W['mla_q_norm_g'][j], W['mla_w_qb'][j],
                          W['mla_kv_norm_g'][j], W['mla_w_kvb'][j], W['mla_w_out'][j])
        else:
            h = gdn_layer(h, W['norm_mix_g'][i], W['gdn_w_in'][j], W['gdn_conv_w'][j], W['gdn_A_log'][j],
                          W['gdn_dt_bias'][j], W['gdn_norm_g'][j], W['gdn_w_out'][j])
        B, S, D = h.shape
        ht = moe_layer(h.reshape(B * S, D), W['norm_ffn_g'][i], W['moe_w_router'][i],
                       W['moe_w_gate'][i], W['moe_w_up'][i], W['moe_w_down'][i])
        ht = ple_layer(ht, p[i].reshape(B * S, PLE_DIM), W['norm_ple_g'][i], W['ple_w_gate'][i],
                       W['ple_w_proj'][i], final_g=W['final_norm_g'] if i == DEPTH - 1 else None)
        h = ht.reshape(B, S, D)
    return h


def kernel(x_prompt, x_sample, p_prompt, p_sample, norm_mix_g, norm_ffn_g, norm_ple_g, final_norm_g,
           mla_w_in, mla_q_norm_g, mla_w_qb, mla_kv_norm_g, mla_w_kvb, mla_w_out,
           gdn_w_in, gdn_conv_w, gdn_A_log, gdn_dt_bias, gdn_norm_g, gdn_w_out,
           moe_w_router, moe_w_gate, moe_w_up, moe_w_down, ple_w_proj, ple_w_gate):
    W = {
        'norm_mix_g': norm_mix_g, 'norm_ffn_g': norm_ffn_g, 'norm_ple_g': norm_ple_g,
        'final_norm_g': final_norm_g,
        'mla_w_in': mla_w_in, 'mla_q_norm_g': mla_q_norm_g, 'mla_w_qb': mla_w_qb,
        'mla_kv_norm_g': mla_kv_norm_g, 'mla_w_kvb': mla_w_kvb, 'mla_w_out': mla_w_out,
        'gdn_w_in': gdn_w_in, 'gdn_conv_w': gdn_conv_w, 'gdn_A_log': gdn_A_log,
        'gdn_dt_bias': gdn_dt_bias, 'gdn_norm_g': gdn_norm_g, 'gdn_w_out': gdn_w_out,
        'moe_w_router': moe_w_router, 'moe_w_gate': moe_w_gate, 'moe_w_up': moe_w_up,
        'moe_w_down': moe_w_down, 'ple_w_proj': ple_w_proj, 'ple_w_gate': ple_w_gate,
    }
    y_prompt = encoder_trunk(x_prompt, p_prompt, W)
    y_sample = encoder_trunk(x_sample, p_sample, W)
    return (y_prompt, y_sample)
```

```python
import functools

import jax
import jax.numpy as jnp
from jax import lax
from jax.experimental import pallas as pl
from jax.experimental.pallas import tpu as pltpu

D_MODEL = 1024
DEPTH = 4
PLE_DIM = 256
N_MIXERS = 2
MLA_HEADS = 8
MLA_NOPE_DIM = 128
MLA_ROPE_DIM = 64
MLA_V_DIM = 128
MLA_Q_LORA = 256
MLA_KV_LORA = 128
ROPE_THETA = 10000.0
Q_BLOCK = 128
GDN_HEADS = 8
GDN_DK = 128
GDN_DV = 128
GDN_CONV = 5
GDN_CHUNK = 64
GDN_QKV_DIM = 2 * GDN_HEADS * GDN_DK + GDN_HEADS * GDN_DV
N_EXPERTS = 16
EC_CAPACITY_FACTOR = 2
NORM_EPS = 1e-6


def rope_tables(seq):
    pos = jnp.arange(seq, dtype=jnp.float32)
    inv = ROPE_THETA ** (-jnp.arange(0, MLA_ROPE_DIM, 2, dtype=jnp.float32) / MLA_ROPE_DIM)
    ang = pos[:, None] * inv[None, :]
    return jnp.cos(ang), jnp.sin(ang)


LANES = 128
ROW_TILE = 512
ATTN_Q_TILE = 512
MLA_QK_PAD = 2 * LANES


def _rms(x, g):
    return x * lax.rsqrt(jnp.mean(x * x, axis=-1, keepdims=True) + NORM_EPS) * g


def _mla_proj_kernel(h_ref, g_ref, w_in_ref, qg_ref, w_qb_ref, kvg_ref, w_k_ref, w_v_ref,
                     cos_ref, sin_ref, q_ref, k_ref, v_ref):
    H, NP = MLA_HEADS, MLA_NOPE_DIM
    scale = (MLA_NOPE_DIM + MLA_ROPE_DIM) ** -0.5
    hn = _rms(h_ref[...], g_ref[...])
    lat = _bdot(hn, w_in_ref[...])
    q_lat = lat[:, :MLA_Q_LORA]
    kv_lat = lat[:, MLA_Q_LORA:MLA_Q_LORA + MLA_KV_LORA]
    o = MLA_Q_LORA + MLA_KV_LORA
    cos, sin = cos_ref[...], sin_ref[...]
    k_rope = (lat[:, o:o + LANES] * cos + lat[:, o + LANES:o + 2 * LANES] * sin).astype(jnp.bfloat16)
    q = _bdot(_rms(q_lat, qg_ref[...]), w_qb_ref[...]) * scale
    kvn = _rms(kv_lat, kvg_ref[...])
    k_nope = _bdot(kvn, w_k_ref[...]).astype(jnp.bfloat16)
    v_ref[...] = _bdot(kvn, w_v_ref[...]).astype(jnp.bfloat16)
    for hd in range(H):
        a = q[:, (H + hd) * NP:(H + hd + 1) * NP]
        b = q[:, (2 * H + hd) * NP:(2 * H + hd + 1) * NP]
        q_ref[:, hd * MLA_QK_PAD:hd * MLA_QK_PAD + NP] = q[:, hd * NP:(hd + 1) * NP].astype(jnp.bfloat16)
        q_ref[:, hd * MLA_QK_PAD + NP:(hd + 1) * MLA_QK_PAD] = (a * cos + b * sin).astype(jnp.bfloat16)
        k_ref[:, hd * MLA_QK_PAD:hd * MLA_QK_PAD + NP] = k_nope[:, hd * NP:(hd + 1) * NP]
        k_ref[:, hd * MLA_QK_PAD + NP:(hd + 1) * MLA_QK_PAD] = k_rope


def _mla_attn_kernel(q_ref, k_ref, v_ref, o_ref):
    s = _bdot_nt(q_ref[0], k_ref[0])
    p = jnp.exp(s - jnp.max(s, axis=-1, keepdims=True))
    denom = jnp.sum(p, axis=-1, keepdims=True)
    o_ref[0] = (_bdot(p, v_ref[0]) / denom).astype(o_ref.dtype)


def _matmul_residual_kernel(x_ref, w_ref, h_ref, o_ref):
    o_ref[...] = h_ref[...] + _bdot(x_ref[...], w_ref[...])


def matmul_residual(x, w, h, name):
    T, K = x.shape
    D = w.shape[1]
    return pl.pallas_call(
        _matmul_residual_kernel,
        grid=(T // ROW_TILE,),
        in_specs=[pl.BlockSpec((ROW_TILE, K), lambda i: (i, 0)),
                  pl.BlockSpec((K, D), lambda i: (0, 0)),
                  pl.BlockSpec((ROW_TILE, D), lambda i: (i, 0))],
        out_specs=pl.BlockSpec((ROW_TILE, D), lambda i: (i, 0)),
        out_shape=jax.ShapeDtypeStruct((T, D), jnp.float32),
        name=name,
    )(x, w, h)


def _rope_swapped_columns(w):
    half = MLA_ROPE_DIM // 2
    pad = jnp.zeros(w.shape[:-1] + (LANES - MLA_ROPE_DIM,), w.dtype)
    plain = jnp.concatenate([w, pad], axis=-1)
    swapped = jnp.concatenate([w[..., half:], w[..., :half], pad], axis=-1)
    return plain, swapped


def mla_layer(h, norm_g, w_in, q_norm_g, w_qb, kv_norm_g, w_kvb, w_out):
    B, S, D = h.shape
    T = B * S
    H, NP, R = MLA_HEADS, MLA_NOPE_DIM, MLA_ROPE_DIM
    bf = jnp.bfloat16
    o = MLA_Q_LORA + MLA_KV_LORA
    kr_plain, kr_swapped = _rope_swapped_columns(w_in[:, o:])
    w_in_ext = jnp.concatenate([w_in[:, :o], kr_plain, kr_swapped], axis=1).astype(bf)
    w_qb_h = w_qb.reshape(MLA_Q_LORA, H, NP + R)
    qr_plain, qr_swapped = _rope_swapped_columns(w_qb_h[:, :, NP:])
    w_qb_ext = jnp.concatenate([w_qb_h[:, :, :NP].reshape(MLA_Q_LORA, H * NP),
                                qr_plain.reshape(MLA_Q_LORA, H * LANES),
                                qr_swapped.reshape(MLA_Q_LORA, H * LANES)], axis=1).astype(bf)
    w_kvb_h = w_kvb.reshape(MLA_KV_LORA, H, NP + MLA_V_DIM)
    w_k = w_kvb_h[:, :, :NP].reshape(MLA_KV_LORA, H * NP).astype(bf)
    w_v = w_kvb_h[:, :, NP:].reshape(MLA_KV_LORA, H * MLA_V_DIM).astype(bf)
    cos, sin = rope_tables(S)
    zeros = jnp.zeros((S, LANES - R), jnp.float32)
    cos_pad = jnp.concatenate([cos, cos, zeros], axis=1)
    sin_pad = jnp.concatenate([-sin, sin, zeros], axis=1)

    tiles_per_seq = S // ROW_TILE
    row = lambda i: (i, 0)
    fixed = lambda i: (0, 0)
    pos = lambda i: (i % tiles_per_seq, 0)
    q_cat, k_cat, v = pl.pallas_call(
        _mla_proj_kernel,
        grid=(T // ROW_TILE,),
        in_specs=[pl.BlockSpec((ROW_TILE, D), row),
                  pl.BlockSpec((1, D), fixed),
                  pl.BlockSpec(w_in_ext.shape, fixed),
                  pl.BlockSpec((1, MLA_Q_LORA), fixed),
                  pl.BlockSpec(w_qb_ext.shape, fixed),
                  pl.BlockSpec((1, MLA_KV_LORA), fixed),
                  pl.BlockSpec(w_k.shape, fixed),
                  pl.BlockSpec(w_v.shape, fixed),
                  pl.BlockSpec((ROW_TILE, LANES), pos),
                  pl.BlockSpec((ROW_TILE, LANES), pos)],
        out_specs=[pl.BlockSpec((ROW_TILE, H * MLA_QK_PAD), row),
                   pl.BlockSpec((ROW_TILE, H * MLA_QK_PAD), row),
                   pl.BlockSpec((ROW_TILE, H * MLA_V_DIM), row)],
        out_shape=[jax.ShapeDtypeStruct((T, H * MLA_QK_PAD), bf),
                   jax.ShapeDtypeStruct((T, H * MLA_QK_PAD), bf),
                   jax.ShapeDtypeStruct((T, H * MLA_V_DIM), bf)],
        name="mla_proj",
    )(h.reshape(T, D), norm_g.reshape(1, D), w_in_ext, q_norm_g.reshape(1, -1), w_qb_ext,
      kv_norm_g.reshape(1, -1), w_k, w_v, cos_pad, sin_pad)

    o_attn = pl.pallas_call(
        _mla_attn_kernel,
        grid=(B, H, S // ATTN_Q_TILE),
        in_specs=[pl.BlockSpec((1, ATTN_Q_TILE, MLA_QK_PAD), lambda b, hd, i: (b, i, hd)),
                  pl.BlockSpec((1, S, MLA_QK_PAD), lambda b, hd, i: (b, 0, hd)),
                  pl.BlockSpec((1, S, MLA_V_DIM), lambda b, hd, i: (b, 0, hd))],
        out_specs=pl.BlockSpec((1, ATTN_Q_TILE, MLA_V_DIM), lambda b, hd, i: (b, i, hd)),
        out_shape=jax.ShapeDtypeStruct((B, S, H * MLA_V_DIM), bf),
        compiler_params=pltpu.CompilerParams(
            dimension_semantics=("parallel", "parallel", "arbitrary")),
        name="mla_attention",
    )(q_cat.reshape(B, S, -1), k_cat.reshape(B, S, -1), v.reshape(B, S, -1))

    h_new = matmul_residual(o_attn.reshape(T, H * MLA_V_DIM), w_out.astype(bf), h.reshape(T, D),
                            name="mla_out_proj")
    return h_new.reshape(B, S, D)


def _bdot(a, b):
    return jnp.dot(a.astype(jnp.bfloat16), b.astype(jnp.bfloat16), preferred_element_type=jnp.float32)


def _bdot_nt(a, b):
    return lax.dot_general(a.astype(jnp.bfloat16), b.astype(jnp.bfloat16),
                           (((1,), (1,)), ((), ())), preferred_element_type=jnp.float32)


def _unit_triangular_inverses(Ls, row, col):
    C = Ls[0].shape[0]
    eye = (row == col).astype(jnp.float32)

    def same_block(s):
        return (row // s) == (col // s)

    blk8 = same_block(8)
    L8 = [jnp.where(blk8, L, 0.0) for L in Ls]
    P2 = [_bdot(a, a) for a in L8]
    P4 = [_bdot(a, a) for a in P2]
    X = [eye - a for a in L8]
    X = [x + _bdot(x, p) for x, p in zip(X, P2)]
    X = [x + _bdot(x, p) for x, p in zip(X, P4)]
    s = 8
    while s < C:
        pair = same_block(2 * s) & jnp.logical_not(same_block(s))
        XC = [_bdot(x, jnp.where(pair, L, 0.0)) for x, L in zip(X, Ls)]
        X = [x - _bdot(xc, x) for x, xc in zip(X, XC)]
        s *= 2
    return X


def _gdn_chunk_kernel(qf_ref, kf_ref, vf_ref, gcf_ref, bcf_ref, grf_ref,
                      qb_ref, kb_ref, vb_ref, gcb_ref, bcb_ref, grb_ref,
                      of_ref, ob_ref, state_ref):
    H, C, DK, DV = GDN_HEADS, GDN_CHUNK, GDN_DK, GDN_DV

    @pl.when(pl.program_id(1) == 0)
    def _():
        state_ref[...] = jnp.zeros_like(state_ref)

    row = lax.broadcasted_iota(jnp.int32, (C, C), 0)
    col = lax.broadcasted_iota(jnp.int32, (C, C), 1)
    dirs = ((0, qf_ref, kf_ref, vf_ref, gcf_ref, bcf_ref, grf_ref, of_ref),
            (1, qb_ref, kb_ref, vb_ref, gcb_ref, bcb_ref, grb_ref, ob_ref))
    chains = []
    for d, q_ref, k_ref, v_ref, gc_ref, bc_ref, gr_ref, o_ref in dirs:
        if d == 0:
            incl, strict, last = row >= col, row > col, C - 1
        else:
            incl, strict, last = row <= col, row < col, 0
        tri = incl.astype(jnp.float32)
        gc_cols = jnp.dot(tri, gc_ref[0], precision=lax.Precision.HIGHEST,
                          preferred_element_type=jnp.float32)
        gc_rows = jnp.dot(gr_ref[0, 0], tri.T, precision=lax.Precision.HIGHEST,
                          preferred_element_type=jnp.float32)
        beta_cols = bc_ref[0]
        for h in range(H):
            c = d * H + h
            chains.append(dict(
                c=c, incl=incl, strict=strict, o_ref=o_ref, sl=slice(h * DK, (h + 1) * DK),
                q_ref=q_ref, k_ref=k_ref, v_ref=v_ref,
                beta=beta_cols[:, c:c + 1], gcc=gc_cols[:, c:c + 1], gcr=gc_rows[c:c + 1, :],
                gl=gc_cols[last:last + 1, c:c + 1]))

    for ch in chains:
        ch['decay'] = jnp.where(ch['incl'], jnp.exp(jnp.where(ch['incl'], ch['gcc'] - ch['gcr'], 0.0)), 0.0)
        ch['eg'] = jnp.exp(ch['gcc'])
    for ch in chains:
        k = ch['k_ref'][0, :, ch['sl']]
        q = ch['q_ref'][0, :, ch['sl']]
        kb = k * ch['beta']
        ch['kbeg'] = kb * ch['eg']
        ch['s1'] = _bdot_nt(jnp.concatenate([kb, q], axis=0), k)
    Ls = [jnp.where(ch['strict'], ch['s1'][:C] * ch['decay'], 0.0) for ch in chains]
    Ts = _unit_triangular_inverses(Ls, row, col)
    for ch, T in zip(chains, Ts):
        v = ch['v_ref'][0, :, ch['sl']]
        ch['uw'] = _bdot(T, jnp.concatenate([v * ch['beta'], ch['kbeg']], axis=1))
    for ch in chains:
        q = ch['q_ref'][0, :, ch['sl']]
        w = ch['uw'][:, DV:]
        ch['ws'] = _bdot(jnp.concatenate([w, q * ch['eg']], axis=0), state_ref[ch['c']])
    for ch in chains:
        ch['v_new'] = ch['uw'][:, :DV] - ch['ws'][:C]
        a_intra = ch['s1'][C:] * ch['decay']
        ch['o_ref'][0, :, ch['sl']] = ch['ws'][C:] + _bdot(a_intra, ch['v_new'])
    for ch in chains:
        k = ch['k_ref'][0, :, ch['sl']]
        k_g = k * jnp.exp(ch['gl'] - ch['gcc'])
        c = ch['c']
        state_ref[c] = state_ref[c] * jnp.exp(ch['gl']) + _bdot(k_g.T, ch['v_new'])


def gdn_bidirectional_delta(q, k, v, beta, g):
    B, S, HD = q.shape
    C, H = GDN_CHUNK, GDN_HEADS
    N = S // C
    g_rows = g.reshape(B, N, C, 2 * H).transpose(0, 1, 3, 2)

    def fwd3(b, n):
        return (b, n, 0)

    def bwd3(b, n):
        return (b, N - 1 - n, 0)

    def fwd4(b, n):
        return (b, n, 0, 0)

    def bwd4(b, n):
        return (b, N - 1 - n, 0, 0)

    def specs(i3, i4):
        big = pl.BlockSpec((1, C, HD), i3)
        small = pl.BlockSpec((1, C, 2 * H), i3)
        return [big, big, big, small, small, pl.BlockSpec((1, 1, 2 * H, C), i4)]

    out = jax.ShapeDtypeStruct((B, S, HD), jnp.float32)
    return pl.pallas_call(
        _gdn_chunk_kernel,
        grid=(B, N),
        in_specs=specs(fwd3, fwd4) + specs(bwd3, bwd4),
        out_specs=[pl.BlockSpec((1, C, HD), fwd3), pl.BlockSpec((1, C, HD), bwd3)],
        out_shape=[out, out],
        scratch_shapes=[pltpu.VMEM((2 * H, GDN_DK, GDN_DV), jnp.float32)],
        compiler_params=pltpu.CompilerParams(dimension_semantics=("parallel", "arbitrary")),
        name="gdn_chunk_delta",
    )(q, k, v, g, beta, g_rows, q, k, v, g, beta, g_rows)


GDN_ROW_TILE = 256
HALO = 8
VMEM_LIMIT_BYTES = 48 * 1024 * 1024


def _gdn_in_kernel(h_ref, hp_ref, hn_ref, g_ref, wqkv_ref, wz_ref, wba_ref, cw_ref, alog_ref, dtb_ref,
                   q_ref, k_ref, v_ref, z_ref, beta_ref, gl_ref, *, tiles_per_seq):
    H, DK = GDN_HEADS, GDN_DK
    TM = h_ref.shape[0]
    n = TM + 2 * HALO
    i = pl.program_id(0) % tiles_per_seq
    first_row = jnp.where(i == 0, HALO, 0)
    end_row = jnp.where(i == tiles_per_seq - 1, HALO + TM, n)
    hx = jnp.concatenate([hp_ref[...], h_ref[...], hn_ref[...]], axis=0)
    hn = _rms(hx, g_ref[...]).astype(jnp.bfloat16)
    pre = _bdot(hn, wqkv_ref[...])
    r = lax.broadcasted_iota(jnp.int32, (n, 1), 0)
    pre = jnp.where((r >= first_row) & (r < end_row), pre, 0.0)
    pad = GDN_CONV // 2
    acc = None
    for j in range(GDN_CONV):
        shifted = pre if j == pad else pltpu.roll(pre, (pad - j) % n, 0)
        term = shifted[HALO:HALO + TM] * cw_ref[j:j + 1, :]
        acc = term if acc is None else acc + term
    qkv = _silu(acc)
    for hd in range(H):
        for off, ref, scale in ((0, q_ref, DK ** -0.5), (H * DK, k_ref, 1.0)):
            x = qkv[:, off + hd * DK:off + (hd + 1) * DK]
            ref[:, hd * DK:(hd + 1) * DK] = x * (lax.rsqrt(jnp.sum(x * x, axis=-1, keepdims=True) + NORM_EPS) * scale)
    v_ref[...] = qkv[:, 2 * H * DK:]
    hc = hn[HALO:HALO + TM]
    z_ref[...] = _bdot(hc, wz_ref[...])
    ba = _bdot(hc, wba_ref[...])
    beta_ref[...] = _sigmoid(ba[:, :2 * H])
    a = ba[:, 2 * H:] + dtb_ref[...]
    softplus = jnp.maximum(a, 0.0) + jnp.log(1.0 + jnp.exp(-jnp.abs(a)))
    gl_ref[...] = -jnp.exp(alog_ref[...]) * softplus


def _gdn_out_kernel(of_ref, ob_ref, z_ref, g_ref, w_ref, h_ref, o_ref):
    H, DV = GDN_HEADS, GDN_DV
    o = of_ref[...] + ob_ref[...]
    g = g_ref[...]
    parts = []
    for hd in range(H):
        x = o[:, hd * DV:(hd + 1) * DV]
        parts.append(_rms(x, g))
    y = jnp.concatenate(parts, axis=1) * _silu(z_ref[...])
    o_ref[...] = h_ref[...] + _bdot(y, w_ref[...])


def gdn_layer(h, norm_g, w_in, conv_w, A_log, dt_bias, out_norm_g, w_out):
    B, S, D = h.shape
    T = B * S
    H = GDN_HEADS
    HD = H * GDN_DK
    bf = jnp.bfloat16
    TM = GDN_ROW_TILE
    tiles_per_seq = S // TM
    hb = TM // HALO
    n_halo = T // HALO
    o = GDN_QKV_DIM + H * GDN_DV
    w_qkv = w_in[:, :GDN_QKV_DIM].astype(bf)
    w_z = w_in[:, GDN_QKV_DIM:o].astype(bf)
    w_ba = w_in[:, o:].astype(bf)
    cw = jnp.concatenate([conv_w, jnp.zeros((HALO - GDN_CONV, GDN_QKV_DIM), conv_w.dtype)], axis=0)
    row = lambda i: (i, 0)
    fixed = lambda i: (0, 0)
    f32 = jnp.float32
    ht = h.reshape(T, D)
    q, k, v, z, beta, gl = pl.pallas_call(
        functools.partial(_gdn_in_kernel, tiles_per_seq=tiles_per_seq),
        grid=(T // TM,),
        in_specs=[pl.BlockSpec((TM, D), row),
                  pl.BlockSpec((HALO, D), lambda i: (jnp.maximum(i * hb - 1, 0), 0)),
                  pl.BlockSpec((HALO, D), lambda i: (jnp.minimum((i + 1) * hb, n_halo - 1), 0)),
                  pl.BlockSpec((1, D), fixed),
                  pl.BlockSpec(w_qkv.shape, fixed), pl.BlockSpec(w_z.shape, fixed),
                  pl.BlockSpec(w_ba.shape, fixed), pl.BlockSpec(cw.shape, fixed),
                  pl.BlockSpec((1, 2 * H), fixed), pl.BlockSpec((1, 2 * H), fixed)],
        out_specs=[pl.BlockSpec((TM, HD), row), pl.BlockSpec((TM, HD), row), pl.BlockSpec((TM, HD), row),
                   pl.BlockSpec((TM, HD), row), pl.BlockSpec((TM, 2 * H), row), pl.BlockSpec((TM, 2 * H), row)],
        out_shape=[jax.ShapeDtypeStruct((T, HD), f32)] * 4 + [jax.ShapeDtypeStruct((T, 2 * H), f32)] * 2,
        compiler_params=pltpu.CompilerParams(vmem_limit_bytes=VMEM_LIMIT_BYTES),
        name="gdn_in_proj",
    )(ht, ht, ht, norm_g.reshape(1, D), w_qkv, w_z, w_ba, cw,
      A_log.reshape(1, 2 * H).astype(f32), dt_bias.reshape(1, 2 * H).astype(f32))

    o_fwd, o_bwd = gdn_bidirectional_delta(q.reshape(B, S, HD), k.reshape(B, S, HD), v.reshape(B, S, HD),
                                           beta.reshape(B, S, 2 * H), gl.reshape(B, S, 2 * H))

    h_new = pl.pallas_call(
        _gdn_out_kernel,
        grid=(T // ROW_TILE,),
        in_specs=[pl.BlockSpec((ROW_TILE, HD), row)] * 3 +
                 [pl.BlockSpec((1, GDN_DV), fixed), pl.BlockSpec((HD, D), fixed), pl.BlockSpec((ROW_TILE, D), row)],
        out_specs=pl.BlockSpec((ROW_TILE, D), row),
        out_shape=jax.ShapeDtypeStruct((T, D), f32),
        name="gdn_out_proj",
    )(o_fwd.reshape(T, HD), o_bwd.reshape(T, HD), z, out_norm_g.reshape(1, GDN_DV), w_out.astype(bf), ht)
    return h_new.reshape(B, S, D)


def _sigmoid(x):
    return 1.0 / (1.0 + jnp.exp(-x))


def _silu(x):
    return x * _sigmoid(x)


def _moe_router_kernel(h_ref, g_ref, w_ref, xn_ref, aff_ref):
    xn = _rms(h_ref[...], g_ref[...])
    xn_ref[...] = xn.astype(xn_ref.dtype)
    logits = _bdot(xn, w_ref[...])
    e = jnp.exp(logits - jnp.max(logits, axis=-1, keepdims=True))
    aff_ref[...] = e / jnp.sum(e, axis=-1, keepdims=True)


def _moe_ffn_kernel(x_ref, wg_ref, wu_ref, wd_ref, gate_ref, y_ref):
    x = x_ref[0]
    hid = _silu(_bdot(x, wg_ref[0])) * _bdot(x, wu_ref[0])
    y_ref[0] = (_bdot(hid, wd_ref[0]) * gate_ref[0]).astype(y_ref.dtype)


COMBINE_TILE = 256
STRIP_ROWS = 64
STRIP_ALIGN = 16
STRIPS_PER_DOT = 4


def _moe_combine_ple_kernel(base_ref, off_ref, cnt_ref,
                            h_ref, p_ref, g_ref, wg_ref, wp_ref, fg_ref, ye_ref, tok_ref,
                            o_ref, ybuf, tbuf, sems, yextra, textra, extra_sems, acc_ref,
                            *, n_tiles, total_rows, final_norm):
    E, TB, R = N_EXPERTS, COMBINE_TILE, STRIP_ROWS
    j = pl.program_id(0)
    cur = j % 2

    def strip_copies(tile, e, b):
        start = pl.multiple_of(base_ref[e * n_tiles + tile], STRIP_ALIGN)
        rows = pl.ds(e * R, R)
        return (pltpu.make_async_copy(ye_ref.at[pl.ds(start, R), :], ybuf.at[b, rows, :], sems.at[0, b, e]),
                pltpu.make_async_copy(tok_ref.at[pl.ds(start, R), :], tbuf.at[b, rows, :], sems.at[1, b, e]))

    @pl.when(j == 0)
    def _():
        for e in range(E):
            for cp in strip_copies(0, e, 0):
                cp.start()

    @pl.when(j + 1 < n_tiles)
    def _():
        for e in range(E):
            for cp in strip_copies(j + 1, e, 1 - cur):
                cp.start()

    for e in range(E):
        for cp in strip_copies(j, e, cur):
            cp.wait()

    tile_tok0 = j * TB
    K = STRIPS_PER_DOT * R
    lane = lax.broadcasted_iota(jnp.int32, (K, TB), 1)
    srow = lax.broadcasted_iota(jnp.int32, (K, 1), 0)
    acc = jnp.zeros(acc_ref.shape, jnp.float32)
    for grp in range(E // STRIPS_PER_DOT):
        wanted = None
        for i in range(STRIPS_PER_DOT):
            e = grp * STRIPS_PER_DOT + i
            lo = off_ref[e * n_tiles + j] + i * R
            hi = jnp.minimum(lo + cnt_ref[e * n_tiles + j], (i + 1) * R)
            w = (srow >= lo) & (srow < hi)
            wanted = w if wanted is None else wanted | w
        rows = pl.ds(grp * K, K)
        tok = tbuf[cur, rows, 0:1] - tile_tok0
        hot_t = jnp.where(wanted & (lane == tok), 1.0, 0.0)
        acc = acc + _bdot(hot_t.T, ybuf[cur, rows, :])
    acc_ref[...] = acc

    lane_r = lax.broadcasted_iota(jnp.int32, (R, TB), 1)
    srow_r = lax.broadcasted_iota(jnp.int32, (R, 1), 0)
    for e in range(E):
        base = base_ref[e * n_tiles + j]
        first = base + off_ref[e * n_tiles + j]
        end = first + cnt_ref[e * n_tiles + j]
        n_sub = (end - base + R - 1) // R

        def body(k, carry, base=base, first=first, end=end):
            want0 = base + k * R
            start = pl.multiple_of(jnp.minimum(want0, total_rows - R), STRIP_ALIGN)
            cy = pltpu.make_async_copy(ye_ref.at[pl.ds(start, R), :], yextra, extra_sems.at[0])
            ct = pltpu.make_async_copy(tok_ref.at[pl.ds(start, R), :], textra, extra_sems.at[1])
            cy.start()
            ct.start()
            cy.wait()
            ct.wait()
            grow = srow_r + start
            wanted = (grow >= jnp.maximum(want0, first)) & (grow < jnp.minimum(want0 + R, end))
            hot_t = jnp.where(wanted & (lane_r == textra[:, 0:1] - tile_tok0), 1.0, 0.0)
            acc_ref[...] += _bdot(hot_t.T, yextra[...])
            return carry

        lax.fori_loop(1, n_sub, body, 0)

    hm = h_ref[...] + acc_ref[...]
    gate = _sigmoid(_bdot(_rms(hm, g_ref[...]), wg_ref[...]))
    out = hm + _bdot(p_ref[...], wp_ref[...]) * gate
    o_ref[...] = _rms(out, fg_ref[...]) if final_norm else out


def moe_ple_layer(h, p, norm_g, w_router, w_gate, w_up, w_down, ple_norm_g, ple_w_gate, ple_w_proj,
                  final_g, final_norm):
    T, D = h.shape
    E, F = N_EXPERTS, w_gate.shape[-1]
    cap = EC_CAPACITY_FACTOR * T // E
    bf = jnp.bfloat16
    row = lambda i: (i, 0)
    fixed = lambda i: (0, 0)
    xn, aff = pl.pallas_call(
        _moe_router_kernel,
        grid=(T // ROW_TILE,),
        in_specs=[pl.BlockSpec((ROW_TILE, D), row), pl.BlockSpec((1, D), fixed),
                  pl.BlockSpec((D, E), fixed)],
        out_specs=[pl.BlockSpec((ROW_TILE, D), row), pl.BlockSpec((ROW_TILE, E), row)],
        out_shape=[jax.ShapeDtypeStruct((T, D), bf), jax.ShapeDtypeStruct((T, E), jnp.float32)],
        name="moe_router",
    )(h, norm_g.reshape(1, D), w_router.astype(bf))
    aff_t = aff.T
    _, idx = lax.top_k(aff_t, cap)
    idx = jnp.sort(idx, axis=1)
    gate = jnp.take_along_axis(aff_t, idx, axis=1)
    xe = jnp.take(xn, idx, axis=0)
    ye = pl.pallas_call(
        _moe_ffn_kernel,
        grid=(E, cap // ROW_TILE),
        in_specs=[pl.BlockSpec((1, ROW_TILE, D), lambda e, i: (e, i, 0)),
                  pl.BlockSpec((1, D, F), lambda e, i: (e, 0, 0)),
                  pl.BlockSpec((1, D, F), lambda e, i: (e, 0, 0)),
                  pl.BlockSpec((1, F, D), lambda e, i: (e, 0, 0)),
                  pl.BlockSpec((1, ROW_TILE, 1), lambda e, i: (e, i, 0))],
        out_specs=pl.BlockSpec((1, ROW_TILE, D), lambda e, i: (e, i, 0)),
        out_shape=jax.ShapeDtypeStruct((E, cap, D), bf),
        compiler_params=pltpu.CompilerParams(dimension_semantics=("parallel", "arbitrary")),
        name="moe_expert_ffn",
    )(xe, w_gate.astype(bf), w_up.astype(bf), w_down.astype(bf), gate[..., None])

    TB, R = COMBINE_TILE, STRIP_ROWS
    n_tiles = T // TB
    total_rows = E * cap
    edges = jnp.arange(n_tiles + 1, dtype=jnp.int32) * TB
    lo = jax.vmap(lambda row_idx: jnp.searchsorted(row_idx, edges, side='left'))(idx).astype(jnp.int32)
    first = lo[:, :-1] + (jnp.arange(E, dtype=jnp.int32) * cap)[:, None]
    cnt = lo[:, 1:] - lo[:, :-1]
    base = jnp.minimum(first // STRIP_ALIGN * STRIP_ALIGN, total_rows - R)
    off = first - base
    tok = jnp.broadcast_to(idx.reshape(total_rows, 1), (total_rows, LANES))

    P = p.shape[1]
    tile = lambda i, *_: (i, 0)
    const = lambda i, *_: (0, 0)
    return pl.pallas_call(
        functools.partial(_moe_combine_ple_kernel, n_tiles=n_tiles, total_rows=total_rows,
                          final_norm=final_norm),
        grid_spec=pltpu.PrefetchScalarGridSpec(
            num_scalar_prefetch=3,
            grid=(n_tiles,),
            in_specs=[pl.BlockSpec((TB, D), tile), pl.BlockSpec((TB, P), tile),
                      pl.BlockSpec((1, D), const), pl.BlockSpec((D, D), const), pl.BlockSpec((P, D), const),
                      pl.BlockSpec((1, D), const),
                      pl.BlockSpec(memory_space=pl.ANY), pl.BlockSpec(memory_space=pl.ANY)],
            out_specs=pl.BlockSpec((TB, D), tile),
            scratch_shapes=[pltpu.VMEM((2, E * R, D), bf), pltpu.VMEM((2, E * R, LANES), jnp.int32),
                            pltpu.SemaphoreType.DMA((2, 2, E)),
                            pltpu.VMEM((R, D), bf), pltpu.VMEM((R, LANES), jnp.int32),
                            pltpu.SemaphoreType.DMA((2,)),
                            pltpu.VMEM((TB, D), jnp.float32)]),
        out_shape=jax.ShapeDtypeStruct((T, D), jnp.float32),
        compiler_params=pltpu.CompilerParams(dimension_semantics=("arbitrary",)),
        name="moe_combine_ple",
    )(base.reshape(-1), off.reshape(-1), cnt.reshape(-1),
      h, p, ple_norm_g.reshape(1, D), ple_w_gate.astype(bf), ple_w_proj.astype(bf), final_g.reshape(1, D),
      ye.reshape(total_rows, D), tok)


def encoder_trunk(x, p, W):
    h = x
    for i in range(DEPTH):
        j = i // N_MIXERS
        if i % N_MIXERS == 0:
            h = mla_layer(h, W['norm_mix_g'][i], W['mla_w_in'][j], W['mla_q_norm_g'][j], W['mla_w_qb'][j],
                          W['mla_kv_norm_g'][j], W['mla_w_kvb'][j], W['mla_w_out'][j])
        else:
            h = gdn_layer(h, W['norm_mix_g'][i], W['gdn_w_in'][j], W['gdn_conv_w'][j], W['gdn_A_log'][j],
                          W['gdn_dt_bias'][j], W['gdn_norm_g'][j], W['gdn_w_out'][j])
        B, S, D = h.shape
        ht = moe_ple_layer(h.reshape(B * S, D), p[i].reshape(B * S, PLE_DIM), W['norm_ffn_g'][i],
                           W['moe_w_router'][i], W['moe_w_gate'][i], W['moe_w_up'][i], W['moe_w_down'][i],
                           W['norm_ple_g'][i], W['ple_w_gate'][i], W['ple_w_proj'][i],
                           W['final_norm_g'], final_norm=(i == DEPTH - 1))
        h = ht.reshape(B, S, D)
    return h


def kernel(x_prompt, x_sample, p_prompt, p_sample, norm_mix_g, norm_ffn_g, norm_ple_g, final_norm_g,
           mla_w_in, mla_q_norm_g, mla_w_qb, mla_kv_norm_g, mla_w_kvb, mla_w_out,
           gdn_w_in, gdn_conv_w, gdn_A_log, gdn_dt_bias, gdn_norm_g, gdn_w_out,
           moe_w_router, moe_w_gate, moe_w_up, moe_w_down, ple_w_proj, ple_w_gate):
    W = {
        'norm_mix_g': norm_mix_g, 'norm_ffn_g': norm_ffn_g, 'norm_ple_g': norm_ple_g,
        'final_norm_g': final_norm_g,
        'mla_w_in': mla_w_in, 'mla_q_norm_g': mla_q_norm_g, 'mla_w_qb': mla_w_qb,
        'mla_kv_norm_g': mla_kv_norm_g, 'mla_w_kvb': mla_w_kvb, 'mla_w_out': mla_w_out,
        'gdn_w_in': gdn_w_in, 'gdn_conv_w': gdn_conv_w, 'gdn_A_log': gdn_A_log,
        'gdn_dt_bias': gdn_dt_bias, 'gdn_norm_g': gdn_norm_g, 'gdn_w_out': gdn_w_out,
        'moe_w_router': moe_w_router, 'moe_w_gate': moe_w_gate, 'moe_w_up': moe_w_up,
        'moe_w_down': moe_w_down, 'ple_w_proj': ple_w_proj, 'ple_w_gate': ple_w_gate,
    }
    y_prompt = encoder_trunk(x_prompt, p_prompt, W)
    y_sample = encoder_trunk(x_sample, p_sample, W)
    return (y_prompt, y_sample)
```

```python
import functools

import jax
import jax.numpy as jnp
from jax import lax
from jax.experimental import pallas as pl
from jax.experimental.pallas import tpu as pltpu

D_MODEL = 1024
DEPTH = 4
PLE_DIM = 256
N_MIXERS = 2
MLA_HEADS = 8
MLA_NOPE_DIM = 128
MLA_ROPE_DIM = 64
MLA_V_DIM = 128
MLA_Q_LORA = 256
MLA_KV_LORA = 128
ROPE_THETA = 10000.0
Q_BLOCK = 128
GDN_HEADS = 8
GDN_DK = 128
GDN_DV = 128
GDN_CONV = 5
GDN_CHUNK = 64
GDN_QKV_DIM = 2 * GDN_HEADS * GDN_DK + GDN_HEADS * GDN_DV
N_EXPERTS = 16
EC_CAPACITY_FACTOR = 2
NORM_EPS = 1e-6


def rope_tables(seq):
    pos = jnp.arange(seq, dtype=jnp.float32)
    inv = ROPE_THETA ** (-jnp.arange(0, MLA_ROPE_DIM, 2, dtype=jnp.float32) / MLA_ROPE_DIM)
    ang = pos[:, None] * inv[None, :]
    return jnp.cos(ang), jnp.sin(ang)


LANES = 128
ROW_TILE = 512
ATTN_Q_TILE = 512
ATTN_HEADS_PER_STEP = 2
VMEM_LIMIT_BYTES = 48 * 1024 * 1024
MLA_QK_PAD = 2 * LANES


def _rms(x, g):
    return x * lax.rsqrt(jnp.mean(x * x, axis=-1, keepdims=True) + NORM_EPS) * g


def _mla_proj_kernel(h_ref, g_ref, w_in_ref, qg_ref, w_qb_ref, kvg_ref, w_k_ref, w_v_ref,
                     cos_ref, sin_ref, q_ref, k_ref, v_ref):
    H, NP = MLA_HEADS, MLA_NOPE_DIM
    scale = (MLA_NOPE_DIM + MLA_ROPE_DIM) ** -0.5
    hn = _rms(h_ref[...], g_ref[...])
    lat = _bdot(hn, w_in_ref[...])
    q_lat = lat[:, :MLA_Q_LORA]
    kv_lat = lat[:, MLA_Q_LORA:MLA_Q_LORA + MLA_KV_LORA]
    o = MLA_Q_LORA + MLA_KV_LORA
    cos, sin = cos_ref[...], sin_ref[...]
    k_rope = (lat[:, o:o + LANES] * cos + lat[:, o + LANES:o + 2 * LANES] * sin).astype(jnp.bfloat16)
    q = _bdot(_rms(q_lat, qg_ref[...]), w_qb_ref[...]) * scale
    kvn = _rms(kv_lat, kvg_ref[...])
    k_nope = _bdot(kvn, w_k_ref[...]).astype(jnp.bfloat16)
    v_ref[...] = _bdot(kvn, w_v_ref[...]).astype(jnp.bfloat16)
    for hd in range(H):
        a = q[:, (H + hd) * NP:(H + hd + 1) * NP]
        b = q[:, (2 * H + hd) * NP:(2 * H + hd + 1) * NP]
        q_ref[:, hd * MLA_QK_PAD:hd * MLA_QK_PAD + NP] = q[:, hd * NP:(hd + 1) * NP].astype(jnp.bfloat16)
        q_ref[:, hd * MLA_QK_PAD + NP:(hd + 1) * MLA_QK_PAD] = (a * cos + b * sin).astype(jnp.bfloat16)
        k_ref[:, hd * MLA_QK_PAD:hd * MLA_QK_PAD + NP] = k_nope[:, hd * NP:(hd + 1) * NP]
        k_ref[:, hd * MLA_QK_PAD + NP:(hd + 1) * MLA_QK_PAD] = k_rope


def _mla_attn_kernel(q_ref, k_ref, v_ref, o_ref):
    heads = range(ATTN_HEADS_PER_STEP)
    qk = lambda hd: slice(hd * MLA_QK_PAD, (hd + 1) * MLA_QK_PAD)
    vv = lambda hd: slice(hd * MLA_V_DIM, (hd + 1) * MLA_V_DIM)
    s = [_bdot_nt(q_ref[0, :, qk(hd)], k_ref[0, :, qk(hd)]) for hd in heads]
    p = [jnp.exp(x - jnp.max(x, axis=-1, keepdims=True)) for x in s]
    denom = [jnp.sum(x, axis=-1, keepdims=True) for x in p]
    for hd in heads:
        o_ref[0, :, vv(hd)] = (_bdot(p[hd], v_ref[0, :, vv(hd)]) / denom[hd]).astype(o_ref.dtype)


def _matmul_residual_kernel(x_ref, w_ref, h_ref, o_ref):
    o_ref[...] = h_ref[...] + _bdot(x_ref[...], w_ref[...])


def matmul_residual(x, w, h, name):
    T, K = x.shape
    D = w.shape[1]
    return pl.pallas_call(
        _matmul_residual_kernel,
        grid=(T // ROW_TILE,),
        in_specs=[pl.BlockSpec((ROW_TILE, K), lambda i: (i, 0)),
                  pl.BlockSpec((K, D), lambda i: (0, 0)),
                  pl.BlockSpec((ROW_TILE, D), lambda i: (i, 0))],
        out_specs=pl.BlockSpec((ROW_TILE, D), lambda i: (i, 0)),
        out_shape=jax.ShapeDtypeStruct((T, D), jnp.float32),
        name=name,
    )(x, w, h)


def _rope_swapped_columns(w):
    half = MLA_ROPE_DIM // 2
    pad = jnp.zeros(w.shape[:-1] + (LANES - MLA_ROPE_DIM,), w.dtype)
    plain = jnp.concatenate([w, pad], axis=-1)
    swapped = jnp.concatenate([w[..., half:], w[..., :half], pad], axis=-1)
    return plain, swapped


def mla_layer(h, norm_g, w_in, q_norm_g, w_qb, kv_norm_g, w_kvb, w_out):
    B, S, D = h.shape
    T = B * S
    H, NP, R = MLA_HEADS, MLA_NOPE_DIM, MLA_ROPE_DIM
    bf = jnp.bfloat16
    o = MLA_Q_LORA + MLA_KV_LORA
    kr_plain, kr_swapped = _rope_swapped_columns(w_in[:, o:])
    w_in_ext = jnp.concatenate([w_in[:, :o], kr_plain, kr_swapped], axis=1).astype(bf)
    w_qb_h = w_qb.reshape(MLA_Q_LORA, H, NP + R)
    qr_plain, qr_swapped = _rope_swapped_columns(w_qb_h[:, :, NP:])
    w_qb_ext = jnp.concatenate([w_qb_h[:, :, :NP].reshape(MLA_Q_LORA, H * NP),
                                qr_plain.reshape(MLA_Q_LORA, H * LANES),
                                qr_swapped.reshape(MLA_Q_LORA, H * LANES)], axis=1).astype(bf)
    w_kvb_h = w_kvb.reshape(MLA_KV_LORA, H, NP + MLA_V_DIM)
    w_k = w_kvb_h[:, :, :NP].reshape(MLA_KV_LORA, H * NP).astype(bf)
    w_v = w_kvb_h[:, :, NP:].reshape(MLA_KV_LORA, H * MLA_V_DIM).astype(bf)
    cos, sin = rope_tables(S)
    zeros = jnp.zeros((S, LANES - R), jnp.float32)
    cos_pad = jnp.concatenate([cos, cos, zeros], axis=1)
    sin_pad = jnp.concatenate([-sin, sin, zeros], axis=1)

    tiles_per_seq = S // ROW_TILE
    row = lambda i: (i, 0)
    fixed = lambda i: (0, 0)
    pos = lambda i: (i % tiles_per_seq, 0)
    q_cat, k_cat, v = pl.pallas_call(
        _mla_proj_kernel,
        grid=(T // ROW_TILE,),
        in_specs=[pl.BlockSpec((ROW_TILE, D), row),
                  pl.BlockSpec((1, D), fixed),
                  pl.BlockSpec(w_in_ext.shape, fixed),
                  pl.BlockSpec((1, MLA_Q_LORA), fixed),
                  pl.BlockSpec(w_qb_ext.shape, fixed),
                  pl.BlockSpec((1, MLA_KV_LORA), fixed),
                  pl.BlockSpec(w_k.shape, fixed),
                  pl.BlockSpec(w_v.shape, fixed),
                  pl.BlockSpec((ROW_TILE, LANES), pos),
                  pl.BlockSpec((ROW_TILE, LANES), pos)],
        out_specs=[pl.BlockSpec((ROW_TILE, H * MLA_QK_PAD), row),
                   pl.BlockSpec((ROW_TILE, H * MLA_QK_PAD), row),
                   pl.BlockSpec((ROW_TILE, H * MLA_V_DIM), row)],
        out_shape=[jax.ShapeDtypeStruct((T, H * MLA_QK_PAD), bf),
                   jax.ShapeDtypeStruct((T, H * MLA_QK_PAD), bf),
                   jax.ShapeDtypeStruct((T, H * MLA_V_DIM), bf)],
        name="mla_proj",
    )(h.reshape(T, D), norm_g.reshape(1, D), w_in_ext, q_norm_g.reshape(1, -1), w_qb_ext,
      kv_norm_g.reshape(1, -1), w_k, w_v, cos_pad, sin_pad)

    G = ATTN_HEADS_PER_STEP
    o_attn = pl.pallas_call(
        _mla_attn_kernel,
        grid=(B, H // G, S // ATTN_Q_TILE),
        in_specs=[pl.BlockSpec((1, ATTN_Q_TILE, G * MLA_QK_PAD), lambda b, hd, i: (b, i, hd)),
                  pl.BlockSpec((1, S, G * MLA_QK_PAD), lambda b, hd, i: (b, 0, hd)),
                  pl.BlockSpec((1, S, G * MLA_V_DIM), lambda b, hd, i: (b, 0, hd))],
        out_specs=pl.BlockSpec((1, ATTN_Q_TILE, G * MLA_V_DIM), lambda b, hd, i: (b, i, hd)),
        out_shape=jax.ShapeDtypeStruct((B, S, H * MLA_V_DIM), bf),
        compiler_params=pltpu.CompilerParams(
            dimension_semantics=("parallel", "parallel", "arbitrary"),
            vmem_limit_bytes=VMEM_LIMIT_BYTES),
        name="mla_attention",
    )(q_cat.reshape(B, S, -1), k_cat.reshape(B, S, -1), v.reshape(B, S, -1))

    h_new = matmul_residual(o_attn.reshape(T, H * MLA_V_DIM), w_out.astype(bf), h.reshape(T, D),
                            name="mla_out_proj")
    return h_new.reshape(B, S, D)


def _bdot(a, b):
    return jnp.dot(a.astype(jnp.bfloat16), b.astype(jnp.bfloat16), preferred_element_type=jnp.float32)


def _bdot_nt(a, b):
    return lax.dot_general(a.astype(jnp.bfloat16), b.astype(jnp.bfloat16),
                           (((1,), (1,)), ((), ())), preferred_element_type=jnp.float32)


def _unit_triangular_inverses(Ls, row, col):
    C = Ls[0].shape[0]
    eye = (row == col).astype(jnp.float32)

    def same_block(s):
        return (row // s) == (col // s)

    blk8 = same_block(8)
    L8 = [jnp.where(blk8, L, 0.0) for L in Ls]
    P2 = [_bdot(a, a) for a in L8]
    P4 = [_bdot(a, a) for a in P2]
    X = [eye - a for a in L8]
    X = [x + _bdot(x, p) for x, p in zip(X, P2)]
    X = [x + _bdot(x, p) for x, p in zip(X, P4)]
    s = 8
    while s < C:
        pair = same_block(2 * s) & jnp.logical_not(same_block(s))
        XC = [_bdot(x, jnp.where(pair, L, 0.0)) for x, L in zip(X, Ls)]
        X = [x - _bdot(xc, x) for x, xc in zip(X, XC)]
        s *= 2
    return X


def _gdn_chunk_kernel(qf_ref, kf_ref, vf_ref, gcf_ref, bcf_ref, grf_ref,
                      qb_ref, kb_ref, vb_ref, gcb_ref, bcb_ref, grb_ref,
                      of_ref, ob_ref, state_ref):
    H, C, DK, DV = GDN_HEADS, GDN_CHUNK, GDN_DK, GDN_DV

    @pl.when(pl.program_id(1) == 0)
    def _():
        state_ref[...] = jnp.zeros_like(state_ref)

    row = lax.broadcasted_iota(jnp.int32, (C, C), 0)
    col = lax.broadcasted_iota(jnp.int32, (C, C), 1)
    dirs = ((0, qf_ref, kf_ref, vf_ref, gcf_ref, bcf_ref, grf_ref, of_ref),
            (1, qb_ref, kb_ref, vb_ref, gcb_ref, bcb_ref, grb_ref, ob_ref))
    chains = []
    for d, q_ref, k_ref, v_ref, gc_ref, bc_ref, gr_ref, o_ref in dirs:
        if d == 0:
            incl, strict, last = row >= col, row > col, C - 1
        else:
            incl, strict, last = row <= col, row < col, 0
        tri = incl.astype(jnp.float32)
        gc_cols = jnp.dot(tri, gc_ref[0], precision=lax.Precision.HIGHEST,
                          preferred_element_type=jnp.float32)
        gc_rows = jnp.dot(gr_ref[0, 0], tri.T, precision=lax.Precision.HIGHEST,
                          preferred_element_type=jnp.float32)
        beta_cols = bc_ref[0]
        for h in range(H):
            c = d * H + h
            chains.append(dict(
                c=c, incl=incl, strict=strict, o_ref=o_ref, sl=slice(h * DK, (h + 1) * DK),
                q_ref=q_ref, k_ref=k_ref, v_ref=v_ref,
                beta=beta_cols[:, c:c + 1], gcc=gc_cols[:, c:c + 1], gcr=gc_rows[c:c + 1, :],
                gl=gc_cols[last:last + 1, c:c + 1]))

    for ch in chains:
        ch['decay'] = jnp.where(ch['incl'], jnp.exp(jnp.where(ch['incl'], ch['gcc'] - ch['gcr'], 0.0)), 0.0)
        ch['eg'] = jnp.exp(ch['gcc'])
    for ch in chains:
        k = ch['k_ref'][0, :, ch['sl']]
        q = ch['q_ref'][0, :, ch['sl']]
        kb = k * ch['beta']
        ch['kbeg'] = kb * ch['eg']
        ch['s1'] = _bdot_nt(jnp.concatenate([kb, q], axis=0), k)
    Ls = [jnp.where(ch['strict'], ch['s1'][:C] * ch['decay'], 0.0) for ch in chains]
    Ts = _unit_triangular_inverses(Ls, row, col)
    for ch, T in zip(chains, Ts):
        v = ch['v_ref'][0, :, ch['sl']]
        ch['uw'] = _bdot(T, jnp.concatenate([v * ch['beta'], ch['kbeg']], axis=1))
    for ch in chains:
        q = ch['q_ref'][0, :, ch['sl']]
        w = ch['uw'][:, DV:]
        ch['ws'] = _bdot(jnp.concatenate([w, q * ch['eg']], axis=0), state_ref[ch['c']])
    for ch in chains:
        ch['v_new'] = ch['uw'][:, :DV] - ch['ws'][:C]
        a_intra = ch['s1'][C:] * ch['decay']
        ch['o_ref'][0, :, ch['sl']] = ch['ws'][C:] + _bdot(a_intra, ch['v_new'])
    for ch in chains:
        k = ch['k_ref'][0, :, ch['sl']]
        k_g = k * jnp.exp(ch['gl'] - ch['gcc'])
        c = ch['c']
        state_ref[c] = state_ref[c] * jnp.exp(ch['gl']) + _bdot(k_g.T, ch['v_new'])


def gdn_bidirectional_delta(q, k, v, beta, g):
    B, S, HD = q.shape
    C, H = GDN_CHUNK, GDN_HEADS
    N = S // C
    g_rows = g.reshape(B, N, C, 2 * H).transpose(0, 1, 3, 2)

    def fwd3(b, n):
        return (b, n, 0)

    def bwd3(b, n):
        return (b, N - 1 - n, 0)

    def fwd4(b, n):
        return (b, n, 0, 0)

    def bwd4(b, n):
        return (b, N - 1 - n, 0, 0)

    def specs(i3, i4):
        big = pl.BlockSpec((1, C, HD), i3)
        small = pl.BlockSpec((1, C, 2 * H), i3)
        return [big, big, big, small, small, pl.BlockSpec((1, 1, 2 * H, C), i4)]

    out = jax.ShapeDtypeStruct((B, S, HD), jnp.float32)
    return pl.pallas_call(
        _gdn_chunk_kernel,
        grid=(B, N),
        in_specs=specs(fwd3, fwd4) + specs(bwd3, bwd4),
        out_specs=[pl.BlockSpec((1, C, HD), fwd3), pl.BlockSpec((1, C, HD), bwd3)],
        out_shape=[out, out],
        scratch_shapes=[pltpu.VMEM((2 * H, GDN_DK, GDN_DV), jnp.float32)],
        compiler_params=pltpu.CompilerParams(dimension_semantics=("parallel", "arbitrary")),
        name="gdn_chunk_delta",
    )(q, k, v, g, beta, g_rows, q, k, v, g, beta, g_rows)


GDN_ROW_TILE = 256
HALO = 8


def _gdn_in_kernel(h_ref, hp_ref, hn_ref, g_ref, wqkv_ref, wz_ref, wba_ref, cw_ref, alog_ref, dtb_ref,
                   q_ref, k_ref, v_ref, z_ref, beta_ref, gl_ref, pre_ref, *, tiles_per_seq):
    H, DK = GDN_HEADS, GDN_DK
    TM = h_ref.shape[0]
    i = pl.program_id(0) % tiles_per_seq
    keep_prev = jnp.where(i == 0, 0.0, 1.0)
    keep_next = jnp.where(i == tiles_per_seq - 1, 0.0, 1.0)
    hx = jnp.concatenate([hp_ref[...] * keep_prev, h_ref[...], hn_ref[...] * keep_next], axis=0)
    hn = _rms(hx, g_ref[...]).astype(jnp.bfloat16)
    pre = _bdot(hn, wqkv_ref[...])
    pre_ref[...] = pre
    pad = GDN_CONV // 2
    acc = None
    for j in range(GDN_CONV):
        term = pre_ref[pl.ds(HALO + j - pad, TM), :] * cw_ref[j:j + 1, :]
        acc = term if acc is None else acc + term
    qkv = _silu(acc)
    for hd in range(H):
        for off, ref, scale in ((0, q_ref, DK ** -0.5), (H * DK, k_ref, 1.0)):
            x = qkv[:, off + hd * DK:off + (hd + 1) * DK]
            ref[:, hd * DK:(hd + 1) * DK] = x * (lax.rsqrt(jnp.sum(x * x, axis=-1, keepdims=True) + NORM_EPS) * scale)
    v_ref[...] = qkv[:, 2 * H * DK:]
    hc = hn[HALO:HALO + TM]
    z_ref[...] = _bdot(hc, wz_ref[...])
    ba = _bdot(hc, wba_ref[...])
    beta_ref[...] = _sigmoid(ba[:, :2 * H])
    a = ba[:, 2 * H:] + dtb_ref[...]
    softplus = jnp.maximum(a, 0.0) + jnp.log(1.0 + jnp.exp(-jnp.abs(a)))
    gl_ref[...] = -jnp.exp(alog_ref[...]) * softplus


def _gdn_out_kernel(of_ref, ob_ref, z_ref, g_ref, w_ref, h_ref, o_ref):
    H, DV = GDN_HEADS, GDN_DV
    o = of_ref[...] + ob_ref[...]
    g = g_ref[...]
    parts = []
    for hd in range(H):
        x = o[:, hd * DV:(hd + 1) * DV]
        parts.append(_rms(x, g))
    y = jnp.concatenate(parts, axis=1) * _silu(z_ref[...])
    o_ref[...] = h_ref[...] + _bdot(y, w_ref[...])


def gdn_layer(h, norm_g, w_in, conv_w, A_log, dt_bias, out_norm_g, w_out):
    B, S, D = h.shape
    T = B * S
    H = GDN_HEADS
    HD = H * GDN_DK
    bf = jnp.bfloat16
    TM = GDN_ROW_TILE
    tiles_per_seq = S // TM
    hb = TM // HALO
    n_halo = T // HALO
    o = GDN_QKV_DIM + H * GDN_DV
    w_qkv = w_in[:, :GDN_QKV_DIM].astype(bf)
    w_z = w_in[:, GDN_QKV_DIM:o].astype(bf)
    w_ba = w_in[:, o:].astype(bf)
    cw = jnp.concatenate([conv_w, jnp.zeros((HALO - GDN_CONV, GDN_QKV_DIM), conv_w.dtype)], axis=0)
    row = lambda i: (i, 0)
    fixed = lambda i: (0, 0)
    f32 = jnp.float32
    ht = h.reshape(T, D)
    q, k, v, z, beta, gl = pl.pallas_call(
        functools.partial(_gdn_in_kernel, tiles_per_seq=tiles_per_seq),
        grid=(T // TM,),
        in_specs=[pl.BlockSpec((TM, D), row),
                  pl.BlockSpec((HALO, D), lambda i: (jnp.maximum(i * hb - 1, 0), 0)),
                  pl.BlockSpec((HALO, D), lambda i: (jnp.minimum((i + 1) * hb, n_halo - 1), 0)),
                  pl.BlockSpec((1, D), fixed),
                  pl.BlockSpec(w_qkv.shape, fixed), pl.BlockSpec(w_z.shape, fixed),
                  pl.BlockSpec(w_ba.shape, fixed), pl.BlockSpec(cw.shape, fixed),
                  pl.BlockSpec((1, 2 * H), fixed), pl.BlockSpec((1, 2 * H), fixed)],
        out_specs=[pl.BlockSpec((TM, HD), row), pl.BlockSpec((TM, HD), row), pl.BlockSpec((TM, HD), row),
                   pl.BlockSpec((TM, HD), row), pl.BlockSpec((TM, 2 * H), row), pl.BlockSpec((TM, 2 * H), row)],
        out_shape=[jax.ShapeDtypeStruct((T, HD), f32)] * 4 + [jax.ShapeDtypeStruct((T, 2 * H), f32)] * 2,
        scratch_shapes=[pltpu.VMEM((TM + 2 * HALO, GDN_QKV_DIM), f32)],
        compiler_params=pltpu.CompilerParams(vmem_limit_bytes=VMEM_LIMIT_BYTES),
        name="gdn_in_proj",
    )(ht, ht, ht, norm_g.reshape(1, D), w_qkv, w_z, w_ba, cw,
      A_log.reshape(1, 2 * H).astype(f32), dt_bias.reshape(1, 2 * H).astype(f32))

    o_fwd, o_bwd = gdn_bidirectional_delta(q.reshape(B, S, HD), k.reshape(B, S, HD), v.reshape(B, S, HD),
                                           beta.reshape(B, S, 2 * H), gl.reshape(B, S, 2 * H))

    h_new = pl.pallas_call(
        _gdn_out_kernel,
        grid=(T // ROW_TILE,),
        in_specs=[pl.BlockSpec((ROW_TILE, HD), row)] * 3 +
                 [pl.BlockSpec((1, GDN_DV), fixed), pl.BlockSpec((HD, D), fixed), pl.BlockSpec((ROW_TILE, D), row)],
        out_specs=pl.BlockSpec((ROW_TILE, D), row),
        out_shape=jax.ShapeDtypeStruct((T, D), f32),
        name="gdn_out_proj",
    )(o_fwd.reshape(T, HD), o_bwd.reshape(T, HD), z, out_norm_g.reshape(1, GDN_DV), w_out.astype(bf), ht)
    return h_new.reshape(B, S, D)


def _sigmoid(x):
    return 0.5 * jnp.tanh(0.5 * x) + 0.5


def _silu(x):
    return x * _sigmoid(x)


def _moe_router_kernel(h_ref, g_ref, w_ref, xn_ref, aff_ref):
    xn = _rms(h_ref[...], g_ref[...])
    xn_ref[...] = xn.astype(xn_ref.dtype)
    logits = _bdot(xn, w_ref[...])
    e = jnp.exp(logits - jnp.max(logits, axis=-1, keepdims=True))
    aff_ref[...] = e / jnp.sum(e, axis=-1, keepdims=True)


def _moe_ffn_kernel(x_ref, wg_ref, wu_ref, wd_ref, gate_ref, y_ref):
    x = x_ref[0]
    hid = _silu(_bdot(x, wg_ref[0])) * _bdot(x, wu_ref[0])
    y_ref[0] = (_bdot(hid, wd_ref[0]) * gate_ref[0]).astype(y_ref.dtype)


COMBINE_TILE = 256
STRIP_ROWS = 64
STRIP_ALIGN = 16
STRIPS_PER_DOT = 4


def _moe_combine_ple_kernel(base_ref, off_ref, cnt_ref,
                            h_ref, p_ref, g_ref, wg_ref, wp_ref, fg_ref, ye_ref, tok_ref,
                            o_ref, ybuf, tbuf, sems, yextra, textra, extra_sems, acc_ref,
                            *, n_tiles, total_rows, final_norm):
    E, TB, R = N_EXPERTS, COMBINE_TILE, STRIP_ROWS
    j = pl.program_id(0)
    cur = j % 2

    def strip_copies(tile, e, b):
        start = pl.multiple_of(base_ref[e * n_tiles + tile], STRIP_ALIGN)
        rows = pl.ds(e * R, R)
        return (pltpu.make_async_copy(ye_ref.at[pl.ds(start, R), :], ybuf.at[b, rows, :], sems.at[0, b, e]),
                pltpu.make_async_copy(tok_ref.at[pl.ds(start, R), :], tbuf.at[b, rows, :], sems.at[1, b, e]))

    @pl.when(j == 0)
    def _():
        for e in range(E):
            for cp in strip_copies(0, e, 0):
                cp.start()

    @pl.when(j + 1 < n_tiles)
    def _():
        for e in range(E):
            for cp in strip_copies(j + 1, e, 1 - cur):
                cp.start()

    for e in range(E):
        for cp in strip_copies(j, e, cur):
            cp.wait()

    tile_tok0 = j * TB
    K = STRIPS_PER_DOT * R
    lane = lax.broadcasted_iota(jnp.int32, (K, TB), 1)
    srow = lax.broadcasted_iota(jnp.int32, (K, 1), 0)
    acc = jnp.zeros(acc_ref.shape, jnp.float32)
    for grp in range(E // STRIPS_PER_DOT):
        wanted = None
        for i in range(STRIPS_PER_DOT):
            e = grp * STRIPS_PER_DOT + i
            lo = off_ref[e * n_tiles + j] + i * R
            hi = jnp.minimum(lo + cnt_ref[e * n_tiles + j], (i + 1) * R)
            w = (srow >= lo) & (srow < hi)
            wanted = w if wanted is None else wanted | w
        rows = pl.ds(grp * K, K)
        tok = tbuf[cur, rows, 0:1] - tile_tok0
        hot_t = jnp.where(wanted & (lane == tok), 1.0, 0.0)
        acc = acc + _bdot(hot_t.T, ybuf[cur, rows, :])
    acc_ref[...] = acc

    lane_r = lax.broadcasted_iota(jnp.int32, (R, TB), 1)
    srow_r = lax.broadcasted_iota(jnp.int32, (R, 1), 0)
    for e in range(E):
        base = base_ref[e * n_tiles + j]
        first = base + off_ref[e * n_tiles + j]
        end = first + cnt_ref[e * n_tiles + j]
        n_sub = (end - base + R - 1) // R

        def body(k, carry, base=base, first=first, end=end):
            want0 = base + k * R
            start = pl.multiple_of(jnp.minimum(want0, total_rows - R), STRIP_ALIGN)
            cy = pltpu.make_async_copy(ye_ref.at[pl.ds(start, R), :], yextra, extra_sems.at[0])
            ct = pltpu.make_async_copy(tok_ref.at[pl.ds(start, R), :], textra, extra_sems.at[1])
            cy.start()
            ct.start()
            cy.wait()
            ct.wait()
            grow = srow_r + start
            wanted = (grow >= jnp.maximum(want0, first)) & (grow < jnp.minimum(want0 + R, end))
            hot_t = jnp.where(wanted & (lane_r == textra[:, 0:1] - tile_tok0), 1.0, 0.0)
            acc_ref[...] += _bdot(hot_t.T, yextra[...])
            return carry

        lax.fori_loop(1, n_sub, body, 0)

    hm = h_ref[...] + acc_ref[...]
    gate = _sigmoid(_bdot(_rms(hm, g_ref[...]), wg_ref[...]))
    out = hm + _bdot(p_ref[...], wp_ref[...]) * gate
    o_ref[...] = _rms(out, fg_ref[...]) if final_norm else out


def moe_ple_layer(h, p, norm_g, w_router, w_gate, w_up, w_down, ple_norm_g, ple_w_gate, ple_w_proj,
                  final_g, final_norm):
    T, D = h.shape
    E, F = N_EXPERTS, w_gate.shape[-1]
    cap = EC_CAPACITY_FACTOR * T // E
    bf = jnp.bfloat16
    row = lambda i: (i, 0)
    fixed = lambda i: (0, 0)
    xn, aff = pl.pallas_call(
        _moe_router_kernel,
        grid=(T // ROW_TILE,),
        in_specs=[pl.BlockSpec((ROW_TILE, D), row), pl.BlockSpec((1, D), fixed),
                  pl.BlockSpec((D, E), fixed)],
        out_specs=[pl.BlockSpec((ROW_TILE, D), row), pl.BlockSpec((ROW_TILE, E), row)],
        out_shape=[jax.ShapeDtypeStruct((T, D), bf), jax.ShapeDtypeStruct((T, E), jnp.float32)],
        name="moe_router",
    )(h, norm_g.reshape(1, D), w_router.astype(bf))
    aff_t = aff.T
    _, idx = lax.top_k(aff_t, cap)
    idx = jnp.sort(idx, axis=1)
    gate = jnp.take_along_axis(aff_t, idx, axis=1)
    xe = xn.at[idx].get(mode='promise_in_bounds')
    ye = pl.pallas_call(
        _moe_ffn_kernel,
        grid=(E, cap // ROW_TILE),
        in_specs=[pl.BlockSpec((1, ROW_TILE, D), lambda e, i: (e, i, 0)),
                  pl.BlockSpec((1, D, F), lambda e, i: (e, 0, 0)),
                  pl.BlockSpec((1, D, F), lambda e, i: (e, 0, 0)),
                  pl.BlockSpec((1, F, D), lambda e, i: (e, 0, 0)),
                  pl.BlockSpec((1, ROW_TILE, 1), lambda e, i: (e, i, 0))],
        out_specs=pl.BlockSpec((1, ROW_TILE, D), lambda e, i: (e, i, 0)),
        out_shape=jax.ShapeDtypeStruct((E, cap, D), bf),
        compiler_params=pltpu.CompilerParams(dimension_semantics=("parallel", "arbitrary")),
        name="moe_expert_ffn",
    )(xe, w_gate.astype(bf), w_up.astype(bf), w_down.astype(bf), gate[..., None])

    TB, R = COMBINE_TILE, STRIP_ROWS
    n_tiles = T // TB
    total_rows = E * cap
    edges = jnp.arange(n_tiles + 1, dtype=jnp.int32) * TB
    lo = jnp.sum((idx[:, :, None] < edges[None, None, :]).astype(jnp.int32), axis=1)
    first = lo[:, :-1] + (jnp.arange(E, dtype=jnp.int32) * cap)[:, None]
    cnt = lo[:, 1:] - lo[:, :-1]
    base = jnp.minimum(first // STRIP_ALIGN * STRIP_ALIGN, total_rows - R)
    off = first - base
    tok = jnp.broadcast_to(idx.reshape(total_rows, 1), (total_rows, LANES))

    P = p.shape[1]
    tile = lambda i, *_: (i, 0)
    const = lambda i, *_: (0, 0)
    return pl.pallas_call(
        functools.partial(_moe_combine_ple_kernel, n_tiles=n_tiles, total_rows=total_rows,
                          final_norm=final_norm),
        grid_spec=pltpu.PrefetchScalarGridSpec(
            num_scalar_prefetch=3,
            grid=(n_tiles,),
            in_specs=[pl.BlockSpec((TB, D), tile), pl.BlockSpec((TB, P), tile),
                      pl.BlockSpec((1, D), const), pl.BlockSpec((D, D), const), pl.BlockSpec((P, D), const),
                      pl.BlockSpec((1, D), const),
                      pl.BlockSpec(memory_space=pl.ANY), pl.BlockSpec(memory_space=pl.ANY)],
            out_specs=pl.BlockSpec((TB, D), tile),
            scratch_shapes=[pltpu.VMEM((2, E * R, D), bf), pltpu.VMEM((2, E * R, LANES), jnp.int32),
                            pltpu.SemaphoreType.DMA((2, 2, E)),
                            pltpu.VMEM((R, D), bf), pltpu.VMEM((R, LANES), jnp.int32),
                            pltpu.SemaphoreType.DMA((2,)),
                            pltpu.VMEM((TB, D), jnp.float32)]),
        out_shape=jax.ShapeDtypeStruct((T, D), jnp.float32),
        compiler_params=pltpu.CompilerParams(dimension_semantics=("arbitrary",)),
        name="moe_combine_ple",
    )(base.reshape(-1), off.reshape(-1), cnt.reshape(-1),
      h, p, ple_norm_g.reshape(1, D), ple_w_gate.astype(bf), ple_w_proj.astype(bf), final_g.reshape(1, D),
      ye.reshape(total_rows, D), tok)


def encoder_trunk(x, p, W):
    h = x
    for i in range(DEPTH):
        j = i // N_MIXERS
        if i % N_MIXERS == 0:
            h = mla_layer(h, W['norm_mix_g'][i], W['mla_w_in'][j], W['mla_q_norm_g'][j], W['mla_w_qb'][j],
                          W['mla_kv_norm_g'][j], W['mla_w_kvb'][j], W['mla_w_out'][j])
        else:
            h = gdn_layer(h, W['norm_mix_g'][i], W['gdn_w_in'][j], W['gdn_conv_w'][j], W['gdn_A_log'][j],
                          W['gdn_dt_bias'][j], W['gdn_norm_g'][j], W['gdn_w_out'][j])
        B, S, D = h.shape
        ht = moe_ple_layer(h.reshape(B * S, D), p[i].reshape(B * S, PLE_DIM), W['norm_ffn_g'][i],
                           W['moe_w_router'][i], W['moe_w_gate'][i], W['moe_w_up'][i], W['moe_w_down'][i],
                           W['norm_ple_g'][i], W['ple_w_gate'][i], W['ple_w_proj'][i],
                           W['final_norm_g'], final_norm=(i == DEPTH - 1))
        h = ht.reshape(B, S, D)
    return h


def kernel(x_prompt, x_sample, p_prompt, p_sample, norm_mix_g, norm_ffn_g, norm_ple_g, final_norm_g,
           mla_w_in, mla_q_norm_g, mla_w_qb, mla_kv_norm_g, mla_w_kvb, mla_w_out,
           gdn_w_in, gdn_conv_w, gdn_A_log, gdn_dt_bias, gdn_norm_g, gdn_w_out,
           moe_w_router, moe_w_gate, moe_w_up, moe_w_down, ple_w_proj, ple_w_gate):
    W = {
        'norm_mix_g': norm_mix_g, 'norm_ffn_g': norm_ffn_g, 'norm_ple_g': norm_ple_g,
        'final_norm_g': final_norm_g,
        'mla_w_in': mla_w_in, 'mla_q_norm_g': mla_q_norm_g, 'mla_w_qb': mla_w_qb,
        'mla_kv_norm_g': mla_kv_norm_g, 'mla_w_kvb': mla_w_kvb, 'mla_w_out': mla_w_out,
        'gdn_w_in': gdn_w_in, 'gdn_conv_w': gdn_conv_w, 'gdn_A_log': gdn_A_log,
        'gdn_dt_bias': gdn_dt_bias, 'gdn_norm_g': gdn_norm_g, 'gdn_w_out': gdn_w_out,
        'moe_w_router': moe_w_router, 'moe_w_gate': moe_w_gate, 'moe_w_up': moe_w_up,
        'moe_w_down': moe_w_down, 'ple_w_proj': ple_w_proj, 'ple_w_gate': ple_w_gate,
    }
    y_prompt = encoder_trunk(x_prompt, p_prompt, W)
    y_sample = encoder_trunk(x_sample, p_sample, W)
    return (y_prompt, y_sample)
```

```python
import functools

import jax
import jax.numpy as jnp
from jax import lax
from jax.experimental import pallas as pl
from jax.experimental.pallas import tpu as pltpu

D_MODEL = 1024
DEPTH = 4
PLE_DIM = 256
N_MIXERS = 2
MLA_HEADS = 8
MLA_NOPE_DIM = 128
MLA_ROPE_DIM = 64
MLA_V_DIM = 128
MLA_Q_LORA = 256
MLA_KV_LORA = 128
ROPE_THETA = 10000.0
Q_BLOCK = 128
GDN_HEADS = 8
GDN_DK = 128
GDN_DV = 128
GDN_CONV = 5
GDN_CHUNK = 64
GDN_QKV_DIM = 2 * GDN_HEADS * GDN_DK + GDN_HEADS * GDN_DV
N_EXPERTS = 16
EC_CAPACITY_FACTOR = 2
NORM_EPS = 1e-6


def rope_tables(seq):
    pos = jnp.arange(seq, dtype=jnp.float32)
    inv = ROPE_THETA ** (-jnp.arange(0, MLA_ROPE_DIM, 2, dtype=jnp.float32) / MLA_ROPE_DIM)
    ang = pos[:, None] * inv[None, :]
    return jnp.cos(ang), jnp.sin(ang)


LANES = 128
ROW_TILE = 512
ATTN_Q_TILE = 512
ATTN_HEADS_PER_STEP = 2
VMEM_LIMIT_BYTES = 48 * 1024 * 1024
MLA_QK_PAD = 2 * LANES


def _rms(x, g):
    return x * lax.rsqrt(jnp.mean(x * x, axis=-1, keepdims=True) + NORM_EPS) * g


def _mla_proj_kernel(h_ref, g_ref, w_in_ref, qg_ref, w_qb_ref, kvg_ref, w_k_ref, w_v_ref,
                     cos_ref, sin_ref, q_ref, k_ref, v_ref):
    H, NP = MLA_HEADS, MLA_NOPE_DIM
    scale = (MLA_NOPE_DIM + MLA_ROPE_DIM) ** -0.5
    hn = _rms(h_ref[...], g_ref[...])
    lat = _bdot(hn, w_in_ref[...])
    q_lat = lat[:, :MLA_Q_LORA]
    kv_lat = lat[:, MLA_Q_LORA:MLA_Q_LORA + MLA_KV_LORA]
    o = MLA_Q_LORA + MLA_KV_LORA
    cos, sin = cos_ref[...], sin_ref[...]
    k_rope = (lat[:, o:o + LANES] * cos + lat[:, o + LANES:o + 2 * LANES] * sin).astype(jnp.bfloat16)
    q = _bdot(_rms(q_lat, qg_ref[...]), w_qb_ref[...]) * scale
    kvn = _rms(kv_lat, kvg_ref[...])
    k_nope = _bdot(kvn, w_k_ref[...]).astype(jnp.bfloat16)
    v_ref[...] = _bdot(kvn, w_v_ref[...]).astype(jnp.bfloat16)
    for hd in range(H):
        a = q[:, (H + hd) * NP:(H + hd + 1) * NP]
        b = q[:, (2 * H + hd) * NP:(2 * H + hd + 1) * NP]
        q_ref[:, hd * MLA_QK_PAD:hd * MLA_QK_PAD + NP] = q[:, hd * NP:(hd + 1) * NP].astype(jnp.bfloat16)
        q_ref[:, hd * MLA_QK_PAD + NP:(hd + 1) * MLA_QK_PAD] = (a * cos + b * sin).astype(jnp.bfloat16)
        k_ref[:, hd * MLA_QK_PAD:hd * MLA_QK_PAD + NP] = k_nope[:, hd * NP:(hd + 1) * NP]
        k_ref[:, hd * MLA_QK_PAD + NP:(hd + 1) * MLA_QK_PAD] = k_rope


def _mla_attn_kernel(q_ref, k_ref, v_ref, o_ref):
    heads = range(ATTN_HEADS_PER_STEP)
    qk = lambda hd: slice(hd * MLA_QK_PAD, (hd + 1) * MLA_QK_PAD)
    vv = lambda hd: slice(hd * MLA_V_DIM, (hd + 1) * MLA_V_DIM)
    s = [_bdot_nt(q_ref[0, :, qk(hd)], k_ref[0, :, qk(hd)]) for hd in heads]
    p = [jnp.exp(x - jnp.max(x, axis=-1, keepdims=True)) for x in s]
    denom = [jnp.sum(x, axis=-1, keepdims=True) for x in p]
    for hd in heads:
        o_ref[0, :, vv(hd)] = (_bdot(p[hd], v_ref[0, :, vv(hd)]) / denom[hd]).astype(o_ref.dtype)


def _matmul_residual_kernel(x_ref, w_ref, h_ref, o_ref):
    o_ref[...] = h_ref[...] + _bdot(x_ref[...], w_ref[...])


def matmul_residual(x, w, h, name):
    T, K = x.shape
    D = w.shape[1]
    return pl.pallas_call(
        _matmul_residual_kernel,
        grid=(T // ROW_TILE,),
        in_specs=[pl.BlockSpec((ROW_TILE, K), lambda i: (i, 0)),
                  pl.BlockSpec((K, D), lambda i: (0, 0)),
                  pl.BlockSpec((ROW_TILE, D), lambda i: (i, 0))],
        out_specs=pl.BlockSpec((ROW_TILE, D), lambda i: (i, 0)),
        out_shape=jax.ShapeDtypeStruct((T, D), jnp.float32),
        name=name,
    )(x, w, h)


def _rope_swapped_columns(w):
    half = MLA_ROPE_DIM // 2
    pad = jnp.zeros(w.shape[:-1] + (LANES - MLA_ROPE_DIM,), w.dtype)
    plain = jnp.concatenate([w, pad], axis=-1)
    swapped = jnp.concatenate([w[..., half:], w[..., :half], pad], axis=-1)
    return plain, swapped


def mla_layer(h, norm_g, w_in, q_norm_g, w_qb, kv_norm_g, w_kvb, w_out):
    B, S, D = h.shape
    T = B * S
    H, NP, R = MLA_HEADS, MLA_NOPE_DIM, MLA_ROPE_DIM
    bf = jnp.bfloat16
    o = MLA_Q_LORA + MLA_KV_LORA
    kr_plain, kr_swapped = _rope_swapped_columns(w_in[:, o:])
    w_in_ext = jnp.concatenate([w_in[:, :o], kr_plain, kr_swapped], axis=1).astype(bf)
    w_qb_h = w_qb.reshape(MLA_Q_LORA, H, NP + R)
    qr_plain, qr_swapped = _rope_swapped_columns(w_qb_h[:, :, NP:])
    w_qb_ext = jnp.concatenate([w_qb_h[:, :, :NP].reshape(MLA_Q_LORA, H * NP),
                                qr_plain.reshape(MLA_Q_LORA, H * LANES),
                                qr_swapped.reshape(MLA_Q_LORA, H * LANES)], axis=1).astype(bf)
    w_kvb_h = w_kvb.reshape(MLA_KV_LORA, H, NP + MLA_V_DIM)
    w_k = w_kvb_h[:, :, :NP].reshape(MLA_KV_LORA, H * NP).astype(bf)
    w_v = w_kvb_h[:, :, NP:].reshape(MLA_KV_LORA, H * MLA_V_DIM).astype(bf)
    cos, sin = rope_tables(S)
    zeros = jnp.zeros((S, LANES - R), jnp.float32)
    cos_pad = jnp.concatenate([cos, cos, zeros], axis=1)
    sin_pad = jnp.concatenate([-sin, sin, zeros], axis=1)

    tiles_per_seq = S // ROW_TILE
    row = lambda i: (i, 0)
    fixed = lambda i: (0, 0)
    pos = lambda i: (i % tiles_per_seq, 0)
    q_cat, k_cat, v = pl.pallas_call(
        _mla_proj_kernel,
        grid=(T // ROW_TILE,),
        in_specs=[pl.BlockSpec((ROW_TILE, D), row),
                  pl.BlockSpec((1, D), fixed),
                  pl.BlockSpec(w_in_ext.shape, fixed),
                  pl.BlockSpec((1, MLA_Q_LORA), fixed),
                  pl.BlockSpec(w_qb_ext.shape, fixed),
                  pl.BlockSpec((1, MLA_KV_LORA), fixed),
                  pl.BlockSpec(w_k.shape, fixed),
                  pl.BlockSpec(w_v.shape, fixed),
                  pl.BlockSpec((ROW_TILE, LANES), pos),
                  pl.BlockSpec((ROW_TILE, LANES), pos)],
        out_specs=[pl.BlockSpec((ROW_TILE, H * MLA_QK_PAD), row),
                   pl.BlockSpec((ROW_TILE, H * MLA_QK_PAD), row),
                   pl.BlockSpec((ROW_TILE, H * MLA_V_DIM), row)],
        out_shape=[jax.ShapeDtypeStruct((T, H * MLA_QK_PAD), bf),
                   jax.ShapeDtypeStruct((T, H * MLA_QK_PAD), bf),
                   jax.ShapeDtypeStruct((T, H * MLA_V_DIM), bf)],
        name="mla_proj",
    )(h.reshape(T, D), norm_g.reshape(1, D), w_in_ext, q_norm_g.reshape(1, -1), w_qb_ext,
      kv_norm_g.reshape(1, -1), w_k, w_v, cos_pad, sin_pad)

    G = ATTN_HEADS_PER_STEP
    o_attn = pl.pallas_call(
        _mla_attn_kernel,
        grid=(B, H // G, S // ATTN_Q_TILE),
        in_specs=[pl.BlockSpec((1, ATTN_Q_TILE, G * MLA_QK_PAD), lambda b, hd, i: (b, i, hd)),
                  pl.BlockSpec((1, S, G * MLA_QK_PAD), lambda b, hd, i: (b, 0, hd)),
                  pl.BlockSpec((1, S, G * MLA_V_DIM), lambda b, hd, i: (b, 0, hd))],
        out_specs=pl.BlockSpec((1, ATTN_Q_TILE, G * MLA_V_DIM), lambda b, hd, i: (b, i, hd)),
        out_shape=jax.ShapeDtypeStruct((B, S, H * MLA_V_DIM), bf),
        compiler_params=pltpu.CompilerParams(
            dimension_semantics=("parallel", "parallel", "arbitrary"),
            vmem_limit_bytes=VMEM_LIMIT_BYTES),
        name="mla_attention",
    )(q_cat.reshape(B, S, -1), k_cat.reshape(B, S, -1), v.reshape(B, S, -1))

    h_new = matmul_residual(o_attn.reshape(T, H * MLA_V_DIM), w_out.astype(bf), h.reshape(T, D),
                            name="mla_out_proj")
    return h_new.reshape(B, S, D)


def _bdot(a, b):
    return jnp.dot(a.astype(jnp.bfloat16), b.astype(jnp.bfloat16), preferred_element_type=jnp.float32)


def _bdot_nt(a, b):
    return lax.dot_general(a.astype(jnp.bfloat16), b.astype(jnp.bfloat16),
                           (((1,), (1,)), ((), ())), preferred_element_type=jnp.float32)


def _unit_triangular_inverses(Ls, row, col):
    C = Ls[0].shape[0]
    eye = (row == col).astype(jnp.float32)

    def same_block(s):
        return (row // s) == (col // s)

    blk8 = same_block(8)
    L8 = [jnp.where(blk8, L, 0.0) for L in Ls]
    P2 = [_bdot(a, a) for a in L8]
    P4 = [_bdot(a, a) for a in P2]
    X = [eye - a for a in L8]
    X = [x + _bdot(x, p) for x, p in zip(X, P2)]
    X = [x + _bdot(x, p) for x, p in zip(X, P4)]
    s = 8
    while s < C:
        pair = same_block(2 * s) & jnp.logical_not(same_block(s))
        XC = [_bdot(x, jnp.where(pair, L, 0.0)) for x, L in zip(X, Ls)]
        X = [x - _bdot(xc, x) for x, xc in zip(X, XC)]
        s *= 2
    return X


def _gdn_chunk_kernel(qf_ref, kf_ref, vf_ref, gcf_ref, bcf_ref, grf_ref,
                      qb_ref, kb_ref, vb_ref, gcb_ref, bcb_ref, grb_ref,
                      of_ref, ob_ref, state_ref):
    H, C, DK, DV = GDN_HEADS, GDN_CHUNK, GDN_DK, GDN_DV

    @pl.when(pl.program_id(1) == 0)
    def _():
        state_ref[...] = jnp.zeros_like(state_ref)

    row = lax.broadcasted_iota(jnp.int32, (C, C), 0)
    col = lax.broadcasted_iota(jnp.int32, (C, C), 1)
    dirs = ((0, qf_ref, kf_ref, vf_ref, gcf_ref, bcf_ref, grf_ref, of_ref),
            (1, qb_ref, kb_ref, vb_ref, gcb_ref, bcb_ref, grb_ref, ob_ref))
    chains = []
    for d, q_ref, k_ref, v_ref, gc_ref, bc_ref, gr_ref, o_ref in dirs:
        if d == 0:
            incl, strict, last = row >= col, row > col, C - 1
        else:
            incl, strict, last = row <= col, row < col, 0
        tri = incl.astype(jnp.float32)
        gc_cols = jnp.dot(tri, gc_ref[0], precision=lax.Precision.HIGHEST,
                          preferred_element_type=jnp.float32)
        gc_rows = jnp.dot(gr_ref[0, 0], tri.T, precision=lax.Precision.HIGHEST,
                          preferred_element_type=jnp.float32)
        beta_cols = bc_ref[0]
        for h in range(H):
            c = d * H + h
            chains.append(dict(
                c=c, incl=incl, strict=strict, o_ref=o_ref, sl=slice(h * DK, (h + 1) * DK),
                q_ref=q_ref, k_ref=k_ref, v_ref=v_ref,
                beta=beta_cols[:, c:c + 1], gcc=gc_cols[:, c:c + 1], gcr=gc_rows[c:c + 1, :],
                gl=gc_cols[last:last + 1, c:c + 1]))

    for ch in chains:
        ch['decay'] = jnp.where(ch['incl'], jnp.exp(jnp.where(ch['incl'], ch['gcc'] - ch['gcr'], 0.0)), 0.0)
        ch['eg'] = jnp.exp(ch['gcc'])
    for ch in chains:
        k = ch['k_ref'][0, :, ch['sl']]
        q = ch['q_ref'][0, :, ch['sl']]
        kb = k * ch['beta']
        ch['kbeg'] = kb * ch['eg']
        ch['s1'] = _bdot_nt(jnp.concatenate([kb, q], axis=0), k)
    Ls = [jnp.where(ch['strict'], ch['s1'][:C] * ch['decay'], 0.0) for ch in chains]
    Ts = _unit_triangular_inverses(Ls, row, col)
    for ch, T in zip(chains, Ts):
        v = ch['v_ref'][0, :, ch['sl']]
        ch['uw'] = _bdot(T, jnp.concatenate([v * ch['beta'], ch['kbeg']], axis=1))
    for ch in chains:
        q = ch['q_ref'][0, :, ch['sl']]
        w = ch['uw'][:, DV:]
        ch['ws'] = _bdot(jnp.concatenate([w, q * ch['eg']], axis=0), state_ref[ch['c']])
    for ch in chains:
        ch['v_new'] = ch['uw'][:, :DV] - ch['ws'][:C]
        a_intra = ch['s1'][C:] * ch['decay']
        ch['o_ref'][0, :, ch['sl']] = ch['ws'][C:] + _bdot(a_intra, ch['v_new'])
    for ch in chains:
        k = ch['k_ref'][0, :, ch['sl']]
        k_g = k * jnp.exp(ch['gl'] - ch['gcc'])
        c = ch['c']
        state_ref[c] = state_ref[c] * jnp.exp(ch['gl']) + _bdot(k_g.T, ch['v_new'])


def gdn_bidirectional_delta(q, k, v, beta, g):
    B, S, HD = q.shape
    C, H = GDN_CHUNK, GDN_HEADS
    N = S // C
    g_rows = g.reshape(B, N, C, 2 * H).transpose(0, 1, 3, 2)

    def fwd3(b, n):
        return (b, n, 0)

    def bwd3(b, n):
        return (b, N - 1 - n, 0)

    def fwd4(b, n):
        return (b, n, 0, 0)

    def bwd4(b, n):
        return (b, N - 1 - n, 0, 0)

    def specs(i3, i4):
        big = pl.BlockSpec((1, C, HD), i3)
        small = pl.BlockSpec((1, C, 2 * H), i3)
        return [big, big, big, small, small, pl.BlockSpec((1, 1, 2 * H, C), i4)]

    out = jax.ShapeDtypeStruct((B, S, HD), jnp.float32)
    return pl.pallas_call(
        _gdn_chunk_kernel,
        grid=(B, N),
        in_specs=specs(fwd3, fwd4) + specs(bwd3, bwd4),
        out_specs=[pl.BlockSpec((1, C, HD), fwd3), pl.BlockSpec((1, C, HD), bwd3)],
        out_shape=[out, out],
        scratch_shapes=[pltpu.VMEM((2 * H, GDN_DK, GDN_DV), jnp.float32)],
        compiler_params=pltpu.CompilerParams(dimension_semantics=("parallel", "arbitrary")),
        name="gdn_chunk_delta",
    )(q, k, v, g, beta, g_rows, q, k, v, g, beta, g_rows)


GDN_ROW_TILE = 256
HALO = 8


def _gdn_in_kernel(h_ref, hp_ref, hn_ref, g_ref, wqkv_ref, wz_ref, wba_ref, cw_ref, alog_ref, dtb_ref,
                   q_ref, k_ref, v_ref, z_ref, beta_ref, gl_ref, pre_ref, *, tiles_per_seq):
    H, DK = GDN_HEADS, GDN_DK
    TM = h_ref.shape[0]
    i = pl.program_id(0) % tiles_per_seq
    keep_prev = jnp.where(i == 0, 0.0, 1.0)
    keep_next = jnp.where(i == tiles_per_seq - 1, 0.0, 1.0)
    hx = jnp.concatenate([hp_ref[...] * keep_prev, h_ref[...], hn_ref[...] * keep_next], axis=0)
    hn = _rms(hx, g_ref[...]).astype(jnp.bfloat16)
    pre = _bdot(hn, wqkv_ref[...])
    pre_ref[...] = pre
    pad = GDN_CONV // 2
    acc = None
    for j in range(GDN_CONV):
        term = pre_ref[pl.ds(HALO + j - pad, TM), :] * cw_ref[j:j + 1, :]
        acc = term if acc is None else acc + term
    qkv = _silu(acc)
    for hd in range(H):
        for off, ref, scale in ((0, q_ref, DK ** -0.5), (H * DK, k_ref, 1.0)):
            x = qkv[:, off + hd * DK:off + (hd + 1) * DK]
            ref[:, hd * DK:(hd + 1) * DK] = x * (lax.rsqrt(jnp.sum(x * x, axis=-1, keepdims=True) + NORM_EPS) * scale)
    v_ref[...] = qkv[:, 2 * H * DK:]
    hc = hn[HALO:HALO + TM]
    z_ref[...] = _bdot(hc, wz_ref[...])
    ba = _bdot(hc, wba_ref[...])
    beta_ref[...] = _sigmoid(ba[:, :2 * H])
    a = ba[:, 2 * H:] + dtb_ref[...]
    softplus = jnp.maximum(a, 0.0) + jnp.log(1.0 + jnp.exp(-jnp.abs(a)))
    gl_ref[...] = -jnp.exp(alog_ref[...]) * softplus


def _gdn_out_kernel(of_ref, ob_ref, z_ref, g_ref, w_ref, h_ref, o_ref):
    H, DV = GDN_HEADS, GDN_DV
    o = of_ref[...] + ob_ref[...]
    g = g_ref[...]
    parts = []
    for hd in range(H):
        x = o[:, hd * DV:(hd + 1) * DV]
        parts.append(_rms(x, g))
    y = jnp.concatenate(parts, axis=1) * _silu(z_ref[...])
    o_ref[...] = h_ref[...] + _bdot(y, w_ref[...])


def gdn_layer(h, norm_g, w_in, conv_w, A_log, dt_bias, out_norm_g, w_out):
    B, S, D = h.shape
    T = B * S
    H = GDN_HEADS
    HD = H * GDN_DK
    bf = jnp.bfloat16
    TM = GDN_ROW_TILE
    tiles_per_seq = S // TM
    hb = TM // HALO
    n_halo = T // HALO
    o = GDN_QKV_DIM + H * GDN_DV
    w_qkv = w_in[:, :GDN_QKV_DIM].astype(bf)
    w_z = w_in[:, GDN_QKV_DIM:o].astype(bf)
    w_ba = w_in[:, o:].astype(bf)
    cw = jnp.concatenate([conv_w, jnp.zeros((HALO - GDN_CONV, GDN_QKV_DIM), conv_w.dtype)], axis=0)
    row = lambda i: (i, 0)
    fixed = lambda i: (0, 0)
    f32 = jnp.float32
    ht = h.reshape(T, D)
    q, k, v, z, beta, gl = pl.pallas_call(
        functools.partial(_gdn_in_kernel, tiles_per_seq=tiles_per_seq),
        grid=(T // TM,),
        in_specs=[pl.BlockSpec((TM, D), row),
                  pl.BlockSpec((HALO, D), lambda i: (jnp.maximum(i * hb - 1, 0), 0)),
                  pl.BlockSpec((HALO, D), lambda i: (jnp.minimum((i + 1) * hb, n_halo - 1), 0)),
                  pl.BlockSpec((1, D), fixed),
                  pl.BlockSpec(w_qkv.shape, fixed), pl.BlockSpec(w_z.shape, fixed),
                  pl.BlockSpec(w_ba.shape, fixed), pl.BlockSpec(cw.shape, fixed),
                  pl.BlockSpec((1, 2 * H), fixed), pl.BlockSpec((1, 2 * H), fixed)],
        out_specs=[pl.BlockSpec((TM, HD), row), pl.BlockSpec((TM, HD), row), pl.BlockSpec((TM, HD), row),
                   pl.BlockSpec((TM, HD), row), pl.BlockSpec((TM, 2 * H), row), pl.BlockSpec((TM, 2 * H), row)],
        out_shape=[jax.ShapeDtypeStruct((T, HD), f32)] * 4 + [jax.ShapeDtypeStruct((T, 2 * H), f32)] * 2,
        scratch_shapes=[pltpu.VMEM((TM + 2 * HALO, GDN_QKV_DIM), f32)],
        compiler_params=pltpu.CompilerParams(vmem_limit_bytes=VMEM_LIMIT_BYTES),
        name="gdn_in_proj",
    )(ht, ht, ht, norm_g.reshape(1, D), w_qkv, w_z, w_ba, cw,
      A_log.reshape(1, 2 * H).astype(f32), dt_bias.reshape(1, 2 * H).astype(f32))

    o_fwd, o_bwd = gdn_bidirectional_delta(q.reshape(B, S, HD), k.reshape(B, S, HD), v.reshape(B, S, HD),
                                           beta.reshape(B, S, 2 * H), gl.reshape(B, S, 2 * H))

    h_new = pl.pallas_call(
        _gdn_out_kernel,
        grid=(T // ROW_TILE,),
        in_specs=[pl.BlockSpec((ROW_TILE, HD), row)] * 3 +
                 [pl.BlockSpec((1, GDN_DV), fixed), pl.BlockSpec((HD, D), fixed), pl.BlockSpec((ROW_TILE, D), row)],
        out_specs=pl.BlockSpec((ROW_TILE, D), row),
        out_shape=jax.ShapeDtypeStruct((T, D), f32),
        name="gdn_out_proj",
    )(o_fwd.reshape(T, HD), o_bwd.reshape(T, HD), z, out_norm_g.reshape(1, GDN_DV), w_out.astype(bf), ht)
    return h_new.reshape(B, S, D)


def _sigmoid(x):
    return 0.5 * jnp.tanh(0.5 * x) + 0.5


def _silu(x):
    return x * _sigmoid(x)


def _moe_router_kernel(h_ref, g_ref, w_ref, xn_ref, aff_ref):
    xn = _rms(h_ref[...], g_ref[...])
    xn_ref[...] = xn.astype(xn_ref.dtype)
    logits = _bdot(xn, w_ref[...])
    e = jnp.exp(logits - jnp.max(logits, axis=-1, keepdims=True))
    aff_ref[...] = e / jnp.sum(e, axis=-1, keepdims=True)


def _moe_ffn_kernel(x_ref, wg_ref, wu_ref, wd_ref, gate_ref, y_ref):
    x = x_ref[0]
    hid = _silu(_bdot(x, wg_ref[0])) * _bdot(x, wu_ref[0])
    y_ref[0] = (_bdot(hid, wd_ref[0]) * gate_ref[0]).astype(y_ref.dtype)


COMBINE_TILE = 256
STRIP_ROWS = 64
STRIP_ALIGN = 16
STRIPS_PER_DOT = 4


def _moe_combine_ple_kernel(base_ref, off_ref, cnt_ref,
                            h_ref, p_ref, g_ref, wg_ref, wp_ref, fg_ref, ye_ref, tok_ref,
                            o_ref, ybuf, tbuf, sems, yextra, textra, extra_sems, acc_ref,
                            *, n_tiles, total_rows, final_norm):
    E, TB, R = N_EXPERTS, COMBINE_TILE, STRIP_ROWS
    j = pl.program_id(0)
    cur = j % 2

    def strip_copies(tile, e, b):
        start = pl.multiple_of(base_ref[e * n_tiles + tile], STRIP_ALIGN)
        rows = pl.ds(e * R, R)
        return (pltpu.make_async_copy(ye_ref.at[pl.ds(start, R), :], ybuf.at[b, rows, :], sems.at[0, b, e]),
                pltpu.make_async_copy(tok_ref.at[pl.ds(start, R), :], tbuf.at[b, rows, :], sems.at[1, b, e]))

    @pl.when(j == 0)
    def _():
        for e in range(E):
            for cp in strip_copies(0, e, 0):
                cp.start()

    @pl.when(j + 1 < n_tiles)
    def _():
        for e in range(E):
            for cp in strip_copies(j + 1, e, 1 - cur):
                cp.start()

    for e in range(E):
        for cp in strip_copies(j, e, cur):
            cp.wait()

    tile_tok0 = j * TB
    K = STRIPS_PER_DOT * R
    lane = lax.broadcasted_iota(jnp.int32, (K, TB), 1)
    srow = lax.broadcasted_iota(jnp.int32, (K, 1), 0)
    acc = jnp.zeros(acc_ref.shape, jnp.float32)
    for grp in range(E // STRIPS_PER_DOT):
        wanted = None
        for i in range(STRIPS_PER_DOT):
            e = grp * STRIPS_PER_DOT + i
            lo = off_ref[e * n_tiles + j] + i * R
            hi = jnp.minimum(lo + cnt_ref[e * n_tiles + j], (i + 1) * R)
            w = (srow >= lo) & (srow < hi)
            wanted = w if wanted is None else wanted | w
        rows = pl.ds(grp * K, K)
        tok = tbuf[cur, rows, 0:1] - tile_tok0
        hot_t = jnp.where(wanted & (lane == tok), 1.0, 0.0)
        acc = acc + _bdot(hot_t.T, ybuf[cur, rows, :])
    acc_ref[...] = acc

    lane_r = lax.broadcasted_iota(jnp.int32, (R, TB), 1)
    srow_r = lax.broadcasted_iota(jnp.int32, (R, 1), 0)
    for e in range(E):
        base = base_ref[e * n_tiles + j]
        first = base + off_ref[e * n_tiles + j]
        end = first + cnt_ref[e * n_tiles + j]
        n_sub = (end - base + R - 1) // R

        def body(k, carry, base=base, first=first, end=end):
            want0 = base + k * R
            start = pl.multiple_of(jnp.minimum(want0, total_rows - R), STRIP_ALIGN)
            cy = pltpu.make_async_copy(ye_ref.at[pl.ds(start, R), :], yextra, extra_sems.at[0])
            ct = pltpu.make_async_copy(tok_ref.at[pl.ds(start, R), :], textra, extra_sems.at[1])
            cy.start()
            ct.start()
            cy.wait()
            ct.wait()
            grow = srow_r + start
            wanted = (grow >= jnp.maximum(want0, first)) & (grow < jnp.minimum(want0 + R, end))
            hot_t = jnp.where(wanted & (lane_r == textra[:, 0:1] - tile_tok0), 1.0, 0.0)
            acc_ref[...] += _bdot(hot_t.T, yextra[...])
            return carry

        lax.fori_loop(1, n_sub, body, 0)

    hm = h_ref[...] + acc_ref[...]
    gate = _sigmoid(_bdot(_rms(hm, g_ref[...]), wg_ref[...]))
    out = hm + _bdot(p_ref[...], wp_ref[...]) * gate
    o_ref[...] = _rms(out, fg_ref[...]) if final_norm else out


F32_INF_BITS = 0x7F800000


def _topk_select_kernel(aff_ref, wcum_ref, *, cap):
    a = aff_ref[0]
    R = a.shape[0]
    bits = lax.bitcast_convert_type(a, jnp.int32)

    def count(mask):
        m = mask.astype(jnp.float32)
        return jnp.sum(jnp.sum(m, axis=0, keepdims=True), axis=1, keepdims=True)

    def bisect(_, carry):
        lo, hi = carry
        mid = lo + jnp.right_shift(hi - lo + 1, 1)
        ok = count(bits >= mid) >= cap
        return jnp.where(ok, mid, lo), jnp.where(ok, hi, mid - 1)

    thr, _ = lax.fori_loop(0, 31, bisect, (jnp.zeros((1, 1), jnp.int32),
                                           jnp.full((1, 1), F32_INF_BITS, jnp.int32)))
    gt = bits > thr
    eq = bits == thr
    need = cap - count(gt)

    lane_r = lax.broadcasted_iota(jnp.int32, (LANES, LANES), 0)
    lane_c = lax.broadcasted_iota(jnp.int32, (LANES, LANES), 1)
    upper = (lane_r <= lane_c).astype(jnp.bfloat16)
    row_r = lax.broadcasted_iota(jnp.int32, (R, R), 0)
    row_c = lax.broadcasted_iota(jnp.int32, (R, R), 1)
    earlier_rows = (row_r > row_c).astype(jnp.bfloat16)

    eq_in_row = jnp.dot(eq.astype(jnp.bfloat16), upper, preferred_element_type=jnp.float32)
    eq_row_total = jnp.broadcast_to(eq_in_row[:, LANES - 1:LANES], (R, LANES))
    eq_before = jnp.dot(earlier_rows, eq_row_total.astype(jnp.bfloat16), preferred_element_type=jnp.float32)
    sel = gt | (eq & (eq_before + eq_in_row <= need))
    wcum = jnp.dot(sel.astype(jnp.bfloat16), upper, preferred_element_type=jnp.float32)
    wcum_ref[0] = wcum.astype(jnp.int32)


def expert_choice_indices(aff_t, cap):
    E, T = aff_t.shape
    R = T // LANES
    wcum = pl.pallas_call(
        functools.partial(_topk_select_kernel, cap=cap),
        grid=(E,),
        in_specs=[pl.BlockSpec((1, R, LANES), lambda e: (e, 0, 0))],
        out_specs=pl.BlockSpec((1, R, LANES), lambda e: (e, 0, 0)),
        out_shape=jax.ShapeDtypeStruct((E, R, LANES), jnp.int32),
        name="moe_topk_select",
    )(aff_t.reshape(E, R, LANES))
    row_end = jnp.cumsum(wcum[:, :, LANES - 1], axis=1)
    row_start = row_end - wcum[:, :, LANES - 1]
    slot = jnp.arange(cap, dtype=jnp.int32)
    row = jnp.sum((row_end[:, None, :] <= slot[None, :, None]).astype(jnp.int32), axis=2)
    local = slot[None, :] - jnp.take_along_axis(row_start, row, axis=1)
    wrow = jnp.take_along_axis(wcum, row[:, :, None], axis=1)
    lane = jnp.sum((wrow <= local[:, :, None]).astype(jnp.int32), axis=2)
    return row * LANES + lane, row_start


def moe_ple_layer(h, p, norm_g, w_router, w_gate, w_up, w_down, ple_norm_g, ple_w_gate, ple_w_proj,
                  final_g, final_norm):
    T, D = h.shape
    E, F = N_EXPERTS, w_gate.shape[-1]
    cap = EC_CAPACITY_FACTOR * T // E
    bf = jnp.bfloat16
    row = lambda i: (i, 0)
    fixed = lambda i: (0, 0)
    xn, aff = pl.pallas_call(
        _moe_router_kernel,
        grid=(T // ROW_TILE,),
        in_specs=[pl.BlockSpec((ROW_TILE, D), row), pl.BlockSpec((1, D), fixed),
                  pl.BlockSpec((D, E), fixed)],
        out_specs=[pl.BlockSpec((ROW_TILE, D), row), pl.BlockSpec((ROW_TILE, E), row)],
        out_shape=[jax.ShapeDtypeStruct((T, D), bf), jax.ShapeDtypeStruct((T, E), jnp.float32)],
        name="moe_router",
    )(h, norm_g.reshape(1, D), w_router.astype(bf))
    aff_t = aff.T
    idx, row_start = expert_choice_indices(aff_t, cap)
    gate = jnp.take_along_axis(aff_t, idx, axis=1)
    xe = xn.at[idx].get(mode='promise_in_bounds')
    ye = pl.pallas_call(
        _moe_ffn_kernel,
        grid=(E, cap // ROW_TILE),
        in_specs=[pl.BlockSpec((1, ROW_TILE, D), lambda e, i: (e, i, 0)),
                  pl.BlockSpec((1, D, F), lambda e, i: (e, 0, 0)),
                  pl.BlockSpec((1, D, F), lambda e, i: (e, 0, 0)),
                  pl.BlockSpec((1, F, D), lambda e, i: (e, 0, 0)),
                  pl.BlockSpec((1, ROW_TILE, 1), lambda e, i: (e, i, 0))],
        out_specs=pl.BlockSpec((1, ROW_TILE, D), lambda e, i: (e, i, 0)),
        out_shape=jax.ShapeDtypeStruct((E, cap, D), bf),
        compiler_params=pltpu.CompilerParams(dimension_semantics=("parallel", "arbitrary")),
        name="moe_expert_ffn",
    )(xe, w_gate.astype(bf), w_up.astype(bf), w_down.astype(bf), gate[..., None])

    TB, R = COMBINE_TILE, STRIP_ROWS
    n_tiles = T // TB
    total_rows = E * cap
    lo = jnp.concatenate([row_start[:, ::TB // LANES], jnp.full((E, 1), cap, jnp.int32)], axis=1)
    first = lo[:, :-1] + (jnp.arange(E, dtype=jnp.int32) * cap)[:, None]
    cnt = lo[:, 1:] - lo[:, :-1]
    base = jnp.minimum(first // STRIP_ALIGN * STRIP_ALIGN, total_rows - R)
    off = first - base
    tok = jnp.broadcast_to(idx.reshape(total_rows, 1), (total_rows, LANES))

    P = p.shape[1]
    tile = lambda i, *_: (i, 0)
    const = lambda i, *_: (0, 0)
    return pl.pallas_call(
        functools.partial(_moe_combine_ple_kernel, n_tiles=n_tiles, total_rows=total_rows,
                          final_norm=final_norm),
        grid_spec=pltpu.PrefetchScalarGridSpec(
            num_scalar_prefetch=3,
            grid=(n_tiles,),
            in_specs=[pl.BlockSpec((TB, D), tile), pl.BlockSpec((TB, P), tile),
                      pl.BlockSpec((1, D), const), pl.BlockSpec((D, D), const), pl.BlockSpec((P, D), const),
                      pl.BlockSpec((1, D), const),
                      pl.BlockSpec(memory_space=pl.ANY), pl.BlockSpec(memory_space=pl.ANY)],
            out_specs=pl.BlockSpec((TB, D), tile),
            scratch_shapes=[pltpu.VMEM((2, E * R, D), bf), pltpu.VMEM((2, E * R, LANES), jnp.int32),
                            pltpu.SemaphoreType.DMA((2, 2, E)),
                            pltpu.VMEM((R, D), bf), pltpu.VMEM((R, LANES), jnp.int32),
                            pltpu.SemaphoreType.DMA((2,)),
                            pltpu.VMEM((TB, D), jnp.float32)]),
        out_shape=jax.ShapeDtypeStruct((T, D), jnp.float32),
        compiler_params=pltpu.CompilerParams(dimension_semantics=("arbitrary",)),
        name="moe_combine_ple",
    )(base.reshape(-1), off.reshape(-1), cnt.reshape(-1),
      h, p, ple_norm_g.reshape(1, D), ple_w_gate.astype(bf), ple_w_proj.astype(bf), final_g.reshape(1, D),
      ye.reshape(total_rows, D), tok)


def encoder_trunk(x, p, W):
    h = x
    for i in range(DEPTH):
        j = i // N_MIXERS
        if i % N_MIXERS == 0:
            h = mla_layer(h, W['norm_mix_g'][i], W['mla_w_in'][j], W['mla_q_norm_g'][j], W['mla_w_qb'][j],
                          W['mla_kv_norm_g'][j], W['mla_w_kvb'][j], W['mla_w_out'][j])
        else:
            h = gdn_layer(h, W['norm_mix_g'][i], W['gdn_w_in'][j], W['gdn_conv_w'][j], W['gdn_A_log'][j],
                          W['gdn_dt_bias'][j], W['gdn_norm_g'][j], W['gdn_w_out'][j])
        B, S, D = h.shape
        ht = moe_ple_layer(h.reshape(B * S, D), p[i].reshape(B * S, PLE_DIM), W['norm_ffn_g'][i],
                           W['moe_w_router'][i], W['moe_w_gate'][i], W['moe_w_up'][i], W['moe_w_down'][i],
                           W['norm_ple_g'][i], W['ple_w_gate'][i], W['ple_w_proj'][i],
                           W['final_norm_g'], final_norm=(i == DEPTH - 1))
        h = ht.reshape(B, S, D)
    return h


def kernel(x_prompt, x_sample, p_prompt, p_sample, norm_mix_g, norm_ffn_g, norm_ple_g, final_norm_g,
           mla_w_in, mla_q_norm_g, mla_w_qb, mla_kv_norm_g, mla_w_kvb, mla_w_out,
           gdn_w_in, gdn_conv_w, gdn_A_log, gdn_dt_bias, gdn_norm_g, gdn_w_out,
           moe_w_router, moe_w_gate, moe_w_up, moe_w_down, ple_w_proj, ple_w_gate):
    W = {
        'norm_mix_g': norm_mix_g, 'norm_ffn_g': norm_ffn_g, 'norm_ple_g': norm_ple_g,
        'final_norm_g': final_norm_g,
        'mla_w_in': mla_w_in, 'mla_q_norm_g': mla_q_norm_g, 'mla_w_qb': mla_w_qb,
        'mla_kv_norm_g': mla_kv_norm_g, 'mla_w_kvb': mla_w_kvb, 'mla_w_out': mla_w_out,
        'gdn_w_in': gdn_w_in, 'gdn_conv_w': gdn_conv_w, 'gdn_A_log': gdn_A_log,
        'gdn_dt_bias': gdn_dt_bias, 'gdn_norm_g': gdn_norm_g, 'gdn_w_out': gdn_w_out,
        'moe_w_router': moe_w_router, 'moe_w_gate': moe_w_gate, 'moe_w_up': moe_w_up,
        'moe_w_down': moe_w_down, 'ple_w_proj': ple_w_proj, 'ple_w_gate': ple_w_gate,
    }
    y_prompt = encoder_trunk(x_prompt, p_prompt, W)
    y_sample = encoder_trunk(x_sample, p_sample, W)
    return (y_prompt, y_sample)
```

```python
import functools

import jax
import jax.numpy as jnp
from jax import lax
from jax.experimental import pallas as pl
from jax.experimental.pallas import tpu as pltpu

D_MODEL = 1024
DEPTH = 4
PLE_DIM = 256
N_MIXERS = 2
MLA_HEADS = 8
MLA_NOPE_DIM = 128
MLA_ROPE_DIM = 64
MLA_V_DIM = 128
MLA_Q_LORA = 256
MLA_KV_LORA = 128
ROPE_THETA = 10000.0
Q_BLOCK = 128
GDN_HEADS = 8
GDN_DK = 128
GDN_DV = 128
GDN_CONV = 5
GDN_CHUNK = 64
GDN_QKV_DIM = 2 * GDN_HEADS * GDN_DK + GDN_HEADS * GDN_DV
N_EXPERTS = 16
EC_CAPACITY_FACTOR = 2
NORM_EPS = 1e-6


def rope_tables(seq):
    pos = jnp.arange(seq, dtype=jnp.float32)
    inv = ROPE_THETA ** (-jnp.arange(0, MLA_ROPE_DIM, 2, dtype=jnp.float32) / MLA_ROPE_DIM)
    ang = pos[:, None] * inv[None, :]
    return jnp.cos(ang), jnp.sin(ang)


LANES = 128
ROW_TILE = 512
ATTN_Q_TILE = 512
ATTN_HEADS_PER_STEP = 4
VMEM_LIMIT_BYTES = 56 * 1024 * 1024
MLA_QK_PAD = 2 * LANES


def _rms(x, g):
    return x * lax.rsqrt(jnp.mean(x * x, axis=-1, keepdims=True) + NORM_EPS) * g


def _mla_proj_kernel(h_ref, g_ref, w_in_ref, qg_ref, w_qb_ref, kvg_ref, w_k_ref, w_v_ref,
                     cos_ref, sin_ref, q_ref, k_ref, v_ref):
    H, NP = MLA_HEADS, MLA_NOPE_DIM
    scale = (MLA_NOPE_DIM + MLA_ROPE_DIM) ** -0.5
    hn = _rms(h_ref[...], g_ref[...])
    lat = _bdot(hn, w_in_ref[...])
    q_lat = lat[:, :MLA_Q_LORA]
    kv_lat = lat[:, MLA_Q_LORA:MLA_Q_LORA + MLA_KV_LORA]
    o = MLA_Q_LORA + MLA_KV_LORA
    cos, sin = cos_ref[...], sin_ref[...]
    k_rope = (lat[:, o:o + LANES] * cos + lat[:, o + LANES:o + 2 * LANES] * sin).astype(jnp.bfloat16)
    q = _bdot(_rms(q_lat, qg_ref[...]), w_qb_ref[...]) * scale
    kvn = _rms(kv_lat, kvg_ref[...])
    k_nope = _bdot(kvn, w_k_ref[...]).astype(jnp.bfloat16)
    v_ref[...] = _bdot(kvn, w_v_ref[...]).astype(jnp.bfloat16)
    for hd in range(H):
        a = q[:, (H + hd) * NP:(H + hd + 1) * NP]
        b = q[:, (2 * H + hd) * NP:(2 * H + hd + 1) * NP]
        q_ref[:, hd * MLA_QK_PAD:hd * MLA_QK_PAD + NP] = q[:, hd * NP:(hd + 1) * NP].astype(jnp.bfloat16)
        q_ref[:, hd * MLA_QK_PAD + NP:(hd + 1) * MLA_QK_PAD] = (a * cos + b * sin).astype(jnp.bfloat16)
        k_ref[:, hd * MLA_QK_PAD:hd * MLA_QK_PAD + NP] = k_nope[:, hd * NP:(hd + 1) * NP]
        k_ref[:, hd * MLA_QK_PAD + NP:(hd + 1) * MLA_QK_PAD] = k_rope


def _mla_attn_kernel(q_ref, k_ref, v_ref, o_ref):
    heads = range(ATTN_HEADS_PER_STEP)
    qk = lambda hd: slice(hd * MLA_QK_PAD, (hd + 1) * MLA_QK_PAD)
    vv = lambda hd: slice(hd * MLA_V_DIM, (hd + 1) * MLA_V_DIM)
    s = [_bdot_nt(q_ref[0, :, qk(hd)], k_ref[0, :, qk(hd)]) for hd in heads]
    p = [jnp.exp(x - jnp.max(x, axis=-1, keepdims=True)) for x in s]
    denom = [jnp.sum(x, axis=-1, keepdims=True) for x in p]
    for hd in heads:
        o_ref[0, :, vv(hd)] = (_bdot(p[hd], v_ref[0, :, vv(hd)]) / denom[hd]).astype(o_ref.dtype)


def _matmul_residual_kernel(x_ref, w_ref, h_ref, o_ref):
    o_ref[...] = h_ref[...] + _bdot(x_ref[...], w_ref[...])


def matmul_residual(x, w, h, name):
    T, K = x.shape
    D = w.shape[1]
    return pl.pallas_call(
        _matmul_residual_kernel,
        grid=(T // ROW_TILE,),
        in_specs=[pl.BlockSpec((ROW_TILE, K), lambda i: (i, 0)),
                  pl.BlockSpec((K, D), lambda i: (0, 0)),
                  pl.BlockSpec((ROW_TILE, D), lambda i: (i, 0))],
        out_specs=pl.BlockSpec((ROW_TILE, D), lambda i: (i, 0)),
        out_shape=jax.ShapeDtypeStruct((T, D), jnp.float32),
        name=name,
    )(x, w, h)


def _rope_swapped_columns(w):
    half = MLA_ROPE_DIM // 2
    pad = jnp.zeros(w.shape[:-1] + (LANES - MLA_ROPE_DIM,), w.dtype)
    plain = jnp.concatenate([w, pad], axis=-1)
    swapped = jnp.concatenate([w[..., half:], w[..., :half], pad], axis=-1)
    return plain, swapped


def mla_layer(h, norm_g, w_in, q_norm_g, w_qb, kv_norm_g, w_kvb, w_out):
    B, S, D = h.shape
    T = B * S
    H, NP, R = MLA_HEADS, MLA_NOPE_DIM, MLA_ROPE_DIM
    bf = jnp.bfloat16
    o = MLA_Q_LORA + MLA_KV_LORA
    kr_plain, kr_swapped = _rope_swapped_columns(w_in[:, o:])
    w_in_ext = jnp.concatenate([w_in[:, :o], kr_plain, kr_swapped], axis=1).astype(bf)
    w_qb_h = w_qb.reshape(MLA_Q_LORA, H, NP + R)
    qr_plain, qr_swapped = _rope_swapped_columns(w_qb_h[:, :, NP:])
    w_qb_ext = jnp.concatenate([w_qb_h[:, :, :NP].reshape(MLA_Q_LORA, H * NP),
                                qr_plain.reshape(MLA_Q_LORA, H * LANES),
                                qr_swapped.reshape(MLA_Q_LORA, H * LANES)], axis=1).astype(bf)
    w_kvb_h = w_kvb.reshape(MLA_KV_LORA, H, NP + MLA_V_DIM)
    w_k = w_kvb_h[:, :, :NP].reshape(MLA_KV_LORA, H * NP).astype(bf)
    w_v = w_kvb_h[:, :, NP:].reshape(MLA_KV_LORA, H * MLA_V_DIM).astype(bf)
    cos, sin = rope_tables(S)
    zeros = jnp.zeros((S, LANES - R), jnp.float32)
    cos_pad = jnp.concatenate([cos, cos, zeros], axis=1)
    sin_pad = jnp.concatenate([-sin, sin, zeros], axis=1)

    tiles_per_seq = S // ROW_TILE
    row = lambda i: (i, 0)
    fixed = lambda i: (0, 0)
    pos = lambda i: (i % tiles_per_seq, 0)
    q_cat, k_cat, v = pl.pallas_call(
        _mla_proj_kernel,
        grid=(T // ROW_TILE,),
        in_specs=[pl.BlockSpec((ROW_TILE, D), row),
                  pl.BlockSpec((1, D), fixed),
                  pl.BlockSpec(w_in_ext.shape, fixed),
                  pl.BlockSpec((1, MLA_Q_LORA), fixed),
                  pl.BlockSpec(w_qb_ext.shape, fixed),
                  pl.BlockSpec((1, MLA_KV_LORA), fixed),
                  pl.BlockSpec(w_k.shape, fixed),
                  pl.BlockSpec(w_v.shape, fixed),
                  pl.BlockSpec((ROW_TILE, LANES), pos),
                  pl.BlockSpec((ROW_TILE, LANES), pos)],
        out_specs=[pl.BlockSpec((ROW_TILE, H * MLA_QK_PAD), row),
                   pl.BlockSpec((ROW_TILE, H * MLA_QK_PAD), row),
                   pl.BlockSpec((ROW_TILE, H * MLA_V_DIM), row)],
        out_shape=[jax.ShapeDtypeStruct((T, H * MLA_QK_PAD), bf),
                   jax.ShapeDtypeStruct((T, H * MLA_QK_PAD), bf),
                   jax.ShapeDtypeStruct((T, H * MLA_V_DIM), bf)],
        name="mla_proj",
    )(h.reshape(T, D), norm_g.reshape(1, D), w_in_ext, q_norm_g.reshape(1, -1), w_qb_ext,
      kv_norm_g.reshape(1, -1), w_k, w_v, cos_pad, sin_pad)

    G = ATTN_HEADS_PER_STEP
    o_attn = pl.pallas_call(
        _mla_attn_kernel,
        grid=(B, H // G, S // ATTN_Q_TILE),
        in_specs=[pl.BlockSpec((1, ATTN_Q_TILE, G * MLA_QK_PAD), lambda b, hd, i: (b, i, hd)),
                  pl.BlockSpec((1, S, G * MLA_QK_PAD), lambda b, hd, i: (b, 0, hd)),
                  pl.BlockSpec((1, S, G * MLA_V_DIM), lambda b, hd, i: (b, 0, hd))],
        out_specs=pl.BlockSpec((1, ATTN_Q_TILE, G * MLA_V_DIM), lambda b, hd, i: (b, i, hd)),
        out_shape=jax.ShapeDtypeStruct((B, S, H * MLA_V_DIM), bf),
        compiler_params=pltpu.CompilerParams(
            dimension_semantics=("parallel", "parallel", "arbitrary"),
            vmem_limit_bytes=VMEM_LIMIT_BYTES),
        name="mla_attention",
    )(q_cat.reshape(B, S, -1), k_cat.reshape(B, S, -1), v.reshape(B, S, -1))

    h_new = matmul_residual(o_attn.reshape(T, H * MLA_V_DIM), w_out.astype(bf), h.reshape(T, D),
                            name="mla_out_proj")
    return h_new.reshape(B, S, D)


def _bdot(a, b):
    return jnp.dot(a.astype(jnp.bfloat16), b.astype(jnp.bfloat16), preferred_element_type=jnp.float32)


def _bdot_nt(a, b):
    return lax.dot_general(a.astype(jnp.bfloat16), b.astype(jnp.bfloat16),
                           (((1,), (1,)), ((), ())), preferred_element_type=jnp.float32)


def _unit_triangular_inverses(Ls, row, col):
    C = Ls[0].shape[0]
    eye = (row == col).astype(jnp.float32)

    def same_block(s):
        return (row // s) == (col // s)

    blk8 = same_block(8)
    L8 = [jnp.where(blk8, L, 0.0) for L in Ls]
    P2 = [_bdot(a, a) for a in L8]
    P4 = [_bdot(a, a) for a in P2]
    X = [eye - a for a in L8]
    X = [x + _bdot(x, p) for x, p in zip(X, P2)]
    X = [x + _bdot(x, p) for x, p in zip(X, P4)]
    s = 8
    while s < C:
        pair = same_block(2 * s) & jnp.logical_not(same_block(s))
        XC = [_bdot(x, jnp.where(pair, L, 0.0)) for x, L in zip(X, Ls)]
        X = [x - _bdot(xc, x) for x, xc in zip(X, XC)]
        s *= 2
    return X


def _gdn_chunk_kernel(qf_ref, kf_ref, vf_ref, gcf_ref, bcf_ref, grf_ref,
                      qb_ref, kb_ref, vb_ref, gcb_ref, bcb_ref, grb_ref,
                      of_ref, ob_ref, state_ref):
    H, C, DK, DV = GDN_HEADS, GDN_CHUNK, GDN_DK, GDN_DV

    @pl.when(pl.program_id(1) == 0)
    def _():
        state_ref[...] = jnp.zeros_like(state_ref)

    row = lax.broadcasted_iota(jnp.int32, (C, C), 0)
    col = lax.broadcasted_iota(jnp.int32, (C, C), 1)
    dirs = ((0, qf_ref, kf_ref, vf_ref, gcf_ref, bcf_ref, grf_ref, of_ref),
            (1, qb_ref, kb_ref, vb_ref, gcb_ref, bcb_ref, grb_ref, ob_ref))
    chains = []
    for d, q_ref, k_ref, v_ref, gc_ref, bc_ref, gr_ref, o_ref in dirs:
        if d == 0:
            incl, strict, last = row >= col, row > col, C - 1
        else:
            incl, strict, last = row <= col, row < col, 0
        tri = incl.astype(jnp.float32)
        gc_cols = jnp.dot(tri, gc_ref[0], precision=lax.Precision.HIGHEST,
                          preferred_element_type=jnp.float32)
        gc_rows = jnp.dot(gr_ref[0, 0], tri.T, precision=lax.Precision.HIGHEST,
                          preferred_element_type=jnp.float32)
        beta_cols = bc_ref[0]
        for h in range(H):
            c = d * H + h
            chains.append(dict(
                c=c, incl=incl, strict=strict, o_ref=o_ref, sl=slice(h * DK, (h + 1) * DK),
                q_ref=q_ref, k_ref=k_ref, v_ref=v_ref,
                beta=beta_cols[:, c:c + 1], gcc=gc_cols[:, c:c + 1], gcr=gc_rows[c:c + 1, :],
                gl=gc_cols[last:last + 1, c:c + 1]))

    for ch in chains:
        ch['decay'] = jnp.where(ch['incl'], jnp.exp(jnp.where(ch['incl'], ch['gcc'] - ch['gcr'], 0.0)), 0.0)
        ch['eg'] = jnp.exp(ch['gcc'])
    for ch in chains:
        k = ch['k_ref'][0, :, ch['sl']]
        q = ch['q_ref'][0, :, ch['sl']]
        kb = k * ch['beta']
        ch['kbeg'] = kb * ch['eg']
        ch['s1'] = _bdot_nt(jnp.concatenate([kb, q], axis=0), k)
    Ls = [jnp.where(ch['strict'], ch['s1'][:C] * ch['decay'], 0.0) for ch in chains]
    Ts = _unit_triangular_inverses(Ls, row, col)
    for ch, T in zip(chains, Ts):
        v = ch['v_ref'][0, :, ch['sl']]
        ch['uw'] = _bdot(T, jnp.concatenate([v * ch['beta'], ch['kbeg']], axis=1))
    for ch in chains:
        q = ch['q_ref'][0, :, ch['sl']]
        w = ch['uw'][:, DV:]
        ch['ws'] = _bdot(jnp.concatenate([w, q * ch['eg']], axis=0), state_ref[ch['c']])
    for ch in chains:
        ch['v_new'] = ch['uw'][:, :DV] - ch['ws'][:C]
        a_intra = ch['s1'][C:] * ch['decay']
        ch['o_ref'][0, :, ch['sl']] = ch['ws'][C:] + _bdot(a_intra, ch['v_new'])
    for ch in chains:
        k = ch['k_ref'][0, :, ch['sl']]
        k_g = k * jnp.exp(ch['gl'] - ch['gcc'])
        c = ch['c']
        state_ref[c] = state_ref[c] * jnp.exp(ch['gl']) + _bdot(k_g.T, ch['v_new'])


def gdn_bidirectional_delta(q, k, v, beta, g):
    B, S, HD = q.shape
    C, H = GDN_CHUNK, GDN_HEADS
    N = S // C
    g_rows = g.reshape(B, N, C, 2 * H).transpose(0, 1, 3, 2)

    def fwd3(b, n):
        return (b, n, 0)

    def bwd3(b, n):
        return (b, N - 1 - n, 0)

    def fwd4(b, n):
        return (b, n, 0, 0)

    def bwd4(b, n):
        return (b, N - 1 - n, 0, 0)

    def specs(i3, i4):
        big = pl.BlockSpec((1, C, HD), i3)
        small = pl.BlockSpec((1, C, 2 * H), i3)
        return [big, big, big, small, small, pl.BlockSpec((1, 1, 2 * H, C), i4)]

    out = jax.ShapeDtypeStruct((B, S, HD), jnp.float32)
    return pl.pallas_call(
        _gdn_chunk_kernel,
        grid=(B, N),
        in_specs=specs(fwd3, fwd4) + specs(bwd3, bwd4),
        out_specs=[pl.BlockSpec((1, C, HD), fwd3), pl.BlockSpec((1, C, HD), bwd3)],
        out_shape=[out, out],
        scratch_shapes=[pltpu.VMEM((2 * H, GDN_DK, GDN_DV), jnp.float32)],
        compiler_params=pltpu.CompilerParams(dimension_semantics=("parallel", "arbitrary")),
        name="gdn_chunk_delta",
    )(q, k, v, g, beta, g_rows, q, k, v, g, beta, g_rows)


GDN_ROW_TILE = 256
HALO = 8


def _gdn_in_kernel(h_ref, hp_ref, hn_ref, g_ref, wqkv_ref, wz_ref, wba_ref, cw_ref, alog_ref, dtb_ref,
                   q_ref, k_ref, v_ref, z_ref, beta_ref, gl_ref, pre_ref, *, tiles_per_seq):
    H, DK = GDN_HEADS, GDN_DK
    TM = h_ref.shape[0]
    i = pl.program_id(0) % tiles_per_seq
    keep_prev = jnp.where(i == 0, 0.0, 1.0)
    keep_next = jnp.where(i == tiles_per_seq - 1, 0.0, 1.0)
    hx = jnp.concatenate([hp_ref[...] * keep_prev, h_ref[...], hn_ref[...] * keep_next], axis=0)
    hn = _rms(hx, g_ref[...]).astype(jnp.bfloat16)
    pre = _bdot(hn, wqkv_ref[...])
    pre_ref[...] = pre
    pad = GDN_CONV // 2
    acc = None
    for j in range(GDN_CONV):
        term = pre_ref[pl.ds(HALO + j - pad, TM), :] * cw_ref[j:j + 1, :]
        acc = term if acc is None else acc + term
    qkv = _silu(acc)
    for hd in range(H):
        for off, ref, scale in ((0, q_ref, DK ** -0.5), (H * DK, k_ref, 1.0)):
            x = qkv[:, off + hd * DK:off + (hd + 1) * DK]
            ref[:, hd * DK:(hd + 1) * DK] = x * (lax.rsqrt(jnp.sum(x * x, axis=-1, keepdims=True) + NORM_EPS) * scale)
    v_ref[...] = qkv[:, 2 * H * DK:]
    hc = hn[HALO:HALO + TM]
    z_ref[...] = _bdot(hc, wz_ref[...])
    ba = _bdot(hc, wba_ref[...])
    beta_ref[...] = _sigmoid(ba[:, :2 * H])
    a = ba[:, 2 * H:] + dtb_ref[...]
    softplus = jnp.maximum(a, 0.0) + jnp.log(1.0 + jnp.exp(-jnp.abs(a)))
    gl_ref[...] = -jnp.exp(alog_ref[...]) * softplus


def _gdn_out_kernel(of_ref, ob_ref, z_ref, g_ref, w_ref, h_ref, o_ref):
    H, DV = GDN_HEADS, GDN_DV
    o = of_ref[...] + ob_ref[...]
    g = g_ref[...]
    parts = []
    for hd in range(H):
        x = o[:, hd * DV:(hd + 1) * DV]
        parts.append(_rms(x, g))
    y = jnp.concatenate(parts, axis=1) * _silu(z_ref[...])
    o_ref[...] = h_ref[...] + _bdot(y, w_ref[...])


def gdn_layer(h, norm_g, w_in, conv_w, A_log, dt_bias, out_norm_g, w_out):
    B, S, D = h.shape
    T = B * S
    H = GDN_HEADS
    HD = H * GDN_DK
    bf = jnp.bfloat16
    TM = GDN_ROW_TILE
    tiles_per_seq = S // TM
    hb = TM // HALO
    n_halo = T // HALO
    o = GDN_QKV_DIM + H * GDN_DV
    w_qkv = w_in[:, :GDN_QKV_DIM].astype(bf)
    w_z = w_in[:, GDN_QKV_DIM:o].astype(bf)
    w_ba = w_in[:, o:].astype(bf)
    cw = jnp.concatenate([conv_w, jnp.zeros((HALO - GDN_CONV, GDN_QKV_DIM), conv_w.dtype)], axis=0)
    row = lambda i: (i, 0)
    fixed = lambda i: (0, 0)
    f32 = jnp.float32
    ht = h.reshape(T, D)
    q, k, v, z, beta, gl = pl.pallas_call(
        functools.partial(_gdn_in_kernel, tiles_per_seq=tiles_per_seq),
        grid=(T // TM,),
        in_specs=[pl.BlockSpec((TM, D), row),
                  pl.BlockSpec((HALO, D), lambda i: (jnp.maximum(i * hb - 1, 0), 0)),
                  pl.BlockSpec((HALO, D), lambda i: (jnp.minimum((i + 1) * hb, n_halo - 1), 0)),
                  pl.BlockSpec((1, D), fixed),
                  pl.BlockSpec(w_qkv.shape, fixed), pl.BlockSpec(w_z.shape, fixed),
                  pl.BlockSpec(w_ba.shape, fixed), pl.BlockSpec(cw.shape, fixed),
                  pl.BlockSpec((1, 2 * H), fixed), pl.BlockSpec((1, 2 * H), fixed)],
        out_specs=[pl.BlockSpec((TM, HD), row), pl.BlockSpec((TM, HD), row), pl.BlockSpec((TM, HD), row),
                   pl.BlockSpec((TM, HD), row), pl.BlockSpec((TM, 2 * H), row), pl.BlockSpec((TM, 2 * H), row)],
        out_shape=[jax.ShapeDtypeStruct((T, HD), f32)] * 4 + [jax.ShapeDtypeStruct((T, 2 * H), f32)] * 2,
        scratch_shapes=[pltpu.VMEM((TM + 2 * HALO, GDN_QKV_DIM), f32)],
        compiler_params=pltpu.CompilerParams(vmem_limit_bytes=VMEM_LIMIT_BYTES),
        name="gdn_in_proj",
    )(ht, ht, ht, norm_g.reshape(1, D), w_qkv, w_z, w_ba, cw,
      A_log.reshape(1, 2 * H).astype(f32), dt_bias.reshape(1, 2 * H).astype(f32))

    o_fwd, o_bwd = gdn_bidirectional_delta(q.reshape(B, S, HD), k.reshape(B, S, HD), v.reshape(B, S, HD),
                                           beta.reshape(B, S, 2 * H), gl.reshape(B, S, 2 * H))

    h_new = pl.pallas_call(
        _gdn_out_kernel,
        grid=(T // ROW_TILE,),
        in_specs=[pl.BlockSpec((ROW_TILE, HD), row)] * 3 +
                 [pl.BlockSpec((1, GDN_DV), fixed), pl.BlockSpec((HD, D), fixed), pl.BlockSpec((ROW_TILE, D), row)],
        out_specs=pl.BlockSpec((ROW_TILE, D), row),
        out_shape=jax.ShapeDtypeStruct((T, D), f32),
        name="gdn_out_proj",
    )(o_fwd.reshape(T, HD), o_bwd.reshape(T, HD), z, out_norm_g.reshape(1, GDN_DV), w_out.astype(bf), ht)
    return h_new.reshape(B, S, D)


def _sigmoid(x):
    return 0.5 * jnp.tanh(0.5 * x) + 0.5


def _silu(x):
    return x * _sigmoid(x)


def _moe_router_kernel(h_ref, g_ref, w_ref, xn_ref, aff_ref):
    xn = _rms(h_ref[...], g_ref[...])
    xn_ref[...] = xn.astype(xn_ref.dtype)
    logits = _bdot(xn, w_ref[...])
    e = jnp.exp(logits - jnp.max(logits, axis=-1, keepdims=True))
    aff_ref[...] = e / jnp.sum(e, axis=-1, keepdims=True)


def _moe_ffn_kernel(x_ref, wg_ref, wu_ref, wd_ref, gate_ref, y_ref):
    x = x_ref[0]
    hid = _silu(_bdot(x, wg_ref[0])) * _bdot(x, wu_ref[0])
    y_ref[0] = (_bdot(hid, wd_ref[0]) * gate_ref[0]).astype(y_ref.dtype)


COMBINE_TILE = 256
COMBINE_SUBTILES = 2
STRIP_ROWS = 64
STRIP_ALIGN = 16
STRIPS_PER_DOT = 4


def _moe_combine_ple_kernel(base_ref, off_ref, cnt_ref,
                            h_ref, p_ref, g_ref, wg_ref, wp_ref, fg_ref, ye_ref, tok_ref,
                            o_ref, ybuf, tbuf, sems, yextra, textra, extra_sems, acc_ref,
                            *, n_tiles, total_rows, final_norm):
    E, TB, R, U = N_EXPERTS, COMBINE_TILE, STRIP_ROWS, COMBINE_SUBTILES
    K = STRIPS_PER_DOT * R
    n_steps = n_tiles // U
    j = pl.program_id(0)
    cur = j % 2

    def strip_copies(step, u, e, b):
        start = pl.multiple_of(base_ref[e * n_tiles + step * U + u], STRIP_ALIGN)
        rows = pl.ds((u * E + e) * R, R)
        return (pltpu.make_async_copy(ye_ref.at[pl.ds(start, R), :], ybuf.at[b, rows, :],
                                      sems.at[0, b, u * E + e]),
                pltpu.make_async_copy(tok_ref.at[pl.ds(start, R), :], tbuf.at[b, rows, :],
                                      sems.at[1, b, u * E + e]))

    def all_strip_copies(step, b):
        return [cp for u in range(U) for e in range(E) for cp in strip_copies(step, u, e, b)]

    @pl.when(j == 0)
    def _():
        for cp in all_strip_copies(0, 0):
            cp.start()

    @pl.when(j + 1 < n_steps)
    def _():
        for cp in all_strip_copies(j + 1, 1 - cur):
            cp.start()

    for cp in all_strip_copies(j, cur):
        cp.wait()

    lane = lax.broadcasted_iota(jnp.int32, (K, TB), 1)
    srow = lax.broadcasted_iota(jnp.int32, (R, 1), 0)
    groups = [(u, grp) for u in range(U) for grp in range(E // STRIPS_PER_DOT)]
    hots = []
    for u, grp in groups:
        tile = j * U + u
        wanted = []
        for e in range(grp * STRIPS_PER_DOT, (grp + 1) * STRIPS_PER_DOT):
            lo = off_ref[e * n_tiles + tile]
            wanted.append((srow >= lo) & (srow < lo + cnt_ref[e * n_tiles + tile]))
        rows = pl.ds((u * E + grp * STRIPS_PER_DOT) * R, K)
        tok = tbuf[cur, rows, 0:1] - tile * TB
        hots.append(jnp.where(jnp.concatenate(wanted, axis=0) & (lane == tok), 1.0, 0.0))
    parts = [_bdot(hot.T, ybuf[cur, pl.ds((u * E + grp * STRIPS_PER_DOT) * R, K), :])
             for hot, (u, grp) in zip(hots, groups)]
    for u in range(U):
        mine = [part for part, (uu, _) in zip(parts, groups) if uu == u]
        acc_ref[u] = functools.reduce(lambda a, b: a + b, mine)

    lane_r = lax.broadcasted_iota(jnp.int32, (R, TB), 1)
    for u in range(U):
        tile = j * U + u
        for e in range(E):
            base = base_ref[e * n_tiles + tile]
            first = base + off_ref[e * n_tiles + tile]
            end = first + cnt_ref[e * n_tiles + tile]
            n_sub = (end - base + R - 1) // R

            def body(k, carry, u=u, tile=tile, base=base, first=first, end=end):
                want0 = base + k * R
                start = pl.multiple_of(jnp.minimum(want0, total_rows - R), STRIP_ALIGN)
                cy = pltpu.make_async_copy(ye_ref.at[pl.ds(start, R), :], yextra, extra_sems.at[0])
                ct = pltpu.make_async_copy(tok_ref.at[pl.ds(start, R), :], textra, extra_sems.at[1])
                cy.start()
                ct.start()
                cy.wait()
                ct.wait()
                grow = srow + start
                wanted = (grow >= jnp.maximum(want0, first)) & (grow < jnp.minimum(want0 + R, end))
                hot_t = jnp.where(wanted & (lane_r == textra[:, 0:1] - tile * TB), 1.0, 0.0)
                acc_ref[u] += _bdot(hot_t.T, yextra[...])
                return carry

            lax.fori_loop(1, n_sub, body, 0)

    tiles = [pl.ds(u * TB, TB) for u in range(U)]
    hm = [h_ref[t, :] + acc_ref[u] for u, t in enumerate(tiles)]
    gate = [_sigmoid(_bdot(_rms(x, g_ref[...]), wg_ref[...])) for x in hm]
    out = [x + _bdot(p_ref[t, :], wp_ref[...]) * gt for x, gt, t in zip(hm, gate, tiles)]
    for x, t in zip(out, tiles):
        o_ref[t, :] = _rms(x, fg_ref[...]) if final_norm else x


F32_INF_BITS = 0x7F800000


def _topk_select_kernel(aff_ref, wcum_ref, *, cap):
    a = aff_ref[0]
    R = a.shape[0]
    bits = lax.bitcast_convert_type(a, jnp.int32)

    def count(mask):
        m = mask.astype(jnp.float32)
        return jnp.sum(jnp.sum(m, axis=0, keepdims=True), axis=1, keepdims=True)

    def bisect(_, carry):
        lo, hi = carry
        mid = lo + jnp.right_shift(hi - lo + 1, 1)
        ok = count(bits >= mid) >= cap
        return jnp.where(ok, mid, lo), jnp.where(ok, hi, mid - 1)

    thr, _ = lax.fori_loop(0, 31, bisect, (jnp.zeros((1, 1), jnp.int32),
                                           jnp.full((1, 1), F32_INF_BITS, jnp.int32)))
    gt = bits > thr
    eq = bits == thr
    need = cap - count(gt)

    lane_r = lax.broadcasted_iota(jnp.int32, (LANES, LANES), 0)
    lane_c = lax.broadcasted_iota(jnp.int32, (LANES, LANES), 1)
    upper = (lane_r <= lane_c).astype(jnp.bfloat16)
    row_r = lax.broadcasted_iota(jnp.int32, (R, R), 0)
    row_c = lax.broadcasted_iota(jnp.int32, (R, R), 1)
    earlier_rows = (row_r > row_c).astype(jnp.bfloat16)

    eq_in_row = jnp.dot(eq.astype(jnp.bfloat16), upper, preferred_element_type=jnp.float32)
    eq_row_total = jnp.broadcast_to(eq_in_row[:, LANES - 1:LANES], (R, LANES))
    eq_before = jnp.dot(earlier_rows, eq_row_total.astype(jnp.bfloat16), preferred_element_type=jnp.float32)
    sel = gt | (eq & (eq_before + eq_in_row <= need))
    wcum = jnp.dot(sel.astype(jnp.bfloat16), upper, preferred_element_type=jnp.float32)
    wcum_ref[0] = wcum.astype(jnp.int32)


def expert_choice_indices(aff_t, cap):
    E, T = aff_t.shape
    R = T // LANES
    wcum = pl.pallas_call(
        functools.partial(_topk_select_kernel, cap=cap),
        grid=(E,),
        in_specs=[pl.BlockSpec((1, R, LANES), lambda e: (e, 0, 0))],
        out_specs=pl.BlockSpec((1, R, LANES), lambda e: (e, 0, 0)),
        out_shape=jax.ShapeDtypeStruct((E, R, LANES), jnp.int32),
        name="moe_topk_select",
    )(aff_t.reshape(E, R, LANES))
    row_end = jnp.cumsum(wcum[:, :, LANES - 1], axis=1)
    row_start = row_end - wcum[:, :, LANES - 1]
    slot = jnp.arange(cap, dtype=jnp.int32)
    done = row_end[:, None, :] <= slot[None, :, None]
    row = jnp.sum(done.astype(jnp.int32), axis=2)
    local = slot[None, :] - jnp.max(jnp.where(done, row_end[:, None, :], 0), axis=2)
    flat_row = row + (jnp.arange(E, dtype=jnp.int32) * R)[:, None]
    wrow = wcum.reshape(E * R, LANES).at[flat_row].get(mode='promise_in_bounds')
    lane = jnp.sum((wrow <= local[:, :, None]).astype(jnp.int32), axis=2)
    return row * LANES + lane, row_start


def moe_ple_layer(h, p, norm_g, w_router, w_gate, w_up, w_down, ple_norm_g, ple_w_gate, ple_w_proj,
                  final_g, final_norm):
    T, D = h.shape
    E, F = N_EXPERTS, w_gate.shape[-1]
    cap = EC_CAPACITY_FACTOR * T // E
    bf = jnp.bfloat16
    row = lambda i: (i, 0)
    fixed = lambda i: (0, 0)
    xn, aff = pl.pallas_call(
        _moe_router_kernel,
        grid=(T // ROW_TILE,),
        in_specs=[pl.BlockSpec((ROW_TILE, D), row), pl.BlockSpec((1, D), fixed),
                  pl.BlockSpec((D, E), fixed)],
        out_specs=[pl.BlockSpec((ROW_TILE, D), row), pl.BlockSpec((ROW_TILE, E), row)],
        out_shape=[jax.ShapeDtypeStruct((T, D), bf), jax.ShapeDtypeStruct((T, E), jnp.float32)],
        name="moe_router",
    )(h, norm_g.reshape(1, D), w_router.astype(bf))
    aff_t = aff.T
    idx, row_start = expert_choice_indices(aff_t, cap)
    gate = jnp.take_along_axis(aff_t, idx, axis=1)
    xe = xn.at[idx].get(mode='promise_in_bounds')
    ye = pl.pallas_call(
        _moe_ffn_kernel,
        grid=(E, cap // ROW_TILE),
        in_specs=[pl.BlockSpec((1, ROW_TILE, D), lambda e, i: (e, i, 0)),
                  pl.BlockSpec((1, D, F), lambda e, i: (e, 0, 0)),
                  pl.BlockSpec((1, D, F), lambda e, i: (e, 0, 0)),
                  pl.BlockSpec((1, F, D), lambda e, i: (e, 0, 0)),
                  pl.BlockSpec((1, ROW_TILE, 1), lambda e, i: (e, i, 0))],
        out_specs=pl.BlockSpec((1, ROW_TILE, D), lambda e, i: (e, i, 0)),
        out_shape=jax.ShapeDtypeStruct((E, cap, D), bf),
        compiler_params=pltpu.CompilerParams(dimension_semantics=("parallel", "arbitrary")),
        name="moe_expert_ffn",
    )(xe, w_gate.astype(bf), w_up.astype(bf), w_down.astype(bf), gate[..., None])

    TB, R, U = COMBINE_TILE, STRIP_ROWS, COMBINE_SUBTILES
    n_tiles = T // TB
    total_rows = E * cap
    lo = jnp.concatenate([row_start[:, ::TB // LANES], jnp.full((E, 1), cap, jnp.int32)], axis=1)
    first = lo[:, :-1] + (jnp.arange(E, dtype=jnp.int32) * cap)[:, None]
    cnt = lo[:, 1:] - lo[:, :-1]
    base = jnp.minimum(first // STRIP_ALIGN * STRIP_ALIGN, total_rows - R)
    off = first - base
    tok = jnp.broadcast_to(idx.reshape(total_rows, 1), (total_rows, LANES))

    P = p.shape[1]
    tile = lambda i, *_: (i, 0)
    const = lambda i, *_: (0, 0)
    return pl.pallas_call(
        functools.partial(_moe_combine_ple_kernel, n_tiles=n_tiles, total_rows=total_rows,
                          final_norm=final_norm),
        grid_spec=pltpu.PrefetchScalarGridSpec(
            num_scalar_prefetch=3,
            grid=(n_tiles // U,),
            in_specs=[pl.BlockSpec((U * TB, D), tile), pl.BlockSpec((U * TB, P), tile),
                      pl.BlockSpec((1, D), const), pl.BlockSpec((D, D), const), pl.BlockSpec((P, D), const),
                      pl.BlockSpec((1, D), const),
                      pl.BlockSpec(memory_space=pl.ANY), pl.BlockSpec(memory_space=pl.ANY)],
            out_specs=pl.BlockSpec((U * TB, D), tile),
            scratch_shapes=[pltpu.VMEM((2, U * E * R, D), bf), pltpu.VMEM((2, U * E * R, LANES), jnp.int32),
                            pltpu.SemaphoreType.DMA((2, 2, U * E)),
                            pltpu.VMEM((R, D), bf), pltpu.VMEM((R, LANES), jnp.int32),
                            pltpu.SemaphoreType.DMA((2,)),
                            pltpu.VMEM((U, TB, D), jnp.float32)]),
        out_shape=jax.ShapeDtypeStruct((T, D), jnp.float32),
        compiler_params=pltpu.CompilerParams(dimension_semantics=("arbitrary",),
                                             vmem_limit_bytes=VMEM_LIMIT_BYTES),
        name="moe_combine_ple",
    )(base.reshape(-1), off.reshape(-1), cnt.reshape(-1),
      h, p, ple_norm_g.reshape(1, D), ple_w_gate.astype(bf), ple_w_proj.astype(bf), final_g.reshape(1, D),
      ye.reshape(total_rows, D), tok)


def encoder_trunk(x, p, W):
    h = x
    for i in range(DEPTH):
        j = i // N_MIXERS
        if i % N_MIXERS == 0:
            h = mla_layer(h, W['norm_mix_g'][i], W['mla_w_in'][j], W['mla_q_norm_g'][j], W['mla_w_qb'][j],
                          W['mla_kv_norm_g'][j], W['mla_w_kvb'][j], W['mla_w_out'][j])
        else:
            h = gdn_layer(h, W['norm_mix_g'][i], W['gdn_w_in'][j], W['gdn_conv_w'][j], W['gdn_A_log'][j],
                          W['gdn_dt_bias'][j], W['gdn_norm_g'][j], W['gdn_w_out'][j])
        B, S, D = h.shape
        ht = moe_ple_layer(h.reshape(B * S, D), p[i].reshape(B * S, PLE_DIM), W['norm_ffn_g'][i],
                           W['moe_w_router'][i], W['moe_w_gate'][i], W['moe_w_up'][i], W['moe_w_down'][i],
                           W['norm_ple_g'][i], W['ple_w_gate'][i], W['ple_w_proj'][i],
                           W['final_norm_g'], final_norm=(i == DEPTH - 1))
        h = ht.reshape(B, S, D)
    return h


def kernel(x_prompt, x_sample, p_prompt, p_sample, norm_mix_g, norm_ffn_g, norm_ple_g, final_norm_g,
           mla_w_in, mla_q_norm_g, mla_w_qb, mla_kv_norm_g, mla_w_kvb, mla_w_out,
           gdn_w_in, gdn_conv_w, gdn_A_log, gdn_dt_bias, gdn_norm_g, gdn_w_out,
           moe_w_router, moe_w_gate, moe_w_up, moe_w_down, ple_w_proj, ple_w_gate):
    W = {
        'norm_mix_g': norm_mix_g, 'norm_ffn_g': norm_ffn_g, 'norm_ple_g': norm_ple_g,
        'final_norm_g': final_norm_g,
        'mla_w_in': mla_w_in, 'mla_q_norm_g': mla_q_norm_g, 'mla_w_qb': mla_w_qb,
        'mla_kv_norm_g': mla_kv_norm_g, 'mla_w_kvb': mla_w_kvb, 'mla_w_out': mla_w_out,
        'gdn_w_in': gdn_w_in, 'gdn_conv_w': gdn_conv_w, 'gdn_A_log': gdn_A_log,
        'gdn_dt_bias': gdn_dt_bias, 'gdn_norm_g': gdn_norm_g, 'gdn_w_out': gdn_w_out,
        'moe_w_router': moe_w_router, 'moe_w_gate': moe_w_gate, 'moe_w_up': moe_w_up,
        'moe_w_down': moe_w_down, 'ple_w_proj': ple_w_proj, 'ple_w_gate': ple_w_gate,
    }
    y_prompt = encoder_trunk(x_prompt, p_prompt, W)
    y_sample = encoder_trunk(x_sample, p_sample, W)
    return (y_prompt, y_sample)
```

```python
import functools

import jax
import jax.numpy as jnp
from jax import lax
from jax.experimental import pallas as pl
from jax.experimental.pallas import tpu as pltpu

D_MODEL = 1024
DEPTH = 4
PLE_DIM = 256
N_MIXERS = 2
MLA_HEADS = 8
MLA_NOPE_DIM = 128
MLA_ROPE_DIM = 64
MLA_V_DIM = 128
MLA_Q_LORA = 256
MLA_KV_LORA = 128
ROPE_THETA = 10000.0
Q_BLOCK = 128
GDN_HEADS = 8
GDN_DK = 128
GDN_DV = 128
GDN_CONV = 5
GDN_CHUNK = 64
GDN_QKV_DIM = 2 * GDN_HEADS * GDN_DK + GDN_HEADS * GDN_DV
N_EXPERTS = 16
EC_CAPACITY_FACTOR = 2
NORM_EPS = 1e-6


def rope_tables(seq):
    pos = jnp.arange(seq, dtype=jnp.float32)
    inv = ROPE_THETA ** (-jnp.arange(0, MLA_ROPE_DIM, 2, dtype=jnp.float32) / MLA_ROPE_DIM)
    ang = pos[:, None] * inv[None, :]
    return jnp.cos(ang), jnp.sin(ang)


LANES = 128
ROW_TILE = 512
ATTN_Q_TILE = 512
ATTN_HEADS_PER_STEP = 4
VMEM_LIMIT_BYTES = 56 * 1024 * 1024
MLA_QK_PAD = 2 * LANES


def _rms(x, g):
    return x * lax.rsqrt(jnp.mean(x * x, axis=-1, keepdims=True) + NORM_EPS) * g


def _mla_proj_kernel(h_ref, g_ref, w_in_ref, qg_ref, w_qb_ref, kvg_ref, w_k_ref, w_v_ref,
                     cos_ref, sin_ref, q_ref, k_ref, v_ref):
    H, NP = MLA_HEADS, MLA_NOPE_DIM
    scale = (MLA_NOPE_DIM + MLA_ROPE_DIM) ** -0.5
    hn = _rms(h_ref[...], g_ref[...])
    lat = _bdot(hn, w_in_ref[...])
    q_lat = lat[:, :MLA_Q_LORA]
    kv_lat = lat[:, MLA_Q_LORA:MLA_Q_LORA + MLA_KV_LORA]
    o = MLA_Q_LORA + MLA_KV_LORA
    cos, sin = cos_ref[...], sin_ref[...]
    k_rope = (lat[:, o:o + LANES] * cos + lat[:, o + LANES:o + 2 * LANES] * sin).astype(jnp.bfloat16)
    q = _bdot(_rms(q_lat, qg_ref[...]), w_qb_ref[...]) * scale
    kvn = _rms(kv_lat, kvg_ref[...])
    k_nope = _bdot(kvn, w_k_ref[...]).astype(jnp.bfloat16)
    v_ref[...] = _bdot(kvn, w_v_ref[...]).astype(jnp.bfloat16)
    for hd in range(H):
        a = q[:, (H + hd) * NP:(H + hd + 1) * NP]
        b = q[:, (2 * H + hd) * NP:(2 * H + hd + 1) * NP]
        q_ref[:, hd * MLA_QK_PAD:hd * MLA_QK_PAD + NP] = q[:, hd * NP:(hd + 1) * NP].astype(jnp.bfloat16)
        q_ref[:, hd * MLA_QK_PAD + NP:(hd + 1) * MLA_QK_PAD] = (a * cos + b * sin).astype(jnp.bfloat16)
        k_ref[:, hd * MLA_QK_PAD:hd * MLA_QK_PAD + NP] = k_nope[:, hd * NP:(hd + 1) * NP]
        k_ref[:, hd * MLA_QK_PAD + NP:(hd + 1) * MLA_QK_PAD] = k_rope


def _mla_attn_kernel(q_ref, k_ref, v_ref, o_ref):
    heads = range(ATTN_HEADS_PER_STEP)
    qk = lambda hd: slice(hd * MLA_QK_PAD, (hd + 1) * MLA_QK_PAD)
    vv = lambda hd: slice(hd * MLA_V_DIM, (hd + 1) * MLA_V_DIM)
    s = [_bdot_nt(q_ref[0, :, qk(hd)], k_ref[0, :, qk(hd)]) for hd in heads]
    p = [jnp.exp(x - jnp.max(x, axis=-1, keepdims=True)) for x in s]
    denom = [jnp.sum(x, axis=-1, keepdims=True) for x in p]
    for hd in heads:
        o_ref[0, :, vv(hd)] = (_bdot(p[hd], v_ref[0, :, vv(hd)]) / denom[hd]).astype(o_ref.dtype)


def _matmul_residual_kernel(x_ref, w_ref, h_ref, o_ref):
    o_ref[...] = h_ref[...] + _bdot(x_ref[...], w_ref[...])


def matmul_residual(x, w, h, name):
    T, K = x.shape
    D = w.shape[1]
    return pl.pallas_call(
        _matmul_residual_kernel,
        grid=(T // ROW_TILE,),
        in_specs=[pl.BlockSpec((ROW_TILE, K), lambda i: (i, 0)),
                  pl.BlockSpec((K, D), lambda i: (0, 0)),
                  pl.BlockSpec((ROW_TILE, D), lambda i: (i, 0))],
        out_specs=pl.BlockSpec((ROW_TILE, D), lambda i: (i, 0)),
        out_shape=jax.ShapeDtypeStruct((T, D), jnp.float32),
        name=name,
    )(x, w, h)


def _rope_swapped_columns(w):
    half = MLA_ROPE_DIM // 2
    pad = jnp.zeros(w.shape[:-1] + (LANES - MLA_ROPE_DIM,), w.dtype)
    plain = jnp.concatenate([w, pad], axis=-1)
    swapped = jnp.concatenate([w[..., half:], w[..., :half], pad], axis=-1)
    return plain, swapped


def mla_layer(h, norm_g, w_in, q_norm_g, w_qb, kv_norm_g, w_kvb, w_out):
    B, S, D = h.shape
    T = B * S
    H, NP, R = MLA_HEADS, MLA_NOPE_DIM, MLA_ROPE_DIM
    bf = jnp.bfloat16
    o = MLA_Q_LORA + MLA_KV_LORA
    kr_plain, kr_swapped = _rope_swapped_columns(w_in[:, o:])
    w_in_ext = jnp.concatenate([w_in[:, :o], kr_plain, kr_swapped], axis=1).astype(bf)
    w_qb_h = w_qb.reshape(MLA_Q_LORA, H, NP + R)
    qr_plain, qr_swapped = _rope_swapped_columns(w_qb_h[:, :, NP:])
    w_qb_ext = jnp.concatenate([w_qb_h[:, :, :NP].reshape(MLA_Q_LORA, H * NP),
                                qr_plain.reshape(MLA_Q_LORA, H * LANES),
                                qr_swapped.reshape(MLA_Q_LORA, H * LANES)], axis=1).astype(bf)
    w_kvb_h = w_kvb.reshape(MLA_KV_LORA, H, NP + MLA_V_DIM)
    w_k = w_kvb_h[:, :, :NP].reshape(MLA_KV_LORA, H * NP).astype(bf)
    w_v = w_kvb_h[:, :, NP:].reshape(MLA_KV_LORA, H * MLA_V_DIM).astype(bf)
    cos, sin = rope_tables(S)
    zeros = jnp.zeros((S, LANES - R), jnp.float32)
    cos_pad = jnp.concatenate([cos, cos, zeros], axis=1)
    sin_pad = jnp.concatenate([-sin, sin, zeros], axis=1)

    tiles_per_seq = S // ROW_TILE
    row = lambda i: (i, 0)
    fixed = lambda i: (0, 0)
    pos = lambda i: (i % tiles_per_seq, 0)
    q_cat, k_cat, v = pl.pallas_call(
        _mla_proj_kernel,
        grid=(T // ROW_TILE,),
        in_specs=[pl.BlockSpec((ROW_TILE, D), row),
                  pl.BlockSpec((1, D), fixed),
                  pl.BlockSpec(w_in_ext.shape, fixed),
                  pl.BlockSpec((1, MLA_Q_LORA), fixed),
                  pl.BlockSpec(w_qb_ext.shape, fixed),
                  pl.BlockSpec((1, MLA_KV_LORA), fixed),
                  pl.BlockSpec(w_k.shape, fixed),
                  pl.BlockSpec(w_v.shape, fixed),
                  pl.BlockSpec((ROW_TILE, LANES), pos),
                  pl.BlockSpec((ROW_TILE, LANES), pos)],
        out_specs=[pl.BlockSpec((ROW_TILE, H * MLA_QK_PAD), row),
                   pl.BlockSpec((ROW_TILE, H * MLA_QK_PAD), row),
                   pl.BlockSpec((ROW_TILE, H * MLA_V_DIM), row)],
        out_shape=[jax.ShapeDtypeStruct((T, H * MLA_QK_PAD), bf),
                   jax.ShapeDtypeStruct((T, H * MLA_QK_PAD), bf),
                   jax.ShapeDtypeStruct((T, H * MLA_V_DIM), bf)],
        name="mla_proj",
    )(h.reshape(T, D), norm_g.reshape(1, D), w_in_ext, q_norm_g.reshape(1, -1), w_qb_ext,
      kv_norm_g.reshape(1, -1), w_k, w_v, cos_pad, sin_pad)

    G = ATTN_HEADS_PER_STEP
    o_attn = pl.pallas_call(
        _mla_attn_kernel,
        grid=(B, H // G, S // ATTN_Q_TILE),
        in_specs=[pl.BlockSpec((1, ATTN_Q_TILE, G * MLA_QK_PAD), lambda b, hd, i: (b, i, hd)),
                  pl.BlockSpec((1, S, G * MLA_QK_PAD), lambda b, hd, i: (b, 0, hd)),
                  pl.BlockSpec((1, S, G * MLA_V_DIM), lambda b, hd, i: (b, 0, hd))],
        out_specs=pl.BlockSpec((1, ATTN_Q_TILE, G * MLA_V_DIM), lambda b, hd, i: (b, i, hd)),
        out_shape=jax.ShapeDtypeStruct((B, S, H * MLA_V_DIM), bf),
        compiler_params=pltpu.CompilerParams(
            dimension_semantics=("parallel", "parallel", "arbitrary"),
            vmem_limit_bytes=VMEM_LIMIT_BYTES),
        name="mla_attention",
    )(q_cat.reshape(B, S, -1), k_cat.reshape(B, S, -1), v.reshape(B, S, -1))

    h_new = matmul_residual(o_attn.reshape(T, H * MLA_V_DIM), w_out.astype(bf), h.reshape(T, D),
                            name="mla_out_proj")
    return h_new.reshape(B, S, D)


def _bdot(a, b):
    return jnp.dot(a.astype(jnp.bfloat16), b.astype(jnp.bfloat16), preferred_element_type=jnp.float32)


def _bdot_nt(a, b):
    return lax.dot_general(a.astype(jnp.bfloat16), b.astype(jnp.bfloat16),
                           (((1,), (1,)), ((), ())), preferred_element_type=jnp.float32)


def _unit_triangular_inverses(Ls, row, col):
    C = Ls[0].shape[0]
    eye = (row == col).astype(jnp.float32)

    def same_block(s):
        return (row // s) == (col // s)

    blk8 = same_block(8)
    L8 = [jnp.where(blk8, L, 0.0) for L in Ls]
    P2 = [_bdot(a, a) for a in L8]
    P4 = [_bdot(a, a) for a in P2]
    X = [eye - a for a in L8]
    X = [x + _bdot(x, p) for x, p in zip(X, P2)]
    X = [x + _bdot(x, p) for x, p in zip(X, P4)]
    s = 8
    while s < C:
        pair = same_block(2 * s) & jnp.logical_not(same_block(s))
        XC = [_bdot(x, jnp.where(pair, L, 0.0)) for x, L in zip(X, Ls)]
        X = [x - _bdot(xc, x) for x, xc in zip(X, XC)]
        s *= 2
    return X


def _gdn_chunk_kernel(qf_ref, kf_ref, vf_ref, gcf_ref, bcf_ref, grf_ref,
                      qb_ref, kb_ref, vb_ref, gcb_ref, bcb_ref, grb_ref,
                      of_ref, ob_ref, state_ref):
    H, C, DK, DV = GDN_HEADS, GDN_CHUNK, GDN_DK, GDN_DV

    @pl.when(pl.program_id(1) == 0)
    def _():
        state_ref[...] = jnp.zeros_like(state_ref)

    row = lax.broadcasted_iota(jnp.int32, (C, C), 0)
    col = lax.broadcasted_iota(jnp.int32, (C, C), 1)
    dirs = ((0, qf_ref, kf_ref, vf_ref, gcf_ref, bcf_ref, grf_ref, of_ref),
            (1, qb_ref, kb_ref, vb_ref, gcb_ref, bcb_ref, grb_ref, ob_ref))
    chains = []
    for d, q_ref, k_ref, v_ref, gc_ref, bc_ref, gr_ref, o_ref in dirs:
        if d == 0:
            incl, strict, last = row >= col, row > col, C - 1
        else:
            incl, strict, last = row <= col, row < col, 0
        tri = incl.astype(jnp.float32)
        gc_cols = jnp.dot(tri, gc_ref[0], precision=lax.Precision.HIGHEST,
                          preferred_element_type=jnp.float32)
        gc_rows = jnp.dot(gr_ref[0, 0], tri.T, precision=lax.Precision.HIGHEST,
                          preferred_element_type=jnp.float32)
        beta_cols = bc_ref[0]
        for h in range(H):
            c = d * H + h
            chains.append(dict(
                c=c, incl=incl, strict=strict, o_ref=o_ref, sl=slice(h * DK, (h + 1) * DK),
                q_ref=q_ref, k_ref=k_ref, v_ref=v_ref,
                beta=beta_cols[:, c:c + 1], gcc=gc_cols[:, c:c + 1], gcr=gc_rows[c:c + 1, :],
                gl=gc_cols[last:last + 1, c:c + 1]))

    for ch in chains:
        ch['decay'] = jnp.where(ch['incl'], jnp.exp(jnp.where(ch['incl'], ch['gcc'] - ch['gcr'], 0.0)), 0.0)
        ch['eg'] = jnp.exp(ch['gcc'])
    for ch in chains:
        k = ch['k_ref'][0, :, ch['sl']]
        q = ch['q_ref'][0, :, ch['sl']]
        kb = k * ch['beta']
        ch['kbeg'] = kb * ch['eg']
        ch['s1'] = _bdot_nt(jnp.concatenate([kb, q], axis=0), k)
    Ls = [jnp.where(ch['strict'], ch['s1'][:C] * ch['decay'], 0.0) for ch in chains]
    Ts = _unit_triangular_inverses(Ls, row, col)
    for ch, T in zip(chains, Ts):
        v = ch['v_ref'][0, :, ch['sl']]
        ch['uw'] = _bdot(T, jnp.concatenate([v * ch['beta'], ch['kbeg']], axis=1))
    for ch in chains:
        q = ch['q_ref'][0, :, ch['sl']]
        w = ch['uw'][:, DV:]
        ch['ws'] = _bdot(jnp.concatenate([w, q * ch['eg']], axis=0), state_ref[ch['c']])
    for ch in chains:
        ch['v_new'] = ch['uw'][:, :DV] - ch['ws'][:C]
        a_intra = ch['s1'][C:] * ch['decay']
        ch['o_ref'][0, :, ch['sl']] = ch['ws'][C:] + _bdot(a_intra, ch['v_new'])
    for ch in chains:
        k = ch['k_ref'][0, :, ch['sl']]
        k_g = k * jnp.exp(ch['gl'] - ch['gcc'])
        c = ch['c']
        state_ref[c] = state_ref[c] * jnp.exp(ch['gl']) + _bdot(k_g.T, ch['v_new'])


def gdn_bidirectional_delta(q, k, v, beta, g):
    B, S, HD = q.shape
    C, H = GDN_CHUNK, GDN_HEADS
    N = S // C
    g_rows = g.reshape(B, N, C, 2 * H).transpose(0, 1, 3, 2)

    def fwd3(b, n):
        return (b, n, 0)

    def bwd3(b, n):
        return (b, N - 1 - n, 0)

    def fwd4(b, n):
        return (b, n, 0, 0)

    def bwd4(b, n):
        return (b, N - 1 - n, 0, 0)

    def specs(i3, i4):
        big = pl.BlockSpec((1, C, HD), i3)
        small = pl.BlockSpec((1, C, 2 * H), i3)
        return [big, big, big, small, small, pl.BlockSpec((1, 1, 2 * H, C), i4)]

    out = jax.ShapeDtypeStruct((B, S, HD), jnp.float32)
    return pl.pallas_call(
        _gdn_chunk_kernel,
        grid=(B, N),
        in_specs=specs(fwd3, fwd4) + specs(bwd3, bwd4),
        out_specs=[pl.BlockSpec((1, C, HD), fwd3), pl.BlockSpec((1, C, HD), bwd3)],
        out_shape=[out, out],
        scratch_shapes=[pltpu.VMEM((2 * H, GDN_DK, GDN_DV), jnp.float32)],
        compiler_params=pltpu.CompilerParams(dimension_semantics=("parallel", "arbitrary")),
        name="gdn_chunk_delta",
    )(q, k, v, g, beta, g_rows, q, k, v, g, beta, g_rows)


GDN_ROW_TILE = 256
HALO = 8


def _gdn_in_kernel(h_ref, hp_ref, hn_ref, g_ref, wqkv_ref, wz_ref, wba_ref, cw_ref, alog_ref, dtb_ref,
                   q_ref, k_ref, v_ref, z_ref, beta_ref, gl_ref, pre_ref, *, tiles_per_seq):
    H, DK = GDN_HEADS, GDN_DK
    TM = h_ref.shape[0]
    i = pl.program_id(0) % tiles_per_seq
    keep_prev = jnp.where(i == 0, 0.0, 1.0)
    keep_next = jnp.where(i == tiles_per_seq - 1, 0.0, 1.0)
    hx = jnp.concatenate([hp_ref[...] * keep_prev, h_ref[...], hn_ref[...] * keep_next], axis=0)
    hn = _rms(hx, g_ref[...]).astype(jnp.bfloat16)
    pre = _bdot(hn, wqkv_ref[...])
    pre_ref[...] = pre
    pad = GDN_CONV // 2
    acc = None
    for j in range(GDN_CONV):
        term = pre_ref[pl.ds(HALO + j - pad, TM), :] * cw_ref[j:j + 1, :]
        acc = term if acc is None else acc + term
    qkv = _silu(acc)
    for hd in range(H):
        for off, ref, scale in ((0, q_ref, DK ** -0.5), (H * DK, k_ref, 1.0)):
            x = qkv[:, off + hd * DK:off + (hd + 1) * DK]
            ref[:, hd * DK:(hd + 1) * DK] = x * (lax.rsqrt(jnp.sum(x * x, axis=-1, keepdims=True) + NORM_EPS) * scale)
    v_ref[...] = qkv[:, 2 * H * DK:]
    hc = hn[HALO:HALO + TM]
    z_ref[...] = _bdot(hc, wz_ref[...])
    ba = _bdot(hc, wba_ref[...])
    beta_ref[...] = _sigmoid(ba[:, :2 * H])
    a = ba[:, 2 * H:] + dtb_ref[...]
    softplus = jnp.maximum(a, 0.0) + jnp.log(1.0 + jnp.exp(-jnp.abs(a)))
    gl_ref[...] = -jnp.exp(alog_ref[...]) * softplus


def _gdn_out_kernel(of_ref, ob_ref, z_ref, g_ref, w_ref, h_ref, o_ref):
    H, DV = GDN_HEADS, GDN_DV
    o = of_ref[...] + ob_ref[...]
    g = g_ref[...]
    parts = []
    for hd in range(H):
        x = o[:, hd * DV:(hd + 1) * DV]
        parts.append(_rms(x, g))
    y = jnp.concatenate(parts, axis=1) * _silu(z_ref[...])
    o_ref[...] = h_ref[...] + _bdot(y, w_ref[...])


def gdn_layer(h, norm_g, w_in, conv_w, A_log, dt_bias, out_norm_g, w_out):
    B, S, D = h.shape
    T = B * S
    H = GDN_HEADS
    HD = H * GDN_DK
    bf = jnp.bfloat16
    TM = GDN_ROW_TILE
    tiles_per_seq = S // TM
    hb = TM // HALO
    n_halo = T // HALO
    o = GDN_QKV_DIM + H * GDN_DV
    w_qkv = w_in[:, :GDN_QKV_DIM].astype(bf)
    w_z = w_in[:, GDN_QKV_DIM:o].astype(bf)
    w_ba = w_in[:, o:].astype(bf)
    cw = jnp.concatenate([conv_w, jnp.zeros((HALO - GDN_CONV, GDN_QKV_DIM), conv_w.dtype)], axis=0)
    row = lambda i: (i, 0)
    fixed = lambda i: (0, 0)
    f32 = jnp.float32
    ht = h.reshape(T, D)
    q, k, v, z, beta, gl = pl.pallas_call(
        functools.partial(_gdn_in_kernel, tiles_per_seq=tiles_per_seq),
        grid=(T // TM,),
        in_specs=[pl.BlockSpec((TM, D), row),
                  pl.BlockSpec((HALO, D), lambda i: (jnp.maximum(i * hb - 1, 0), 0)),
                  pl.BlockSpec((HALO, D), lambda i: (jnp.minimum((i + 1) * hb, n_halo - 1), 0)),
                  pl.BlockSpec((1, D), fixed),
                  pl.BlockSpec(w_qkv.shape, fixed), pl.BlockSpec(w_z.shape, fixed),
                  pl.BlockSpec(w_ba.shape, fixed), pl.BlockSpec(cw.shape, fixed),
                  pl.BlockSpec((1, 2 * H), fixed), pl.BlockSpec((1, 2 * H), fixed)],
        out_specs=[pl.BlockSpec((TM, HD), row), pl.BlockSpec((TM, HD), row), pl.BlockSpec((TM, HD), row),
                   pl.BlockSpec((TM, HD), row), pl.BlockSpec((TM, 2 * H), row), pl.BlockSpec((TM, 2 * H), row)],
        out_shape=[jax.ShapeDtypeStruct((T, HD), f32)] * 4 + [jax.ShapeDtypeStruct((T, 2 * H), f32)] * 2,
        scratch_shapes=[pltpu.VMEM((TM + 2 * HALO, GDN_QKV_DIM), f32)],
        compiler_params=pltpu.CompilerParams(vmem_limit_bytes=VMEM_LIMIT_BYTES),
        name="gdn_in_proj",
    )(ht, ht, ht, norm_g.reshape(1, D), w_qkv, w_z, w_ba, cw,
      A_log.reshape(1, 2 * H).astype(f32), dt_bias.reshape(1, 2 * H).astype(f32))

    o_fwd, o_bwd = gdn_bidirectional_delta(q.reshape(B, S, HD), k.reshape(B, S, HD), v.reshape(B, S, HD),
                                           beta.reshape(B, S, 2 * H), gl.reshape(B, S, 2 * H))

    h_new = pl.pallas_call(
        _gdn_out_kernel,
        grid=(T // ROW_TILE,),
        in_specs=[pl.BlockSpec((ROW_TILE, HD), row)] * 3 +
                 [pl.BlockSpec((1, GDN_DV), fixed), pl.BlockSpec((HD, D), fixed), pl.BlockSpec((ROW_TILE, D), row)],
        out_specs=pl.BlockSpec((ROW_TILE, D), row),
        out_shape=jax.ShapeDtypeStruct((T, D), f32),
        name="gdn_out_proj",
    )(o_fwd.reshape(T, HD), o_bwd.reshape(T, HD), z, out_norm_g.reshape(1, GDN_DV), w_out.astype(bf), ht)
    return h_new.reshape(B, S, D)


def _sigmoid(x):
    return 0.5 * jnp.tanh(0.5 * x) + 0.5


def _silu(x):
    return x * _sigmoid(x)


def _moe_router_kernel(h_ref, g_ref, w_ref, xn_ref, aff_ref):
    xn = _rms(h_ref[...], g_ref[...])
    xn_ref[...] = xn.astype(xn_ref.dtype)
    logits = _bdot(xn, w_ref[...])
    e = jnp.exp(logits - jnp.max(logits, axis=-1, keepdims=True))
    aff_ref[...] = e / jnp.sum(e, axis=-1, keepdims=True)


def _moe_ffn_kernel(x_ref, wg_ref, wu_ref, wd_ref, gate_ref, y_ref):
    x = x_ref[0]
    hid = _silu(_bdot(x, wg_ref[0])) * _bdot(x, wu_ref[0])
    y_ref[0] = (_bdot(hid, wd_ref[0]) * gate_ref[0]).astype(y_ref.dtype)


COMBINE_TILE = 256
COMBINE_SUBTILES = 2
STRIP_ROWS = 64
STRIP_ALIGN = 16
STRIPS_PER_DOT = 4


def _moe_combine_ple_kernel(base_ref, off_ref, cnt_ref,
                            h_ref, p_ref, g_ref, wg_ref, wp_ref, fg_ref, ye_ref, tok_ref,
                            o_ref, ybuf, tbuf, sems, yextra, textra, extra_sems, acc_ref,
                            *, n_tiles, total_rows, final_norm):
    E, TB, R, U = N_EXPERTS, COMBINE_TILE, STRIP_ROWS, COMBINE_SUBTILES
    K = STRIPS_PER_DOT * R
    n_steps = n_tiles // U
    j = pl.program_id(0)
    cur = j % 2

    def strip_copies(step, u, e, b):
        start = pl.multiple_of(base_ref[e * n_tiles + step * U + u], STRIP_ALIGN)
        rows = pl.ds((u * E + e) * R, R)
        return (pltpu.make_async_copy(ye_ref.at[pl.ds(start, R), :], ybuf.at[b, rows, :],
                                      sems.at[0, b, u * E + e]),
                pltpu.make_async_copy(tok_ref.at[pl.ds(start, R), :], tbuf.at[b, rows, :],
                                      sems.at[1, b, u * E + e]))

    def all_strip_copies(step, b):
        return [cp for u in range(U) for e in range(E) for cp in strip_copies(step, u, e, b)]

    @pl.when(j == 0)
    def _():
        for cp in all_strip_copies(0, 0):
            cp.start()

    @pl.when(j + 1 < n_steps)
    def _():
        for cp in all_strip_copies(j + 1, 1 - cur):
            cp.start()

    for cp in all_strip_copies(j, cur):
        cp.wait()

    lane = lax.broadcasted_iota(jnp.int32, (K, TB), 1)
    srow = lax.broadcasted_iota(jnp.int32, (R, 1), 0)
    groups = [(u, grp) for u in range(U) for grp in range(E // STRIPS_PER_DOT)]
    hots = []
    for u, grp in groups:
        tile = j * U + u
        wanted = []
        for e in range(grp * STRIPS_PER_DOT, (grp + 1) * STRIPS_PER_DOT):
            lo = off_ref[e * n_tiles + tile]
            wanted.append((srow >= lo) & (srow < lo + cnt_ref[e * n_tiles + tile]))
        rows = pl.ds((u * E + grp * STRIPS_PER_DOT) * R, K)
        tok = tbuf[cur, rows, 0:1] - tile * TB
        hots.append(jnp.where(jnp.concatenate(wanted, axis=0) & (lane == tok), 1.0, 0.0))
    parts = [_bdot(hot.T, ybuf[cur, pl.ds((u * E + grp * STRIPS_PER_DOT) * R, K), :])
             for hot, (u, grp) in zip(hots, groups)]
    for u in range(U):
        mine = [part for part, (uu, _) in zip(parts, groups) if uu == u]
        acc_ref[u] = functools.reduce(lambda a, b: a + b, mine)

    lane_r = lax.broadcasted_iota(jnp.int32, (R, TB), 1)
    for u in range(U):
        tile = j * U + u
        for e in range(E):
            base = base_ref[e * n_tiles + tile]
            first = base + off_ref[e * n_tiles + tile]
            end = first + cnt_ref[e * n_tiles + tile]
            n_sub = (end - base + R - 1) // R

            def body(k, carry, u=u, tile=tile, base=base, first=first, end=end):
                want0 = base + k * R
                start = pl.multiple_of(jnp.minimum(want0, total_rows - R), STRIP_ALIGN)
                cy = pltpu.make_async_copy(ye_ref.at[pl.ds(start, R), :], yextra, extra_sems.at[0])
                ct = pltpu.make_async_copy(tok_ref.at[pl.ds(start, R), :], textra, extra_sems.at[1])
                cy.start()
                ct.start()
                cy.wait()
                ct.wait()
                grow = srow + start
                wanted = (grow >= jnp.maximum(want0, first)) & (grow < jnp.minimum(want0 + R, end))
                hot_t = jnp.where(wanted & (lane_r == textra[:, 0:1] - tile * TB), 1.0, 0.0)
                acc_ref[u] += _bdot(hot_t.T, yextra[...])
                return carry

            lax.fori_loop(1, n_sub, body, 0)

    tiles = [pl.ds(u * TB, TB) for u in range(U)]
    hm = [h_ref[t, :] + acc_ref[u] for u, t in enumerate(tiles)]
    gate = [_sigmoid(_bdot(_rms(x, g_ref[...]), wg_ref[...])) for x in hm]
    out = [x + _bdot(p_ref[t, :], wp_ref[...]) * gt for x, gt, t in zip(hm, gate, tiles)]
    for x, t in zip(out, tiles):
        o_ref[t, :] = _rms(x, fg_ref[...]) if final_norm else x


F32_INF_BITS = 0x7F800000


def _topk_select_kernel(aff_ref, wcum_ref, *, cap):
    a = aff_ref[0]
    R = a.shape[0]
    bits = lax.bitcast_convert_type(a, jnp.int32)

    def count(mask):
        m = mask.astype(jnp.float32)
        return jnp.sum(jnp.sum(m, axis=0, keepdims=True), axis=1, keepdims=True)

    def bisect(_, carry):
        lo, hi = carry
        mid = lo + jnp.right_shift(hi - lo + 1, 1)
        ok = count(bits >= mid) >= cap
        return jnp.where(ok, mid, lo), jnp.where(ok, hi, mid - 1)

    thr, _ = lax.fori_loop(0, 31, bisect, (jnp.zeros((1, 1), jnp.int32),
                                           jnp.full((1, 1), F32_INF_BITS, jnp.int32)))
    gt = bits > thr
    eq = bits == thr
    need = cap - count(gt)

    lane_r = lax.broadcasted_iota(jnp.int32, (LANES, LANES), 0)
    lane_c = lax.broadcasted_iota(jnp.int32, (LANES, LANES), 1)
    upper = (lane_r <= lane_c).astype(jnp.bfloat16)
    row_r = lax.broadcasted_iota(jnp.int32, (R, R), 0)
    row_c = lax.broadcasted_iota(jnp.int32, (R, R), 1)
    earlier_rows = (row_r > row_c).astype(jnp.bfloat16)

    eq_in_row = jnp.dot(eq.astype(jnp.bfloat16), upper, preferred_element_type=jnp.float32)
    eq_row_total = jnp.broadcast_to(eq_in_row[:, LANES - 1:LANES], (R, LANES))
    eq_before = jnp.dot(earlier_rows, eq_row_total.astype(jnp.bfloat16), preferred_element_type=jnp.float32)
    sel = gt | (eq & (eq_before + eq_in_row <= need))
    wcum = jnp.dot(sel.astype(jnp.bfloat16), upper, preferred_element_type=jnp.float32)
    wcum_ref[0] = wcum.astype(jnp.int32)


SLOT_TILE = 512


def _slot_token_kernel(row_end_ref, wcum_ref, tok_ref):
    SB = tok_ref.shape[1]
    row_end = row_end_ref[0]
    R = row_end.shape[1]
    slot = pl.program_id(1) * SB + lax.broadcasted_iota(jnp.int32, (SB, 1), 0)
    done = row_end <= slot
    row = jnp.sum(done.astype(jnp.int32), axis=1, keepdims=True)
    row_start = jnp.max(jnp.where(done, row_end, 0), axis=1, keepdims=True)
    one_hot = (lax.broadcasted_iota(jnp.int32, (SB, R), 1) == row).astype(jnp.bfloat16)
    wrow = jnp.dot(one_hot, wcum_ref[0].astype(jnp.float32).astype(jnp.bfloat16),
                   preferred_element_type=jnp.float32)
    local = (slot - row_start).astype(jnp.float32)
    lane = jnp.sum((wrow <= local).astype(jnp.int32), axis=1, keepdims=True)
    tok_ref[0] = jnp.broadcast_to(row * LANES + lane, (SB, LANES))


def expert_choice_indices(aff_t, cap):
    E, T = aff_t.shape
    R = T // LANES
    wcum = pl.pallas_call(
        functools.partial(_topk_select_kernel, cap=cap),
        grid=(E,),
        in_specs=[pl.BlockSpec((1, R, LANES), lambda e: (e, 0, 0))],
        out_specs=pl.BlockSpec((1, R, LANES), lambda e: (e, 0, 0)),
        out_shape=jax.ShapeDtypeStruct((E, R, LANES), jnp.int32),
        name="moe_topk_select",
    )(aff_t.reshape(E, R, LANES))
    row_end = jnp.cumsum(wcum[:, :, LANES - 1], axis=1)
    row_start = row_end - wcum[:, :, LANES - 1]
    tok = pl.pallas_call(
        _slot_token_kernel,
        grid=(E, cap // SLOT_TILE),
        in_specs=[pl.BlockSpec((1, 1, R), lambda e, j: (e, 0, 0)),
                  pl.BlockSpec((1, R, LANES), lambda e, j: (e, 0, 0))],
        out_specs=pl.BlockSpec((1, SLOT_TILE, LANES), lambda e, j: (e, j, 0)),
        out_shape=jax.ShapeDtypeStruct((E, cap, LANES), jnp.int32),
        name="moe_slot_token",
    )(row_end.reshape(E, 1, R), wcum)
    return tok, row_start


def moe_ple_layer(h, p, norm_g, w_router, w_gate, w_up, w_down, ple_norm_g, ple_w_gate, ple_w_proj,
                  final_g, final_norm):
    T, D = h.shape
    E, F = N_EXPERTS, w_gate.shape[-1]
    cap = EC_CAPACITY_FACTOR * T // E
    bf = jnp.bfloat16
    row = lambda i: (i, 0)
    fixed = lambda i: (0, 0)
    xn, aff = pl.pallas_call(
        _moe_router_kernel,
        grid=(T // ROW_TILE,),
        in_specs=[pl.BlockSpec((ROW_TILE, D), row), pl.BlockSpec((1, D), fixed),
                  pl.BlockSpec((D, E), fixed)],
        out_specs=[pl.BlockSpec((ROW_TILE, D), row), pl.BlockSpec((ROW_TILE, E), row)],
        out_shape=[jax.ShapeDtypeStruct((T, D), bf), jax.ShapeDtypeStruct((T, E), jnp.float32)],
        name="moe_router",
    )(h, norm_g.reshape(1, D), w_router.astype(bf))
    aff_t = aff.T
    tok, row_start = expert_choice_indices(aff_t, cap)
    idx = tok[:, :, 0]
    gate = jnp.take_along_axis(aff_t, idx, axis=1)
    xe = xn.at[idx].get(mode='promise_in_bounds')
    ye = pl.pallas_call(
        _moe_ffn_kernel,
        grid=(E, cap // ROW_TILE),
        in_specs=[pl.BlockSpec((1, ROW_TILE, D), lambda e, i: (e, i, 0)),
                  pl.BlockSpec((1, D, F), lambda e, i: (e, 0, 0)),
                  pl.BlockSpec((1, D, F), lambda e, i: (e, 0, 0)),
                  pl.BlockSpec((1, F, D), lambda e, i: (e, 0, 0)),
                  pl.BlockSpec((1, ROW_TILE, 1), lambda e, i: (e, i, 0))],
        out_specs=pl.BlockSpec((1, ROW_TILE, D), lambda e, i: (e, i, 0)),
        out_shape=jax.ShapeDtypeStruct((E, cap, D), bf),
        compiler_params=pltpu.CompilerParams(dimension_semantics=("parallel", "arbitrary")),
        name="moe_expert_ffn",
    )(xe, w_gate.astype(bf), w_up.astype(bf), w_down.astype(bf), gate[..., None])

    TB, R, U = COMBINE_TILE, STRIP_ROWS, COMBINE_SUBTILES
    n_tiles = T // TB
    total_rows = E * cap
    lo = jnp.concatenate([row_start[:, ::TB // LANES], jnp.full((E, 1), cap, jnp.int32)], axis=1)
    first = lo[:, :-1] + (jnp.arange(E, dtype=jnp.int32) * cap)[:, None]
    cnt = lo[:, 1:] - lo[:, :-1]
    base = jnp.minimum(first // STRIP_ALIGN * STRIP_ALIGN, total_rows - R)
    off = first - base

    P = p.shape[1]
    tile = lambda i, *_: (i, 0)
    const = lambda i, *_: (0, 0)
    return pl.pallas_call(
        functools.partial(_moe_combine_ple_kernel, n_tiles=n_tiles, total_rows=total_rows,
                          final_norm=final_norm),
        grid_spec=pltpu.PrefetchScalarGridSpec(
            num_scalar_prefetch=3,
            grid=(n_tiles // U,),
            in_specs=[pl.BlockSpec((U * TB, D), tile), pl.BlockSpec((U * TB, P), tile),
                      pl.BlockSpec((1, D), const), pl.BlockSpec((D, D), const), pl.BlockSpec((P, D), const),
                      pl.BlockSpec((1, D), const),
                      pl.BlockSpec(memory_space=pl.ANY), pl.BlockSpec(memory_space=pl.ANY)],
            out_specs=pl.BlockSpec((U * TB, D), tile),
            scratch_shapes=[pltpu.VMEM((2, U * E * R, D), bf), pltpu.VMEM((2, U * E * R, LANES), jnp.int32),
                            pltpu.SemaphoreType.DMA((2, 2, U * E)),
                            pltpu.VMEM((R, D), bf), pltpu.VMEM((R, LANES), jnp.int32),
                            pltpu.SemaphoreType.DMA((2,)),
                            pltpu.VMEM((U, TB, D), jnp.float32)]),
        out_shape=jax.ShapeDtypeStruct((T, D), jnp.float32),
        compiler_params=pltpu.CompilerParams(dimension_semantics=("arbitrary",),
                                             vmem_limit_bytes=VMEM_LIMIT_BYTES),
        name="moe_combine_ple",
    )(base.reshape(-1), off.reshape(-1), cnt.reshape(-1),
      h, p, ple_norm_g.reshape(1, D), ple_w_gate.astype(bf), ple_w_proj.astype(bf), final_g.reshape(1, D),
      ye.reshape(total_rows, D), tok.reshape(total_rows, LANES))


def encoder_trunk(x, p, W):
    h = x
    for i in range(DEPTH):
        j = i // N_MIXERS
        if i % N_MIXERS == 0:
            h = mla_layer(h, W['norm_mix_g'][i], W['mla_w_in'][j], W['mla_q_norm_g'][j], W['mla_w_qb'][j],
                          W['mla_kv_norm_g'][j], W['mla_w_kvb'][j], W['mla_w_out'][j])
        else:
            h = gdn_layer(h, W['norm_mix_g'][i], W['gdn_w_in'][j], W['gdn_conv_w'][j], W['gdn_A_log'][j],
                          W['gdn_dt_bias'][j], W['gdn_norm_g'][j], W['gdn_w_out'][j])
        B, S, D = h.shape
        ht = moe_ple_layer(h.reshape(B * S, D), p[i].reshape(B * S, PLE_DIM), W['norm_ffn_g'][i],
                           W['moe_w_router'][i], W['moe_w_gate'][i], W['moe_w_up'][i], W['moe_w_down'][i],
                           W['norm_ple_g'][i], W['ple_w_gate'][i], W['ple_w_proj'][i],
                           W['final_norm_g'], final_norm=(i == DEPTH - 1))
        h = ht.reshape(B, S, D)
    return h


def kernel(x_prompt, x_sample, p_prompt, p_sample, norm_mix_g, norm_ffn_g, norm_ple_g, final_norm_g,
           mla_w_in, mla_q_norm_g, mla_w_qb, mla_kv_norm_g, mla_w_kvb, mla_w_out,
           gdn_w_in, gdn_conv_w, gdn_A_log, gdn_dt_bias, gdn_norm_g, gdn_w_out,
           moe_w_router, moe_w_gate, moe_w_up, moe_w_down, ple_w_proj, ple_w_gate):
    W = {
        'norm_mix_g': norm_mix_g, 'norm_ffn_g': norm_ffn_g, 'norm_ple_g': norm_ple_g,
        'final_norm_g': final_norm_g,
        'mla_w_in': mla_w_in, 'mla_q_norm_g': mla_q_norm_g, 'mla_w_qb': mla_w_qb,
        'mla_kv_norm_g': mla_kv_norm_g, 'mla_w_kvb': mla_w_kvb, 'mla_w_out': mla_w_out,
        'gdn_w_in': gdn_w_in, 'gdn_conv_w': gdn_conv_w, 'gdn_A_log': gdn_A_log,
        'gdn_dt_bias': gdn_dt_bias, 'gdn_norm_g': gdn_norm_g, 'gdn_w_out': gdn_w_out,
        'moe_w_router': moe_w_router, 'moe_w_gate': moe_w_gate, 'moe_w_up': moe_w_up,
        'moe_w_down': moe_w_down, 'ple_w_proj': ple_w_proj, 'ple_w_gate': ple_w_gate,
    }
    y_prompt = encoder_trunk(x_prompt, p_prompt, W)
    y_sample = encoder_trunk(x_sample, p_sample, W)
    return (y_prompt, y_sample)
```

```python
import functools

import jax
import jax.numpy as jnp
from jax import lax
from jax.experimental import pallas as pl
from jax.experimental.pallas import tpu as pltpu

D_MODEL = 1024
DEPTH = 4
PLE_DIM = 256
N_MIXERS = 2
MLA_HEADS = 8
MLA_NOPE_DIM = 128
MLA_ROPE_DIM = 64
MLA_V_DIM = 128
MLA_Q_LORA = 256
MLA_KV_LORA = 128
ROPE_THETA = 10000.0
Q_BLOCK = 128
GDN_HEADS = 8
GDN_DK = 128
GDN_DV = 128
GDN_CONV = 5
GDN_CHUNK = 64
GDN_QKV_DIM = 2 * GDN_HEADS * GDN_DK + GDN_HEADS * GDN_DV
N_EXPERTS = 16
EC_CAPACITY_FACTOR = 2
NORM_EPS = 1e-6


def rope_tables(seq):
    pos = jnp.arange(seq, dtype=jnp.float32)
    inv = ROPE_THETA ** (-jnp.arange(0, MLA_ROPE_DIM, 2, dtype=jnp.float32) / MLA_ROPE_DIM)
    ang = pos[:, None] * inv[None, :]
    return jnp.cos(ang), jnp.sin(ang)


LANES = 128
ROW_TILE = 512
FFN_ROW_TILE = 1024
ATTN_Q_TILE = 512
ATTN_HEADS_PER_STEP = 4
VMEM_LIMIT_BYTES = 56 * 1024 * 1024
MLA_QK_PAD = 2 * LANES


def _rms(x, g):
    return x * lax.rsqrt(jnp.mean(x * x, axis=-1, keepdims=True) + NORM_EPS) * g


def _mla_proj_kernel(h_ref, g_ref, w_in_ref, qg_ref, w_qb_ref, kvg_ref, w_k_ref, w_v_ref,
                     cos_ref, sin_ref, q_ref, k_ref, v_ref):
    H, NP = MLA_HEADS, MLA_NOPE_DIM
    scale = (MLA_NOPE_DIM + MLA_ROPE_DIM) ** -0.5
    hn = _rms(h_ref[...], g_ref[...])
    lat = _bdot(hn, w_in_ref[...])
    q_lat = lat[:, :MLA_Q_LORA]
    kv_lat = lat[:, MLA_Q_LORA:MLA_Q_LORA + MLA_KV_LORA]
    o = MLA_Q_LORA + MLA_KV_LORA
    cos, sin = cos_ref[...], sin_ref[...]
    k_rope = (lat[:, o:o + LANES] * cos + lat[:, o + LANES:o + 2 * LANES] * sin).astype(jnp.bfloat16)
    q = _bdot(_rms(q_lat, qg_ref[...]), w_qb_ref[...]) * scale
    kvn = _rms(kv_lat, kvg_ref[...])
    k_nope = _bdot(kvn, w_k_ref[...]).astype(jnp.bfloat16)
    v_ref[...] = _bdot(kvn, w_v_ref[...]).astype(jnp.bfloat16)
    for hd in range(H):
        a = q[:, (H + hd) * NP:(H + hd + 1) * NP]
        b = q[:, (2 * H + hd) * NP:(2 * H + hd + 1) * NP]
        q_ref[:, hd * MLA_QK_PAD:hd * MLA_QK_PAD + NP] = q[:, hd * NP:(hd + 1) * NP].astype(jnp.bfloat16)
        q_ref[:, hd * MLA_QK_PAD + NP:(hd + 1) * MLA_QK_PAD] = (a * cos + b * sin).astype(jnp.bfloat16)
        k_ref[:, hd * MLA_QK_PAD:hd * MLA_QK_PAD + NP] = k_nope[:, hd * NP:(hd + 1) * NP]
        k_ref[:, hd * MLA_QK_PAD + NP:(hd + 1) * MLA_QK_PAD] = k_rope


def _mla_attn_kernel(q_ref, k_ref, vt_ref, o_ref):
    heads = range(ATTN_HEADS_PER_STEP)
    qk = lambda hd: slice(hd * MLA_QK_PAD, (hd + 1) * MLA_QK_PAD)
    vv = lambda hd: slice(hd * MLA_V_DIM, (hd + 1) * MLA_V_DIM)
    s_t = [_bdot_nt(k_ref[0, :, qk(hd)], q_ref[0, :, qk(hd)]) for hd in heads]
    p_t = [jnp.exp(x - jnp.max(x, axis=0, keepdims=True)) for x in s_t]
    denom = [jnp.sum(x, axis=0, keepdims=True) for x in p_t]
    o_t = [_bdot(vt_ref[0, vv(hd), :], p_t[hd]) / denom[hd] for hd in heads]
    for hd in heads:
        o_ref[0, :, vv(hd)] = o_t[hd].T.astype(o_ref.dtype)


def _matmul_residual_kernel(x_ref, w_ref, h_ref, o_ref):
    o_ref[...] = h_ref[...] + _bdot(x_ref[...], w_ref[...])


def matmul_residual(x, w, h, name):
    T, K = x.shape
    D = w.shape[1]
    return pl.pallas_call(
        _matmul_residual_kernel,
        grid=(T // ROW_TILE,),
        in_specs=[pl.BlockSpec((ROW_TILE, K), lambda i: (i, 0)),
                  pl.BlockSpec((K, D), lambda i: (0, 0)),
                  pl.BlockSpec((ROW_TILE, D), lambda i: (i, 0))],
        out_specs=pl.BlockSpec((ROW_TILE, D), lambda i: (i, 0)),
        out_shape=jax.ShapeDtypeStruct((T, D), jnp.float32),
        name=name,
    )(x, w, h)


def _rope_swapped_columns(w):
    half = MLA_ROPE_DIM // 2
    pad = jnp.zeros(w.shape[:-1] + (LANES - MLA_ROPE_DIM,), w.dtype)
    plain = jnp.concatenate([w, pad], axis=-1)
    swapped = jnp.concatenate([w[..., half:], w[..., :half], pad], axis=-1)
    return plain, swapped


def mla_layer(h, norm_g, w_in, q_norm_g, w_qb, kv_norm_g, w_kvb, w_out):
    B, S, D = h.shape
    T = B * S
    H, NP, R = MLA_HEADS, MLA_NOPE_DIM, MLA_ROPE_DIM
    bf = jnp.bfloat16
    o = MLA_Q_LORA + MLA_KV_LORA
    kr_plain, kr_swapped = _rope_swapped_columns(w_in[:, o:])
    w_in_ext = jnp.concatenate([w_in[:, :o], kr_plain, kr_swapped], axis=1).astype(bf)
    w_qb_h = w_qb.reshape(MLA_Q_LORA, H, NP + R)
    qr_plain, qr_swapped = _rope_swapped_columns(w_qb_h[:, :, NP:])
    w_qb_ext = jnp.concatenate([w_qb_h[:, :, :NP].reshape(MLA_Q_LORA, H * NP),
                                qr_plain.reshape(MLA_Q_LORA, H * LANES),
                                qr_swapped.reshape(MLA_Q_LORA, H * LANES)], axis=1).astype(bf)
    w_kvb_h = w_kvb.reshape(MLA_KV_LORA, H, NP + MLA_V_DIM)
    w_k = w_kvb_h[:, :, :NP].reshape(MLA_KV_LORA, H * NP).astype(bf)
    w_v = w_kvb_h[:, :, NP:].reshape(MLA_KV_LORA, H * MLA_V_DIM).astype(bf)
    cos, sin = rope_tables(S)
    zeros = jnp.zeros((S, LANES - R), jnp.float32)
    cos_pad = jnp.concatenate([cos, cos, zeros], axis=1)
    sin_pad = jnp.concatenate([-sin, sin, zeros], axis=1)

    tiles_per_seq = S // ROW_TILE
    row = lambda i: (i, 0)
    fixed = lambda i: (0, 0)
    pos = lambda i: (i % tiles_per_seq, 0)
    q_cat, k_cat, v = pl.pallas_call(
        _mla_proj_kernel,
        grid=(T // ROW_TILE,),
        in_specs=[pl.BlockSpec((ROW_TILE, D), row),
                  pl.BlockSpec((1, D), fixed),
                  pl.BlockSpec(w_in_ext.shape, fixed),
                  pl.BlockSpec((1, MLA_Q_LORA), fixed),
                  pl.BlockSpec(w_qb_ext.shape, fixed),
                  pl.BlockSpec((1, MLA_KV_LORA), fixed),
                  pl.BlockSpec(w_k.shape, fixed),
                  pl.BlockSpec(w_v.shape, fixed),
                  pl.BlockSpec((ROW_TILE, LANES), pos),
                  pl.BlockSpec((ROW_TILE, LANES), pos)],
        out_specs=[pl.BlockSpec((ROW_TILE, H * MLA_QK_PAD), row),
                   pl.BlockSpec((ROW_TILE, H * MLA_QK_PAD), row),
                   pl.BlockSpec((ROW_TILE, H * MLA_V_DIM), row)],
        out_shape=[jax.ShapeDtypeStruct((T, H * MLA_QK_PAD), bf),
                   jax.ShapeDtypeStruct((T, H * MLA_QK_PAD), bf),
                   jax.ShapeDtypeStruct((T, H * MLA_V_DIM), bf)],
        name="mla_proj",
    )(h.reshape(T, D), norm_g.reshape(1, D), w_in_ext, q_norm_g.reshape(1, -1), w_qb_ext,
      kv_norm_g.reshape(1, -1), w_k, w_v, cos_pad, sin_pad)

    G = ATTN_HEADS_PER_STEP
    o_attn = pl.pallas_call(
        _mla_attn_kernel,
        grid=(B, H // G, S // ATTN_Q_TILE),
        in_specs=[pl.BlockSpec((1, ATTN_Q_TILE, G * MLA_QK_PAD), lambda b, hd, i: (b, i, hd)),
                  pl.BlockSpec((1, S, G * MLA_QK_PAD), lambda b, hd, i: (b, 0, hd)),
                  pl.BlockSpec((1, G * MLA_V_DIM, S), lambda b, hd, i: (b, hd, 0))],
        out_specs=pl.BlockSpec((1, ATTN_Q_TILE, G * MLA_V_DIM), lambda b, hd, i: (b, i, hd)),
        out_shape=jax.ShapeDtypeStruct((B, S, H * MLA_V_DIM), bf),
        compiler_params=pltpu.CompilerParams(
            dimension_semantics=("parallel", "parallel", "arbitrary"),
            vmem_limit_bytes=VMEM_LIMIT_BYTES),
        name="mla_attention",
    )(q_cat.reshape(B, S, -1), k_cat.reshape(B, S, -1), v.reshape(B, S, -1).transpose(0, 2, 1))

    h_new = matmul_residual(o_attn.reshape(T, H * MLA_V_DIM), w_out.astype(bf), h.reshape(T, D),
                            name="mla_out_proj")
    return h_new.reshape(B, S, D)


def _bdot(a, b):
    return jnp.dot(a.astype(jnp.bfloat16), b.astype(jnp.bfloat16), preferred_element_type=jnp.float32)


def _bdot_nt(a, b):
    return lax.dot_general(a.astype(jnp.bfloat16), b.astype(jnp.bfloat16),
                           (((1,), (1,)), ((), ())), preferred_element_type=jnp.float32)


def _unit_triangular_inverses(Ls, row, col):
    C = Ls[0].shape[0]
    eye = (row == col).astype(jnp.float32)

    def same_block(s):
        return (row // s) == (col // s)

    blk8 = same_block(8)
    L8 = [jnp.where(blk8, L, 0.0) for L in Ls]
    P2 = [_bdot(a, a) for a in L8]
    P4 = [_bdot(a, a) for a in P2]
    X = [eye - a for a in L8]
    X = [x + _bdot(x, p) for x, p in zip(X, P2)]
    X = [x + _bdot(x, p) for x, p in zip(X, P4)]
    s = 8
    while s < C:
        pair = same_block(2 * s) & jnp.logical_not(same_block(s))
        XC = [_bdot(x, jnp.where(pair, L, 0.0)) for x, L in zip(X, Ls)]
        X = [x - _bdot(xc, x) for x, xc in zip(X, XC)]
        s *= 2
    return X


def _gdn_chunk_kernel(qf_ref, kf_ref, vf_ref, gcf_ref, bcf_ref, grf_ref,
                      qb_ref, kb_ref, vb_ref, gcb_ref, bcb_ref, grb_ref,
                      of_ref, ob_ref, state_ref):
    H, C, DK, DV = GDN_HEADS, GDN_CHUNK, GDN_DK, GDN_DV

    @pl.when(pl.program_id(1) == 0)
    def _():
        state_ref[...] = jnp.zeros_like(state_ref)

    row = lax.broadcasted_iota(jnp.int32, (C, C), 0)
    col = lax.broadcasted_iota(jnp.int32, (C, C), 1)
    dirs = ((0, qf_ref, kf_ref, vf_ref, gcf_ref, bcf_ref, grf_ref, of_ref),
            (1, qb_ref, kb_ref, vb_ref, gcb_ref, bcb_ref, grb_ref, ob_ref))
    chains = []
    for d, q_ref, k_ref, v_ref, gc_ref, bc_ref, gr_ref, o_ref in dirs:
        if d == 0:
            incl, strict, last = row >= col, row > col, C - 1
        else:
            incl, strict, last = row <= col, row < col, 0
        tri = incl.astype(jnp.float32)
        gc_cols = jnp.dot(tri, gc_ref[0], precision=lax.Precision.HIGHEST,
                          preferred_element_type=jnp.float32)
        gc_rows = jnp.dot(gr_ref[0, 0], tri.T, precision=lax.Precision.HIGHEST,
                          preferred_element_type=jnp.float32)
        beta_cols = bc_ref[0]
        for h in range(H):
            c = d * H + h
            chains.append(dict(
                c=c, incl=incl, strict=strict, o_ref=o_ref, sl=slice(h * DK, (h + 1) * DK),
                q_ref=q_ref, k_ref=k_ref, v_ref=v_ref,
                beta=beta_cols[:, c:c + 1], gcc=gc_cols[:, c:c + 1], gcr=gc_rows[c:c + 1, :],
                gl=gc_cols[last:last + 1, c:c + 1]))

    for ch in chains:
        ch['decay'] = jnp.where(ch['incl'], jnp.exp(jnp.where(ch['incl'], ch['gcc'] - ch['gcr'], 0.0)), 0.0)
        ch['eg'] = jnp.exp(ch['gcc'])
    for ch in chains:
        k = ch['k_ref'][0, :, ch['sl']]
        q = ch['q_ref'][0, :, ch['sl']]
        kb = k * ch['beta']
        ch['kbeg'] = kb * ch['eg']
        ch['s1'] = _bdot_nt(jnp.concatenate([kb, q], axis=0), k)
    Ls = [jnp.where(ch['strict'], ch['s1'][:C] * ch['decay'], 0.0) for ch in chains]
    Ts = _unit_triangular_inverses(Ls, row, col)
    for ch, T in zip(chains, Ts):
        v = ch['v_ref'][0, :, ch['sl']]
        ch['uw'] = _bdot(T, jnp.concatenate([v * ch['beta'], ch['kbeg']], axis=1))
    for ch in chains:
        q = ch['q_ref'][0, :, ch['sl']]
        w = ch['uw'][:, DV:]
        ch['ws'] = _bdot(jnp.concatenate([w, q * ch['eg']], axis=0), state_ref[ch['c']])
    for ch in chains:
        ch['v_new'] = ch['uw'][:, :DV] - ch['ws'][:C]
        a_intra = ch['s1'][C:] * ch['decay']
        ch['o_ref'][0, :, ch['sl']] = ch['ws'][C:] + _bdot(a_intra, ch['v_new'])
    for ch in chains:
        k = ch['k_ref'][0, :, ch['sl']]
        k_g = k * jnp.exp(ch['gl'] - ch['gcc'])
        c = ch['c']
        state_ref[c] = state_ref[c] * jnp.exp(ch['gl']) + _bdot(k_g.T, ch['v_new'])


def gdn_bidirectional_delta(q, k, v, beta, g):
    B, S, HD = q.shape
    C, H = GDN_CHUNK, GDN_HEADS
    N = S // C
    g_rows = g.reshape(B, N, C, 2 * H).transpose(0, 1, 3, 2)

    def fwd3(b, n):
        return (b, n, 0)

    def bwd3(b, n):
        return (b, N - 1 - n, 0)

    def fwd4(b, n):
        return (b, n, 0, 0)

    def bwd4(b, n):
        return (b, N - 1 - n, 0, 0)

    def specs(i3, i4):
        big = pl.BlockSpec((1, C, HD), i3)
        small = pl.BlockSpec((1, C, 2 * H), i3)
        return [big, big, big, small, small, pl.BlockSpec((1, 1, 2 * H, C), i4)]

    out = jax.ShapeDtypeStruct((B, S, HD), jnp.float32)
    return pl.pallas_call(
        _gdn_chunk_kernel,
        grid=(B, N),
        in_specs=specs(fwd3, fwd4) + specs(bwd3, bwd4),
        out_specs=[pl.BlockSpec((1, C, HD), fwd3), pl.BlockSpec((1, C, HD), bwd3)],
        out_shape=[out, out],
        scratch_shapes=[pltpu.VMEM((2 * H, GDN_DK, GDN_DV), jnp.float32)],
        compiler_params=pltpu.CompilerParams(dimension_semantics=("parallel", "arbitrary")),
        name="gdn_chunk_delta",
    )(q, k, v, g, beta, g_rows, q, k, v, g, beta, g_rows)


GDN_ROW_TILE = 256
HALO = 8


def _gdn_in_kernel(h_ref, hp_ref, hn_ref, g_ref, wqkv_ref, wz_ref, wba_ref, cw_ref, alog_ref, dtb_ref,
                   q_ref, k_ref, v_ref, z_ref, beta_ref, gl_ref, pre_ref, *, tiles_per_seq):
    H, DK = GDN_HEADS, GDN_DK
    TM = h_ref.shape[0]
    i = pl.program_id(0) % tiles_per_seq
    keep_prev = jnp.where(i == 0, 0.0, 1.0)
    keep_next = jnp.where(i == tiles_per_seq - 1, 0.0, 1.0)
    hx = jnp.concatenate([hp_ref[...] * keep_prev, h_ref[...], hn_ref[...] * keep_next], axis=0)
    hn = _rms(hx, g_ref[...]).astype(jnp.bfloat16)
    pre = _bdot(hn, wqkv_ref[...])
    pre_ref[...] = pre
    pad = GDN_CONV // 2
    acc = None
    for j in range(GDN_CONV):
        term = pre_ref[pl.ds(HALO + j - pad, TM), :] * cw_ref[j:j + 1, :]
        acc = term if acc is None else acc + term
    qkv = _silu(acc)
    for hd in range(H):
        for off, ref, scale in ((0, q_ref, DK ** -0.5), (H * DK, k_ref, 1.0)):
            x = qkv[:, off + hd * DK:off + (hd + 1) * DK]
            ref[:, hd * DK:(hd + 1) * DK] = x * (lax.rsqrt(jnp.sum(x * x, axis=-1, keepdims=True) + NORM_EPS) * scale)
    v_ref[...] = qkv[:, 2 * H * DK:]
    hc = hn[HALO:HALO + TM]
    z_ref[...] = _bdot(hc, wz_ref[...])
    ba = _bdot(hc, wba_ref[...])
    beta_ref[...] = _sigmoid(ba[:, :2 * H])
    a = ba[:, 2 * H:] + dtb_ref[...]
    softplus = jnp.maximum(a, 0.0) + jnp.log(1.0 + jnp.exp(-jnp.abs(a)))
    gl_ref[...] = -jnp.exp(alog_ref[...]) * softplus


def _gdn_out_kernel(of_ref, ob_ref, z_ref, g_ref, w_ref, h_ref, o_ref):
    H, DV = GDN_HEADS, GDN_DV
    o = of_ref[...] + ob_ref[...]
    g = g_ref[...]
    parts = []
    for hd in range(H):
        x = o[:, hd * DV:(hd + 1) * DV]
        parts.append(_rms(x, g))
    y = jnp.concatenate(parts, axis=1) * _silu(z_ref[...])
    o_ref[...] = h_ref[...] + _bdot(y, w_ref[...])


def gdn_layer(h, norm_g, w_in, conv_w, A_log, dt_bias, out_norm_g, w_out):
    B, S, D = h.shape
    T = B * S
    H = GDN_HEADS
    HD = H * GDN_DK
    bf = jnp.bfloat16
    TM = GDN_ROW_TILE
    tiles_per_seq = S // TM
    hb = TM // HALO
    n_halo = T // HALO
    o = GDN_QKV_DIM + H * GDN_DV
    w_qkv = w_in[:, :GDN_QKV_DIM].astype(bf)
    w_z = w_in[:, GDN_QKV_DIM:o].astype(bf)
    w_ba = w_in[:, o:].astype(bf)
    cw = jnp.concatenate([conv_w, jnp.zeros((HALO - GDN_CONV, GDN_QKV_DIM), conv_w.dtype)], axis=0)
    row = lambda i: (i, 0)
    fixed = lambda i: (0, 0)
    f32 = jnp.float32
    ht = h.reshape(T, D)
    q, k, v, z, beta, gl = pl.pallas_call(
        functools.partial(_gdn_in_kernel, tiles_per_seq=tiles_per_seq),
        grid=(T // TM,),
        in_specs=[pl.BlockSpec((TM, D), row),
                  pl.BlockSpec((HALO, D), lambda i: (jnp.maximum(i * hb - 1, 0), 0)),
                  pl.BlockSpec((HALO, D), lambda i: (jnp.minimum((i + 1) * hb, n_halo - 1), 0)),
                  pl.BlockSpec((1, D), fixed),
                  pl.BlockSpec(w_qkv.shape, fixed), pl.BlockSpec(w_z.shape, fixed),
                  pl.BlockSpec(w_ba.shape, fixed), pl.BlockSpec(cw.shape, fixed),
                  pl.BlockSpec((1, 2 * H), fixed), pl.BlockSpec((1, 2 * H), fixed)],
        out_specs=[pl.BlockSpec((TM, HD), row), pl.BlockSpec((TM, HD), row), pl.BlockSpec((TM, HD), row),
                   pl.BlockSpec((TM, HD), row), pl.BlockSpec((TM, 2 * H), row), pl.BlockSpec((TM, 2 * H), row)],
        out_shape=[jax.ShapeDtypeStruct((T, HD), f32)] * 4 + [jax.ShapeDtypeStruct((T, 2 * H), f32)] * 2,
        scratch_shapes=[pltpu.VMEM((TM + 2 * HALO, GDN_QKV_DIM), f32)],
        compiler_params=pltpu.CompilerParams(vmem_limit_bytes=VMEM_LIMIT_BYTES),
        name="gdn_in_proj",
    )(ht, ht, ht, norm_g.reshape(1, D), w_qkv, w_z, w_ba, cw,
      A_log.reshape(1, 2 * H).astype(f32), dt_bias.reshape(1, 2 * H).astype(f32))

    o_fwd, o_bwd = gdn_bidirectional_delta(q.reshape(B, S, HD), k.reshape(B, S, HD), v.reshape(B, S, HD),
                                           beta.reshape(B, S, 2 * H), gl.reshape(B, S, 2 * H))

    h_new = pl.pallas_call(
        _gdn_out_kernel,
        grid=(T // ROW_TILE,),
        in_specs=[pl.BlockSpec((ROW_TILE, HD), row)] * 3 +
                 [pl.BlockSpec((1, GDN_DV), fixed), pl.BlockSpec((HD, D), fixed), pl.BlockSpec((ROW_TILE, D), row)],
        out_specs=pl.BlockSpec((ROW_TILE, D), row),
        out_shape=jax.ShapeDtypeStruct((T, D), f32),
        name="gdn_out_proj",
    )(o_fwd.reshape(T, HD), o_bwd.reshape(T, HD), z, out_norm_g.reshape(1, GDN_DV), w_out.astype(bf), ht)
    return h_new.reshape(B, S, D)


def _sigmoid(x):
    return 0.5 * jnp.tanh(0.5 * x) + 0.5


def _silu(x):
    return x * _sigmoid(x)


def _moe_router_kernel(h_ref, g_ref, w_ref, xn_ref, aff_ref):
    xn = _rms(h_ref[...], g_ref[...])
    xn_ref[...] = xn.astype(xn_ref.dtype)
    logits = _bdot(xn, w_ref[...])
    e = jnp.exp(logits - jnp.max(logits, axis=-1, keepdims=True))
    aff_ref[...] = e / jnp.sum(e, axis=-1, keepdims=True)


def _moe_ffn_kernel(x_ref, wg_ref, wu_ref, wd_ref, gate_ref, y_ref):
    x = x_ref[0]
    hid = _silu(_bdot(x, wg_ref[0])) * _bdot(x, wu_ref[0])
    y_ref[0] = (_bdot(hid, wd_ref[0]) * gate_ref[0]).astype(y_ref.dtype)


COMBINE_TILE = 256
COMBINE_SUBTILES = 2
STRIP_ROWS = 64
STRIP_ALIGN = 16
STRIPS_PER_DOT = 4


def _moe_combine_ple_kernel(base_ref, off_ref, cnt_ref,
                            h_ref, p_ref, g_ref, wg_ref, wp_ref, fg_ref, ye_ref, tok_ref,
                            o_ref, ybuf, tbuf, sems, yextra, textra, extra_sems, acc_ref,
                            *, n_tiles, total_rows, final_norm):
    E, TB, R, U = N_EXPERTS, COMBINE_TILE, STRIP_ROWS, COMBINE_SUBTILES
    K = STRIPS_PER_DOT * R
    n_steps = n_tiles // U
    j = pl.program_id(0)
    cur = j % 2

    def strip_copies(step, u, e, b):
        start = pl.multiple_of(base_ref[e * n_tiles + step * U + u], STRIP_ALIGN)
        rows = pl.ds((u * E + e) * R, R)
        return (pltpu.make_async_copy(ye_ref.at[pl.ds(start, R), :], ybuf.at[b, rows, :],
                                      sems.at[0, b, u * E + e]),
                pltpu.make_async_copy(tok_ref.at[pl.ds(start, R), :], tbuf.at[b, rows, :],
                                      sems.at[1, b, u * E + e]))

    def all_strip_copies(step, b):
        return [cp for u in range(U) for e in range(E) for cp in strip_copies(step, u, e, b)]

    @pl.when(j == 0)
    def _():
        for cp in all_strip_copies(0, 0):
            cp.start()

    @pl.when(j + 1 < n_steps)
    def _():
        for cp in all_strip_copies(j + 1, 1 - cur):
            cp.start()

    for cp in all_strip_copies(j, cur):
        cp.wait()

    lane = lax.broadcasted_iota(jnp.int32, (K, TB), 1)
    srow = lax.broadcasted_iota(jnp.int32, (R, 1), 0)
    groups = [(u, grp) for u in range(U) for grp in range(E // STRIPS_PER_DOT)]
    hots = []
    for u, grp in groups:
        tile = j * U + u
        wanted = []
        for e in range(grp * STRIPS_PER_DOT, (grp + 1) * STRIPS_PER_DOT):
            lo = off_ref[e * n_tiles + tile]
            wanted.append((srow >= lo) & (srow < lo + cnt_ref[e * n_tiles + tile]))
        rows = pl.ds((u * E + grp * STRIPS_PER_DOT) * R, K)
        tok = tbuf[cur, rows, 0:1] - tile * TB
        hots.append(jnp.where(jnp.concatenate(wanted, axis=0) & (lane == tok), 1.0, 0.0))
    parts = [_bdot(hot.T, ybuf[cur, pl.ds((u * E + grp * STRIPS_PER_DOT) * R, K), :])
             for hot, (u, grp) in zip(hots, groups)]
    for u in range(U):
        mine = [part for part, (uu, _) in zip(parts, groups) if uu == u]
        acc_ref[u] = functools.reduce(lambda a, b: a + b, mine)

    lane_r = lax.broadcasted_iota(jnp.int32, (R, TB), 1)
    for u in range(U):
        tile = j * U + u
        for e in range(E):
            base = base_ref[e * n_tiles + tile]
            first = base + off_ref[e * n_tiles + tile]
            end = first + cnt_ref[e * n_tiles + tile]
            n_sub = (end - base + R - 1) // R

            def body(k, carry, u=u, tile=tile, base=base, first=first, end=end):
                want0 = base + k * R
                start = pl.multiple_of(jnp.minimum(want0, total_rows - R), STRIP_ALIGN)
                cy = pltpu.make_async_copy(ye_ref.at[pl.ds(start, R), :], yextra, extra_sems.at[0])
                ct = pltpu.make_async_copy(tok_ref.at[pl.ds(start, R), :], textra, extra_sems.at[1])
                cy.start()
                ct.start()
                cy.wait()
                ct.wait()
                grow = srow + start
                wanted = (grow >= jnp.maximum(want0, first)) & (grow < jnp.minimum(want0 + R, end))
                hot_t = jnp.where(wanted & (lane_r == textra[:, 0:1] - tile * TB), 1.0, 0.0)
                acc_ref[u] += _bdot(hot_t.T, yextra[...])
                return carry

            lax.fori_loop(1, n_sub, body, 0)

    tiles = [pl.ds(u * TB, TB) for u in range(U)]
    hm = [h_ref[t, :] + acc_ref[u] for u, t in enumerate(tiles)]
    gate = [_sigmoid(_bdot(_rms(x, g_ref[...]), wg_ref[...])) for x in hm]
    out = [x + _bdot(p_ref[t, :], wp_ref[...]) * gt for x, gt, t in zip(hm, gate, tiles)]
    for x, t in zip(out, tiles):
        o_ref[t, :] = _rms(x, fg_ref[...]) if final_norm else x


F32_INF_BITS = 0x7F800000


def _topk_select_kernel(aff_ref, wcum_ref, *, cap):
    a = aff_ref[0]
    R = a.shape[0]
    bits = lax.bitcast_convert_type(a, jnp.int32)

    def count(mask):
        m = mask.astype(jnp.float32)
        return jnp.sum(jnp.sum(m, axis=0, keepdims=True), axis=1, keepdims=True)

    def bisect(_, carry):
        lo, hi = carry
        mid = lo + jnp.right_shift(hi - lo + 1, 1)
        ok = count(bits >= mid) >= cap
        return jnp.where(ok, mid, lo), jnp.where(ok, hi, mid - 1)

    thr, _ = lax.fori_loop(0, 31, bisect, (jnp.zeros((1, 1), jnp.int32),
                                           jnp.full((1, 1), F32_INF_BITS, jnp.int32)))
    gt = bits > thr
    eq = bits == thr
    need = cap - count(gt)

    lane_r = lax.broadcasted_iota(jnp.int32, (LANES, LANES), 0)
    lane_c = lax.broadcasted_iota(jnp.int32, (LANES, LANES), 1)
    upper = (lane_r <= lane_c).astype(jnp.bfloat16)
    row_r = lax.broadcasted_iota(jnp.int32, (R, R), 0)
    row_c = lax.broadcasted_iota(jnp.int32, (R, R), 1)
    earlier_rows = (row_r > row_c).astype(jnp.bfloat16)

    eq_in_row = jnp.dot(eq.astype(jnp.bfloat16), upper, preferred_element_type=jnp.float32)
    eq_row_total = jnp.broadcast_to(eq_in_row[:, LANES - 1:LANES], (R, LANES))
    eq_before = jnp.dot(earlier_rows, eq_row_total.astype(jnp.bfloat16), preferred_element_type=jnp.float32)
    sel = gt | (eq & (eq_before + eq_in_row <= need))
    wcum = jnp.dot(sel.astype(jnp.bfloat16), upper, preferred_element_type=jnp.float32)
    wcum_ref[0] = wcum.astype(jnp.int32)


SLOT_TILE = 512


def _slot_token_kernel(row_end_ref, wcum_ref, tok_ref):
    SB = tok_ref.shape[1]
    row_end = row_end_ref[0]
    R = row_end.shape[1]
    slot = pl.program_id(1) * SB + lax.broadcasted_iota(jnp.int32, (SB, 1), 0)
    done = row_end <= slot
    row = jnp.sum(done.astype(jnp.int32), axis=1, keepdims=True)
    row_start = jnp.max(jnp.where(done, row_end, 0), axis=1, keepdims=True)
    one_hot = (lax.broadcasted_iota(jnp.int32, (SB, R), 1) == row).astype(jnp.bfloat16)
    wrow = jnp.dot(one_hot, wcum_ref[0].astype(jnp.float32).astype(jnp.bfloat16),
                   preferred_element_type=jnp.float32)
    local = (slot - row_start).astype(jnp.float32)
    lane = jnp.sum((wrow <= local).astype(jnp.int32), axis=1, keepdims=True)
    tok_ref[0] = jnp.broadcast_to(row * LANES + lane, (SB, LANES))


def expert_choice_indices(aff_t, cap):
    E, T = aff_t.shape
    R = T // LANES
    wcum = pl.pallas_call(
        functools.partial(_topk_select_kernel, cap=cap),
        grid=(E,),
        in_specs=[pl.BlockSpec((1, R, LANES), lambda e: (e, 0, 0))],
        out_specs=pl.BlockSpec((1, R, LANES), lambda e: (e, 0, 0)),
        out_shape=jax.ShapeDtypeStruct((E, R, LANES), jnp.int32),
        name="moe_topk_select",
    )(aff_t.reshape(E, R, LANES))
    row_end = jnp.cumsum(wcum[:, :, LANES - 1], axis=1)
    row_start = row_end - wcum[:, :, LANES - 1]
    tok = pl.pallas_call(
        _slot_token_kernel,
        grid=(E, cap // SLOT_TILE),
        in_specs=[pl.BlockSpec((1, 1, R), lambda e, j: (e, 0, 0)),
                  pl.BlockSpec((1, R, LANES), lambda e, j: (e, 0, 0))],
        out_specs=pl.BlockSpec((1, SLOT_TILE, LANES), lambda e, j: (e, j, 0)),
        out_shape=jax.ShapeDtypeStruct((E, cap, LANES), jnp.int32),
        name="moe_slot_token",
    )(row_end.reshape(E, 1, R), wcum)
    return tok, row_start


def moe_ple_layer(h, p, norm_g, w_router, w_gate, w_up, w_down, ple_norm_g, ple_w_gate, ple_w_proj,
                  final_g, final_norm):
    T, D = h.shape
    E, F = N_EXPERTS, w_gate.shape[-1]
    cap = EC_CAPACITY_FACTOR * T // E
    bf = jnp.bfloat16
    row = lambda i: (i, 0)
    fixed = lambda i: (0, 0)
    xn, aff = pl.pallas_call(
        _moe_router_kernel,
        grid=(T // ROW_TILE,),
        in_specs=[pl.BlockSpec((ROW_TILE, D), row), pl.BlockSpec((1, D), fixed),
                  pl.BlockSpec((D, E), fixed)],
        out_specs=[pl.BlockSpec((ROW_TILE, D), row), pl.BlockSpec((ROW_TILE, E), row)],
        out_shape=[jax.ShapeDtypeStruct((T, D), bf), jax.ShapeDtypeStruct((T, E), jnp.float32)],
        name="moe_router",
    )(h, norm_g.reshape(1, D), w_router.astype(bf))
    aff_t = aff.T
    tok, row_start = expert_choice_indices(aff_t, cap)
    idx = tok[:, :, 0]
    gate = jnp.take_along_axis(aff_t, idx, axis=1)
    xe = xn.at[idx].get(mode='promise_in_bounds')
    ye = pl.pallas_call(
        _moe_ffn_kernel,
        grid=(E, cap // FFN_ROW_TILE),
        in_specs=[pl.BlockSpec((1, FFN_ROW_TILE, D), lambda e, i: (e, i, 0)),
                  pl.BlockSpec((1, D, F), lambda e, i: (e, 0, 0)),
                  pl.BlockSpec((1, D, F), lambda e, i: (e, 0, 0)),
                  pl.BlockSpec((1, F, D), lambda e, i: (e, 0, 0)),
                  pl.BlockSpec((1, FFN_ROW_TILE, 1), lambda e, i: (e, i, 0))],
        out_specs=pl.BlockSpec((1, FFN_ROW_TILE, D), lambda e, i: (e, i, 0)),
        out_shape=jax.ShapeDtypeStruct((E, cap, D), bf),
        compiler_params=pltpu.CompilerParams(dimension_semantics=("parallel", "arbitrary"),
                                             vmem_limit_bytes=VMEM_LIMIT_BYTES),
        name="moe_expert_ffn",
    )(xe, w_gate.astype(bf), w_up.astype(bf), w_down.astype(bf), gate[..., None])

    TB, R, U = COMBINE_TILE, STRIP_ROWS, COMBINE_SUBTILES
    n_tiles = T // TB
    total_rows = E * cap
    lo = jnp.concatenate([row_start[:, ::TB // LANES], jnp.full((E, 1), cap, jnp.int32)], axis=1)
    first = lo[:, :-1] + (jnp.arange(E, dtype=jnp.int32) * cap)[:, None]
    cnt = lo[:, 1:] - lo[:, :-1]
    base = jnp.minimum(first // STRIP_ALIGN * STRIP_ALIGN, total_rows - R)
    off = first - base

    P = p.shape[1]
    tile = lambda i, *_: (i, 0)
    const = lambda i, *_: (0, 0)
    return pl.pallas_call(
        functools.partial(_moe_combine_ple_kernel, n_tiles=n_tiles, total_rows=total_rows,
                          final_norm=final_norm),
        grid_spec=pltpu.PrefetchScalarGridSpec(
            num_scalar_prefetch=3,
            grid=(n_tiles // U,),
            in_specs=[pl.BlockSpec((U * TB, D), tile), pl.BlockSpec((U * TB, P), tile),
                      pl.BlockSpec((1, D), const), pl.BlockSpec((D, D), const), pl.BlockSpec((P, D), const),
                      pl.BlockSpec((1, D), const),
                      pl.BlockSpec(memory_space=pl.ANY), pl.BlockSpec(memory_space=pl.ANY)],
            out_specs=pl.BlockSpec((U * TB, D), tile),
            scratch_shapes=[pltpu.VMEM((2, U * E * R, D), bf), pltpu.VMEM((2, U * E * R, LANES), jnp.int32),
                            pltpu.SemaphoreType.DMA((2, 2, U * E)),
                            pltpu.VMEM((R, D), bf), pltpu.VMEM((R, LANES), jnp.int32),
                            pltpu.SemaphoreType.DMA((2,)),
                            pltpu.VMEM((U, TB, D), jnp.float32)]),
        out_shape=jax.ShapeDtypeStruct((T, D), jnp.float32),
        compiler_params=pltpu.CompilerParams(dimension_semantics=("arbitrary",),
                                             vmem_limit_bytes=VMEM_LIMIT_BYTES),
        name="moe_combine_ple",
    )(base.reshape(-1), off.reshape(-1), cnt.reshape(-1),
      h, p, ple_norm_g.reshape(1, D), ple_w_gate.astype(bf), ple_w_proj.astype(bf), final_g.reshape(1, D),
      ye.reshape(total_rows, D), tok.reshape(total_rows, LANES))


def encoder_trunk(x, p, W):
    h = x
    for i in range(DEPTH):
        j = i // N_MIXERS
        if i % N_MIXERS == 0:
            h = mla_layer(h, W['norm_mix_g'][i], W['mla_w_in'][j], W['mla_q_norm_g'][j], W['mla_w_qb'][j],
                          W['mla_kv_norm_g'][j], W['mla_w_kvb'][j], W['mla_w_out'][j])
        else:
            h = gdn_layer(h, W['norm_mix_g'][i], W['gdn_w_in'][j], W['gdn_conv_w'][j], W['gdn_A_log'][j],
                          W['gdn_dt_bias'][j], W['gdn_norm_g'][j], W['gdn_w_out'][j])
        B, S, D = h.shape
        ht = moe_ple_layer(h.reshape(B * S, D), p[i].reshape(B * S, PLE_DIM), W['norm_ffn_g'][i],
                           W['moe_w_router'][i], W['moe_w_gate'][i], W['moe_w_up'][i], W['moe_w_down'][i],
                           W['norm_ple_g'][i], W['ple_w_gate'][i], W['ple_w_proj'][i],
                           W['final_norm_g'], final_norm=(i == DEPTH - 1))
        h = ht.reshape(B, S, D)
    return h


def kernel(x_prompt, x_sample, p_prompt, p_sample, norm_mix_g, norm_ffn_g, norm_ple_g, final_norm_g,
           mla_w_in, mla_q_norm_g, mla_w_qb, mla_kv_norm_g, mla_w_kvb, mla_w_out,
           gdn_w_in, gdn_conv_w, gdn_A_log, gdn_dt_bias, gdn_norm_g, gdn_w_out,
           moe_w_router, moe_w_gate, moe_w_up, moe_w_down, ple_w_proj, ple_w_gate):
    W = {
        'norm_mix_g': norm_mix_g, 'norm_ffn_g': norm_ffn_g, 'norm_ple_g': norm_ple_g,
        'final_norm_g': final_norm_g,
        'mla_w_in': mla_w_in, 'mla_q_norm_g': mla_q_norm_g, 'mla_w_qb': mla_w_qb,
        'mla_kv_norm_g': mla_kv_norm_g, 'mla_w_kvb': mla_w_kvb, 'mla_w_out': mla_w_out,
        'gdn_w_in': gdn_w_in, 'gdn_conv_w': gdn_conv_w, 'gdn_A_log': gdn_A_log,
        'gdn_dt_bias': gdn_dt_bias, 'gdn_norm_g': gdn_norm_g, 'gdn_w_out': gdn_w_out,
        'moe_w_router': moe_w_router, 'moe_w_gate': moe_w_gate, 'moe_w_up': moe_w_up,
        'moe_w_down': moe_w_down, 'ple_w_proj': ple_w_proj, 'ple_w_gate': ple_w_gate,
    }
    y_prompt = encoder_trunk(x_prompt, p_prompt, W)
    y_sample = encoder_trunk(x_sample, p_sample, W)
    return (y_prompt, y_sample)
```

```python
import functools

import jax
import jax.numpy as jnp
from jax import lax
from jax.experimental import pallas as pl
from jax.experimental.pallas import tpu as pltpu

D_MODEL = 1024
DEPTH = 4
PLE_DIM = 256
N_MIXERS = 2
MLA_HEADS = 8
MLA_NOPE_DIM = 128
MLA_ROPE_DIM = 64
MLA_V_DIM = 128
MLA_Q_LORA = 256
MLA_KV_LORA = 128
ROPE_THETA = 10000.0
Q_BLOCK = 128
GDN_HEADS = 8
GDN_DK = 128
GDN_DV = 128
GDN_CONV = 5
GDN_CHUNK = 64
GDN_QKV_DIM = 2 * GDN_HEADS * GDN_DK + GDN_HEADS * GDN_DV
N_EXPERTS = 16
EC_CAPACITY_FACTOR = 2
NORM_EPS = 1e-6


def rope_tables(seq):
    pos = jnp.arange(seq, dtype=jnp.float32)
    inv = ROPE_THETA ** (-jnp.arange(0, MLA_ROPE_DIM, 2, dtype=jnp.float32) / MLA_ROPE_DIM)
    ang = pos[:, None] * inv[None, :]
    return jnp.cos(ang), jnp.sin(ang)


LANES = 128
ROW_TILE = 512
FFN_ROW_TILE = 1024
GDN_CHUNKS_PER_STEP = 2
ATTN_Q_TILE = 512
ATTN_HEADS_PER_STEP = 4
VMEM_LIMIT_BYTES = 56 * 1024 * 1024
MLA_QK_PAD = 2 * LANES


def _rms(x, g):
    return x * lax.rsqrt(jnp.mean(x * x, axis=-1, keepdims=True) + NORM_EPS) * g


def _mla_proj_kernel(h_ref, g_ref, w_in_ref, qg_ref, w_qb_ref, kvg_ref, w_k_ref, w_v_ref,
                     cos_ref, sin_ref, q_ref, k_ref, v_ref):
    H, NP = MLA_HEADS, MLA_NOPE_DIM
    scale = (MLA_NOPE_DIM + MLA_ROPE_DIM) ** -0.5
    hn = _rms(h_ref[...], g_ref[...])
    lat = _bdot(hn, w_in_ref[...])
    q_lat = lat[:, :MLA_Q_LORA]
    kv_lat = lat[:, MLA_Q_LORA:MLA_Q_LORA + MLA_KV_LORA]
    o = MLA_Q_LORA + MLA_KV_LORA
    cos, sin = cos_ref[...], sin_ref[...]
    k_rope = (lat[:, o:o + LANES] * cos + lat[:, o + LANES:o + 2 * LANES] * sin).astype(jnp.bfloat16)
    q = _bdot(_rms(q_lat, qg_ref[...]), w_qb_ref[...]) * scale
    kvn = _rms(kv_lat, kvg_ref[...])
    k_nope = _bdot(kvn, w_k_ref[...]).astype(jnp.bfloat16)
    v_ref[...] = _bdot(kvn, w_v_ref[...]).astype(jnp.bfloat16)
    for hd in range(H):
        a = q[:, (H + hd) * NP:(H + hd + 1) * NP]
        b = q[:, (2 * H + hd) * NP:(2 * H + hd + 1) * NP]
        q_ref[:, hd * MLA_QK_PAD:hd * MLA_QK_PAD + NP] = q[:, hd * NP:(hd + 1) * NP].astype(jnp.bfloat16)
        q_ref[:, hd * MLA_QK_PAD + NP:(hd + 1) * MLA_QK_PAD] = (a * cos + b * sin).astype(jnp.bfloat16)
        k_ref[:, hd * MLA_QK_PAD:hd * MLA_QK_PAD + NP] = k_nope[:, hd * NP:(hd + 1) * NP]
        k_ref[:, hd * MLA_QK_PAD + NP:(hd + 1) * MLA_QK_PAD] = k_rope


def _mla_attn_kernel(q_ref, k_ref, vt_ref, o_ref):
    heads = range(ATTN_HEADS_PER_STEP)
    qk = lambda hd: slice(hd * MLA_QK_PAD, (hd + 1) * MLA_QK_PAD)
    vv = lambda hd: slice(hd * MLA_V_DIM, (hd + 1) * MLA_V_DIM)
    s_t = [_bdot_nt(k_ref[0, :, qk(hd)], q_ref[0, :, qk(hd)]) for hd in heads]
    p_t = [jnp.exp(x - jnp.max(x, axis=0, keepdims=True)) for x in s_t]
    denom = [jnp.sum(x, axis=0, keepdims=True) for x in p_t]
    o_t = [_bdot(vt_ref[0, vv(hd), :], p_t[hd]) / denom[hd] for hd in heads]
    for hd in heads:
        o_ref[0, :, vv(hd)] = o_t[hd].T.astype(o_ref.dtype)


def _matmul_residual_kernel(x_ref, w_ref, h_ref, o_ref):
    o_ref[...] = h_ref[...] + _bdot(x_ref[...], w_ref[...])


def matmul_residual(x, w, h, name):
    T, K = x.shape
    D = w.shape[1]
    return pl.pallas_call(
        _matmul_residual_kernel,
        grid=(T // ROW_TILE,),
        in_specs=[pl.BlockSpec((ROW_TILE, K), lambda i: (i, 0)),
                  pl.BlockSpec((K, D), lambda i: (0, 0)),
                  pl.BlockSpec((ROW_TILE, D), lambda i: (i, 0))],
        out_specs=pl.BlockSpec((ROW_TILE, D), lambda i: (i, 0)),
        out_shape=jax.ShapeDtypeStruct((T, D), jnp.float32),
        name=name,
    )(x, w, h)


def _rope_swapped_columns(w):
    half = MLA_ROPE_DIM // 2
    pad = jnp.zeros(w.shape[:-1] + (LANES - MLA_ROPE_DIM,), w.dtype)
    plain = jnp.concatenate([w, pad], axis=-1)
    swapped = jnp.concatenate([w[..., half:], w[..., :half], pad], axis=-1)
    return plain, swapped


def mla_layer(h, norm_g, w_in, q_norm_g, w_qb, kv_norm_g, w_kvb, w_out):
    B, S, D = h.shape
    T = B * S
    H, NP, R = MLA_HEADS, MLA_NOPE_DIM, MLA_ROPE_DIM
    bf = jnp.bfloat16
    o = MLA_Q_LORA + MLA_KV_LORA
    kr_plain, kr_swapped = _rope_swapped_columns(w_in[:, o:])
    w_in_ext = jnp.concatenate([w_in[:, :o], kr_plain, kr_swapped], axis=1).astype(bf)
    w_qb_h = w_qb.reshape(MLA_Q_LORA, H, NP + R)
    qr_plain, qr_swapped = _rope_swapped_columns(w_qb_h[:, :, NP:])
    w_qb_ext = jnp.concatenate([w_qb_h[:, :, :NP].reshape(MLA_Q_LORA, H * NP),
                                qr_plain.reshape(MLA_Q_LORA, H * LANES),
                                qr_swapped.reshape(MLA_Q_LORA, H * LANES)], axis=1).astype(bf)
    w_kvb_h = w_kvb.reshape(MLA_KV_LORA, H, NP + MLA_V_DIM)
    w_k = w_kvb_h[:, :, :NP].reshape(MLA_KV_LORA, H * NP).astype(bf)
    w_v = w_kvb_h[:, :, NP:].reshape(MLA_KV_LORA, H * MLA_V_DIM).astype(bf)
    cos, sin = rope_tables(S)
    zeros = jnp.zeros((S, LANES - R), jnp.float32)
    cos_pad = jnp.concatenate([cos, cos, zeros], axis=1)
    sin_pad = jnp.concatenate([-sin, sin, zeros], axis=1)

    tiles_per_seq = S // ROW_TILE
    row = lambda i: (i, 0)
    fixed = lambda i: (0, 0)
    pos = lambda i: (i % tiles_per_seq, 0)
    q_cat, k_cat, v = pl.pallas_call(
        _mla_proj_kernel,
        grid=(T // ROW_TILE,),
        in_specs=[pl.BlockSpec((ROW_TILE, D), row),
                  pl.BlockSpec((1, D), fixed),
                  pl.BlockSpec(w_in_ext.shape, fixed),
                  pl.BlockSpec((1, MLA_Q_LORA), fixed),
                  pl.BlockSpec(w_qb_ext.shape, fixed),
                  pl.BlockSpec((1, MLA_KV_LORA), fixed),
                  pl.BlockSpec(w_k.shape, fixed),
                  pl.BlockSpec(w_v.shape, fixed),
                  pl.BlockSpec((ROW_TILE, LANES), pos),
                  pl.BlockSpec((ROW_TILE, LANES), pos)],
        out_specs=[pl.BlockSpec((ROW_TILE, H * MLA_QK_PAD), row),
                   pl.BlockSpec((ROW_TILE, H * MLA_QK_PAD), row),
                   pl.BlockSpec((ROW_TILE, H * MLA_V_DIM), row)],
        out_shape=[jax.ShapeDtypeStruct((T, H * MLA_QK_PAD), bf),
                   jax.ShapeDtypeStruct((T, H * MLA_QK_PAD), bf),
                   jax.ShapeDtypeStruct((T, H * MLA_V_DIM), bf)],
        name="mla_proj",
    )(h.reshape(T, D), norm_g.reshape(1, D), w_in_ext, q_norm_g.reshape(1, -1), w_qb_ext,
      kv_norm_g.reshape(1, -1), w_k, w_v, cos_pad, sin_pad)

    G = ATTN_HEADS_PER_STEP
    o_attn = pl.pallas_call(
        _mla_attn_kernel,
        grid=(B, H // G, S // ATTN_Q_TILE),
        in_specs=[pl.BlockSpec((1, ATTN_Q_TILE, G * MLA_QK_PAD), lambda b, hd, i: (b, i, hd)),
                  pl.BlockSpec((1, S, G * MLA_QK_PAD), lambda b, hd, i: (b, 0, hd)),
                  pl.BlockSpec((1, G * MLA_V_DIM, S), lambda b, hd, i: (b, hd, 0))],
        out_specs=pl.BlockSpec((1, ATTN_Q_TILE, G * MLA_V_DIM), lambda b, hd, i: (b, i, hd)),
        out_shape=jax.ShapeDtypeStruct((B, S, H * MLA_V_DIM), bf),
        compiler_params=pltpu.CompilerParams(
            dimension_semantics=("parallel", "parallel", "arbitrary"),
            vmem_limit_bytes=VMEM_LIMIT_BYTES),
        name="mla_attention",
    )(q_cat.reshape(B, S, -1), k_cat.reshape(B, S, -1), v.reshape(B, S, -1).transpose(0, 2, 1))

    h_new = matmul_residual(o_attn.reshape(T, H * MLA_V_DIM), w_out.astype(bf), h.reshape(T, D),
                            name="mla_out_proj")
    return h_new.reshape(B, S, D)


def _bdot(a, b):
    return jnp.dot(a.astype(jnp.bfloat16), b.astype(jnp.bfloat16), preferred_element_type=jnp.float32)


def _bdot_nt(a, b):
    return lax.dot_general(a.astype(jnp.bfloat16), b.astype(jnp.bfloat16),
                           (((1,), (1,)), ((), ())), preferred_element_type=jnp.float32)


def _unit_triangular_inverses(Ls, row, col):
    C = Ls[0].shape[0]
    eye = (row == col).astype(jnp.float32)

    def same_block(s):
        return (row // s) == (col // s)

    blk8 = same_block(8)
    L8 = [jnp.where(blk8, L, 0.0) for L in Ls]
    P2 = [_bdot(a, a) for a in L8]
    P4 = [_bdot(a, a) for a in P2]
    X = [eye - a for a in L8]
    X = [x + _bdot(x, p) for x, p in zip(X, P2)]
    X = [x + _bdot(x, p) for x, p in zip(X, P4)]
    s = 8
    while s < C:
        pair = same_block(2 * s) & jnp.logical_not(same_block(s))
        XC = [_bdot(x, jnp.where(pair, L, 0.0)) for x, L in zip(X, Ls)]
        X = [x - _bdot(xc, x) for x, xc in zip(X, XC)]
        s *= 2
    return X


def _gdn_chunk_kernel(qf_ref, kf_ref, vf_ref, gcf_ref, bcf_ref, grf_ref,
                      qb_ref, kb_ref, vb_ref, gcb_ref, bcb_ref, grb_ref,
                      of_ref, ob_ref, state_ref):
    H, C, DK, DV = GDN_HEADS, GDN_CHUNK, GDN_DK, GDN_DV

    @pl.when(pl.program_id(1) == 0)
    def _():
        state_ref[...] = jnp.zeros_like(state_ref)

    row = lax.broadcasted_iota(jnp.int32, (C, C), 0)
    col = lax.broadcasted_iota(jnp.int32, (C, C), 1)
    dirs = ((0, qf_ref, kf_ref, vf_ref, gcf_ref, bcf_ref, grf_ref, of_ref),
            (1, qb_ref, kb_ref, vb_ref, gcb_ref, bcb_ref, grb_ref, ob_ref))
    chains = [[] for _ in range(GDN_CHUNKS_PER_STEP)]
    for d, q_ref, k_ref, v_ref, gc_ref, bc_ref, gr_ref, o_ref in dirs:
        if d == 0:
            incl, strict, last = row >= col, row > col, C - 1
        else:
            incl, strict, last = row <= col, row < col, 0
        tri = incl.astype(jnp.float32)
        for i in range(GDN_CHUNKS_PER_STEP):
            sub = i if d == 0 else GDN_CHUNKS_PER_STEP - 1 - i
            rows = slice(sub * C, (sub + 1) * C)
            gc_cols = jnp.dot(tri, gc_ref[0, rows, :], precision=lax.Precision.HIGHEST,
                              preferred_element_type=jnp.float32)
            gc_rows = jnp.dot(gr_ref[0, sub], tri.T, precision=lax.Precision.HIGHEST,
                              preferred_element_type=jnp.float32)
            beta_cols = bc_ref[0, rows, :]
            for h in range(H):
                c = d * H + h
                chains[i].append(dict(
                    c=c, incl=incl, strict=strict, o_ref=o_ref, rows=rows, sl=slice(h * DK, (h + 1) * DK),
                    q_ref=q_ref, k_ref=k_ref, v_ref=v_ref,
                    beta=beta_cols[:, c:c + 1], gcc=gc_cols[:, c:c + 1], gcr=gc_rows[c:c + 1, :],
                    gl=gc_cols[last:last + 1, c:c + 1]))

    every = [ch for group in chains for ch in group]
    for ch in every:
        ch['decay'] = jnp.where(ch['incl'], jnp.exp(jnp.where(ch['incl'], ch['gcc'] - ch['gcr'], 0.0)), 0.0)
        ch['eg'] = jnp.exp(ch['gcc'])
    for ch in every:
        k = ch['k_ref'][0, ch['rows'], ch['sl']]
        q = ch['q_ref'][0, ch['rows'], ch['sl']]
        kb = k * ch['beta']
        ch['kbeg'] = kb * ch['eg']
        ch['s1'] = _bdot_nt(jnp.concatenate([kb, q], axis=0), k)
    Ls = [jnp.where(ch['strict'], ch['s1'][:C] * ch['decay'], 0.0) for ch in every]
    Ts = _unit_triangular_inverses(Ls, row, col)
    for ch, T in zip(every, Ts):
        v = ch['v_ref'][0, ch['rows'], ch['sl']]
        ch['uw'] = _bdot(T, jnp.concatenate([v * ch['beta'], ch['kbeg']], axis=1))

    for group in chains:
        for ch in group:
            q = ch['q_ref'][0, ch['rows'], ch['sl']]
            w = ch['uw'][:, DV:]
            ch['ws'] = _bdot(jnp.concatenate([w, q * ch['eg']], axis=0), state_ref[ch['c']])
        for ch in group:
            ch['v_new'] = ch['uw'][:, :DV] - ch['ws'][:C]
            a_intra = ch['s1'][C:] * ch['decay']
            ch['o_ref'][0, ch['rows'], ch['sl']] = ch['ws'][C:] + _bdot(a_intra, ch['v_new'])
        for ch in group:
            k = ch['k_ref'][0, ch['rows'], ch['sl']]
            k_g = k * jnp.exp(ch['gl'] - ch['gcc'])
            c = ch['c']
            state_ref[c] = state_ref[c] * jnp.exp(ch['gl']) + _bdot(k_g.T, ch['v_new'])


def gdn_bidirectional_delta(q, k, v, beta, g):
    B, S, HD = q.shape
    C, H, P = GDN_CHUNK, GDN_HEADS, GDN_CHUNKS_PER_STEP
    N = S // (C * P)
    g_rows = g.reshape(B, N * P, C, 2 * H).transpose(0, 1, 3, 2)

    def fwd3(b, n):
        return (b, n, 0)

    def bwd3(b, n):
        return (b, N - 1 - n, 0)

    def fwd4(b, n):
        return (b, n, 0, 0)

    def bwd4(b, n):
        return (b, N - 1 - n, 0, 0)

    def specs(i3, i4):
        big = pl.BlockSpec((1, P * C, HD), i3)
        small = pl.BlockSpec((1, P * C, 2 * H), i3)
        return [big, big, big, small, small, pl.BlockSpec((1, P, 2 * H, C), i4)]

    out = jax.ShapeDtypeStruct((B, S, HD), jnp.float32)
    return pl.pallas_call(
        _gdn_chunk_kernel,
        grid=(B, N),
        in_specs=specs(fwd3, fwd4) + specs(bwd3, bwd4),
        out_specs=[pl.BlockSpec((1, P * C, HD), fwd3), pl.BlockSpec((1, P * C, HD), bwd3)],
        out_shape=[out, out],
        scratch_shapes=[pltpu.VMEM((2 * H, GDN_DK, GDN_DV), jnp.float32)],
        compiler_params=pltpu.CompilerParams(dimension_semantics=("parallel", "arbitrary")),
        name="gdn_chunk_delta",
    )(q, k, v, g, beta, g_rows, q, k, v, g, beta, g_rows)


GDN_ROW_TILE = 256
HALO = 8


def _gdn_in_kernel(h_ref, hp_ref, hn_ref, g_ref, wqkv_ref, wz_ref, wba_ref, cw_ref, alog_ref, dtb_ref,
                   q_ref, k_ref, v_ref, z_ref, beta_ref, gl_ref, pre_ref, *, tiles_per_seq):
    H, DK = GDN_HEADS, GDN_DK
    TM = h_ref.shape[0]
    i = pl.program_id(0) % tiles_per_seq
    keep_prev = jnp.where(i == 0, 0.0, 1.0)
    keep_next = jnp.where(i == tiles_per_seq - 1, 0.0, 1.0)
    hx = jnp.concatenate([hp_ref[...] * keep_prev, h_ref[...], hn_ref[...] * keep_next], axis=0)
    hn = _rms(hx, g_ref[...]).astype(jnp.bfloat16)
    pre = _bdot(hn, wqkv_ref[...])
    pre_ref[...] = pre
    pad = GDN_CONV // 2
    acc = None
    for j in range(GDN_CONV):
        term = pre_ref[pl.ds(HALO + j - pad, TM), :] * cw_ref[j:j + 1, :]
        acc = term if acc is None else acc + term
    qkv = _silu(acc)
    for hd in range(H):
        for off, ref, scale in ((0, q_ref, DK ** -0.5), (H * DK, k_ref, 1.0)):
            x = qkv[:, off + hd * DK:off + (hd + 1) * DK]
            ref[:, hd * DK:(hd + 1) * DK] = x * (lax.rsqrt(jnp.sum(x * x, axis=-1, keepdims=True) + NORM_EPS) * scale)
    v_ref[...] = qkv[:, 2 * H * DK:]
    hc = hn[HALO:HALO + TM]
    z_ref[...] = _bdot(hc, wz_ref[...])
    ba = _bdot(hc, wba_ref[...])
    beta_ref[...] = _sigmoid(ba[:, :2 * H])
    a = ba[:, 2 * H:] + dtb_ref[...]
    softplus = jnp.maximum(a, 0.0) + jnp.log(1.0 + jnp.exp(-jnp.abs(a)))
    gl_ref[...] = -jnp.exp(alog_ref[...]) * softplus


def _gdn_out_kernel(of_ref, ob_ref, z_ref, g_ref, w_ref, h_ref, o_ref):
    H, DV = GDN_HEADS, GDN_DV
    o = of_ref[...] + ob_ref[...]
    g = g_ref[...]
    parts = []
    for hd in range(H):
        x = o[:, hd * DV:(hd + 1) * DV]
        parts.append(_rms(x, g))
    y = jnp.concatenate(parts, axis=1) * _silu(z_ref[...])
    o_ref[...] = h_ref[...] + _bdot(y, w_ref[...])


def gdn_layer(h, norm_g, w_in, conv_w, A_log, dt_bias, out_norm_g, w_out):
    B, S, D = h.shape
    T = B * S
    H = GDN_HEADS
    HD = H * GDN_DK
    bf = jnp.bfloat16
    TM = GDN_ROW_TILE
    tiles_per_seq = S // TM
    hb = TM // HALO
    n_halo = T // HALO
    o = GDN_QKV_DIM + H * GDN_DV
    w_qkv = w_in[:, :GDN_QKV_DIM].astype(bf)
    w_z = w_in[:, GDN_QKV_DIM:o].astype(bf)
    w_ba = w_in[:, o:].astype(bf)
    cw = jnp.concatenate([conv_w, jnp.zeros((HALO - GDN_CONV, GDN_QKV_DIM), conv_w.dtype)], axis=0)
    row = lambda i: (i, 0)
    fixed = lambda i: (0, 0)
    f32 = jnp.float32
    ht = h.reshape(T, D)
    q, k, v, z, beta, gl = pl.pallas_call(
        functools.partial(_gdn_in_kernel, tiles_per_seq=tiles_per_seq),
        grid=(T // TM,),
        in_specs=[pl.BlockSpec((TM, D), row),
                  pl.BlockSpec((HALO, D), lambda i: (jnp.maximum(i * hb - 1, 0), 0)),
                  pl.BlockSpec((HALO, D), lambda i: (jnp.minimum((i + 1) * hb, n_halo - 1), 0)),
                  pl.BlockSpec((1, D), fixed),
                  pl.BlockSpec(w_qkv.shape, fixed), pl.BlockSpec(w_z.shape, fixed),
                  pl.BlockSpec(w_ba.shape, fixed), pl.BlockSpec(cw.shape, fixed),
                  pl.BlockSpec((1, 2 * H), fixed), pl.BlockSpec((1, 2 * H), fixed)],
        out_specs=[pl.BlockSpec((TM, HD), row), pl.BlockSpec((TM, HD), row), pl.BlockSpec((TM, HD), row),
                   pl.BlockSpec((TM, HD), row), pl.BlockSpec((TM, 2 * H), row), pl.BlockSpec((TM, 2 * H), row)],
        out_shape=[jax.ShapeDtypeStruct((T, HD), f32)] * 4 + [jax.ShapeDtypeStruct((T, 2 * H), f32)] * 2,
        scratch_shapes=[pltpu.VMEM((TM + 2 * HALO, GDN_QKV_DIM), f32)],
        compiler_params=pltpu.CompilerParams(vmem_limit_bytes=VMEM_LIMIT_BYTES),
        name="gdn_in_proj",
    )(ht, ht, ht, norm_g.reshape(1, D), w_qkv, w_z, w_ba, cw,
      A_log.reshape(1, 2 * H).astype(f32), dt_bias.reshape(1, 2 * H).astype(f32))

    o_fwd, o_bwd = gdn_bidirectional_delta(q.reshape(B, S, HD), k.reshape(B, S, HD), v.reshape(B, S, HD),
                                           beta.reshape(B, S, 2 * H), gl.reshape(B, S, 2 * H))

    h_new = pl.pallas_call(
        _gdn_out_kernel,
        grid=(T // ROW_TILE,),
        in_specs=[pl.BlockSpec((ROW_TILE, HD), row)] * 3 +
                 [pl.BlockSpec((1, GDN_DV), fixed), pl.BlockSpec((HD, D), fixed), pl.BlockSpec((ROW_TILE, D), row)],
        out_specs=pl.BlockSpec((ROW_TILE, D), row),
        out_shape=jax.ShapeDtypeStruct((T, D), f32),
        name="gdn_out_proj",
    )(o_fwd.reshape(T, HD), o_bwd.reshape(T, HD), z, out_norm_g.reshape(1, GDN_DV), w_out.astype(bf), ht)
    return h_new.reshape(B, S, D)


def _sigmoid(x):
    return 0.5 * jnp.tanh(0.5 * x) + 0.5


def _silu(x):
    return x * _sigmoid(x)


def _moe_router_kernel(h_ref, g_ref, w_ref, xn_ref, aff_ref):
    xn = _rms(h_ref[...], g_ref[...])
    xn_ref[...] = xn.astype(xn_ref.dtype)
    logits = _bdot(xn, w_ref[...])
    e = jnp.exp(logits - jnp.max(logits, axis=-1, keepdims=True))
    aff_ref[...] = e / jnp.sum(e, axis=-1, keepdims=True)


def _moe_ffn_kernel(x_ref, wg_ref, wu_ref, wd_ref, gate_ref, y_ref):
    x = x_ref[0]
    hid = _silu(_bdot(x, wg_ref[0])) * _bdot(x, wu_ref[0])
    y_ref[0] = (_bdot(hid, wd_ref[0]) * gate_ref[0]).astype(y_ref.dtype)


COMBINE_TILE = 256
COMBINE_SUBTILES = 2
STRIP_ROWS = 64
STRIP_ALIGN = 16
STRIPS_PER_DOT = 4


def _moe_combine_ple_kernel(base_ref, off_ref, cnt_ref,
                            h_ref, p_ref, g_ref, wg_ref, wp_ref, fg_ref, ye_ref, tok_ref,
                            o_ref, ybuf, tbuf, sems, yextra, textra, extra_sems, acc_ref,
                            *, n_tiles, total_rows, final_norm):
    E, TB, R, U = N_EXPERTS, COMBINE_TILE, STRIP_ROWS, COMBINE_SUBTILES
    K = STRIPS_PER_DOT * R
    n_steps = n_tiles // U
    j = pl.program_id(0)
    cur = j % 2

    def strip_copies(step, u, e, b):
        start = pl.multiple_of(base_ref[e * n_tiles + step * U + u], STRIP_ALIGN)
        rows = pl.ds((u * E + e) * R, R)
        return (pltpu.make_async_copy(ye_ref.at[pl.ds(start, R), :], ybuf.at[b, rows, :],
                                      sems.at[0, b, u * E + e]),
                pltpu.make_async_copy(tok_ref.at[pl.ds(start, R), :], tbuf.at[b, rows, :],
                                      sems.at[1, b, u * E + e]))

    def all_strip_copies(step, b):
        return [cp for u in range(U) for e in range(E) for cp in strip_copies(step, u, e, b)]

    @pl.when(j == 0)
    def _():
        for cp in all_strip_copies(0, 0):
            cp.start()

    @pl.when(j + 1 < n_steps)
    def _():
        for cp in all_strip_copies(j + 1, 1 - cur):
            cp.start()

    for cp in all_strip_copies(j, cur):
        cp.wait()

    lane = lax.broadcasted_iota(jnp.int32, (K, TB), 1)
    srow = lax.broadcasted_iota(jnp.int32, (R, 1), 0)
    groups = [(u, grp) for u in range(U) for grp in range(E // STRIPS_PER_DOT)]
    hots = []
    for u, grp in groups:
        tile = j * U + u
        wanted = []
        for e in range(grp * STRIPS_PER_DOT, (grp + 1) * STRIPS_PER_DOT):
            lo = off_ref[e * n_tiles + tile]
            wanted.append((srow >= lo) & (srow < lo + cnt_ref[e * n_tiles + tile]))
        rows = pl.ds((u * E + grp * STRIPS_PER_DOT) * R, K)
        tok = tbuf[cur, rows, 0:1] - tile * TB
        hots.append(jnp.where(jnp.concatenate(wanted, axis=0) & (lane == tok), 1.0, 0.0))
    parts = [_bdot(hot.T, ybuf[cur, pl.ds((u * E + grp * STRIPS_PER_DOT) * R, K), :])
             for hot, (u, grp) in zip(hots, groups)]
    for u in range(U):
        mine = [part for part, (uu, _) in zip(parts, groups) if uu == u]
        acc_ref[u] = functools.reduce(lambda a, b: a + b, mine)

    lane_r = lax.broadcasted_iota(jnp.int32, (R, TB), 1)
    for u in range(U):
        tile = j * U + u
        for e in range(E):
            base = base_ref[e * n_tiles + tile]
            first = base + off_ref[e * n_tiles + tile]
            end = first + cnt_ref[e * n_tiles + tile]
            n_sub = (end - base + R - 1) // R

            def body(k, carry, u=u, tile=tile, base=base, first=first, end=end):
                want0 = base + k * R
                start = pl.multiple_of(jnp.minimum(want0, total_rows - R), STRIP_ALIGN)
                cy = pltpu.make_async_copy(ye_ref.at[pl.ds(start, R), :], yextra, extra_sems.at[0])
                ct = pltpu.make_async_copy(tok_ref.at[pl.ds(start, R), :], textra, extra_sems.at[1])
                cy.start()
                ct.start()
                cy.wait()
                ct.wait()
                grow = srow + start
                wanted = (grow >= jnp.maximum(want0, first)) & (grow < jnp.minimum(want0 + R, end))
                hot_t = jnp.where(wanted & (lane_r == textra[:, 0:1] - tile * TB), 1.0, 0.0)
                acc_ref[u] += _bdot(hot_t.T, yextra[...])
                return carry

            lax.fori_loop(1, n_sub, body, 0)

    tiles = [pl.ds(u * TB, TB) for u in range(U)]
    hm = [h_ref[t, :] + acc_ref[u] for u, t in enumerate(tiles)]
    gate = [_sigmoid(_bdot(_rms(x, g_ref[...]), wg_ref[...])) for x in hm]
    out = [x + _bdot(p_ref[t, :], wp_ref[...]) * gt for x, gt, t in zip(hm, gate, tiles)]
    for x, t in zip(out, tiles):
        o_ref[t, :] = _rms(x, fg_ref[...]) if final_norm else x


F32_INF_BITS = 0x7F800000


def _topk_select_kernel(aff_ref, wcum_ref, *, cap):
    a = aff_ref[0]
    R = a.shape[0]
    bits = lax.bitcast_convert_type(a, jnp.int32)

    def count(mask):
        m = mask.astype(jnp.float32)
        return jnp.sum(jnp.sum(m, axis=0, keepdims=True), axis=1, keepdims=True)

    def bisect(_, carry):
        lo, hi = carry
        mid = lo + jnp.right_shift(hi - lo + 1, 1)
        ok = count(bits >= mid) >= cap
        return jnp.where(ok, mid, lo), jnp.where(ok, hi, mid - 1)

    thr, _ = lax.fori_loop(0, 31, bisect, (jnp.zeros((1, 1), jnp.int32),
                                           jnp.full((1, 1), F32_INF_BITS, jnp.int32)))
    gt = bits > thr
    eq = bits == thr
    need = cap - count(gt)

    lane_r = lax.broadcasted_iota(jnp.int32, (LANES, LANES), 0)
    lane_c = lax.broadcasted_iota(jnp.int32, (LANES, LANES), 1)
    upper = (lane_r <= lane_c).astype(jnp.bfloat16)
    row_r = lax.broadcasted_iota(jnp.int32, (R, R), 0)
    row_c = lax.broadcasted_iota(jnp.int32, (R, R), 1)
    earlier_rows = (row_r > row_c).astype(jnp.bfloat16)

    eq_in_row = jnp.dot(eq.astype(jnp.bfloat16), upper, preferred_element_type=jnp.float32)
    eq_row_total = jnp.broadcast_to(eq_in_row[:, LANES - 1:LANES], (R, LANES))
    eq_before = jnp.dot(earlier_rows, eq_row_total.astype(jnp.bfloat16), preferred_element_type=jnp.float32)
    sel = gt | (eq & (eq_before + eq_in_row <= need))
    wcum = jnp.dot(sel.astype(jnp.bfloat16), upper, preferred_element_type=jnp.float32)
    wcum_ref[0] = wcum.astype(jnp.int32)


SLOT_TILE = 512


def _slot_token_kernel(row_end_ref, wcum_ref, tok_ref):
    SB = tok_ref.shape[1]
    row_end = row_end_ref[0]
    R = row_end.shape[1]
    slot = pl.program_id(1) * SB + lax.broadcasted_iota(jnp.int32, (SB, 1), 0)
    done = row_end <= slot
    row = jnp.sum(done.astype(jnp.int32), axis=1, keepdims=True)
    row_start = jnp.max(jnp.where(done, row_end, 0), axis=1, keepdims=True)
    one_hot = (lax.broadcasted_iota(jnp.int32, (SB, R), 1) == row).astype(jnp.bfloat16)
    wrow = jnp.dot(one_hot, wcum_ref[0].astype(jnp.float32).astype(jnp.bfloat16),
                   preferred_element_type=jnp.float32)
    local = (slot - row_start).astype(jnp.float32)
    lane = jnp.sum((wrow <= local).astype(jnp.int32), axis=1, keepdims=True)
    tok_ref[0] = jnp.broadcast_to(row * LANES + lane, (SB, LANES))


def expert_choice_indices(aff_t, cap):
    E, T = aff_t.shape
    R = T // LANES
    wcum = pl.pallas_call(
        functools.partial(_topk_select_kernel, cap=cap),
        grid=(E,),
        in_specs=[pl.BlockSpec((1, R, LANES), lambda e: (e, 0, 0))],
        out_specs=pl.BlockSpec((1, R, LANES), lambda e: (e, 0, 0)),
        out_shape=jax.ShapeDtypeStruct((E, R, LANES), jnp.int32),
        name="moe_topk_select",
    )(aff_t.reshape(E, R, LANES))
    row_end = jnp.cumsum(wcum[:, :, LANES - 1], axis=1)
    row_start = row_end - wcum[:, :, LANES - 1]
    tok = pl.pallas_call(
        _slot_token_kernel,
        grid=(E, cap // SLOT_TILE),
        in_specs=[pl.BlockSpec((1, 1, R), lambda e, j: (e, 0, 0)),
                  pl.BlockSpec((1, R, LANES), lambda e, j: (e, 0, 0))],
        out_specs=pl.BlockSpec((1, SLOT_TILE, LANES), lambda e, j: (e, j, 0)),
        out_shape=jax.ShapeDtypeStruct((E, cap, LANES), jnp.int32),
        name="moe_slot_token",
    )(row_end.reshape(E, 1, R), wcum)
    return tok, row_start


def moe_ple_layer(h, p, norm_g, w_router, w_gate, w_up, w_down, ple_norm_g, ple_w_gate, ple_w_proj,
                  final_g, final_norm):
    T, D = h.shape
    E, F = N_EXPERTS, w_gate.shape[-1]
    cap = EC_CAPACITY_FACTOR * T // E
    bf = jnp.bfloat16
    row = lambda i: (i, 0)
    fixed = lambda i: (0, 0)
    xn, aff = pl.pallas_call(
        _moe_router_kernel,
        grid=(T // ROW_TILE,),
        in_specs=[pl.BlockSpec((ROW_TILE, D), row), pl.BlockSpec((1, D), fixed),
                  pl.BlockSpec((D, E), fixed)],
        out_specs=[pl.BlockSpec((ROW_TILE, D), row), pl.BlockSpec((ROW_TILE, E), row)],
        out_shape=[jax.ShapeDtypeStruct((T, D), bf), jax.ShapeDtypeStruct((T, E), jnp.float32)],
        name="moe_router",
    )(h, norm_g.reshape(1, D), w_router.astype(bf))
    aff_t = aff.T
    tok, row_start = expert_choice_indices(aff_t, cap)
    idx = tok[:, :, 0]
    gate = jnp.take_along_axis(aff_t, idx, axis=1)
    xe = xn.at[idx].get(mode='promise_in_bounds')
    ye = pl.pallas_call(
        _moe_ffn_kernel,
        grid=(E, cap // FFN_ROW_TILE),
        in_specs=[pl.BlockSpec((1, FFN_ROW_TILE, D), lambda e, i: (e, i, 0)),
                  pl.BlockSpec((1, D, F), lambda e, i: (e, 0, 0)),
                  pl.BlockSpec((1, D, F), lambda e, i: (e, 0, 0)),
                  pl.BlockSpec((1, F, D), lambda e, i: (e, 0, 0)),
                  pl.BlockSpec((1, FFN_ROW_TILE, 1), lambda e, i: (e, i, 0))],
        out_specs=pl.BlockSpec((1, FFN_ROW_TILE, D), lambda e, i: (e, i, 0)),
        out_shape=jax.ShapeDtypeStruct((E, cap, D), bf),
        compiler_params=pltpu.CompilerParams(dimension_semantics=("parallel", "arbitrary"),
                                             vmem_limit_bytes=VMEM_LIMIT_BYTES),
        name="moe_expert_ffn",
    )(xe, w_gate.astype(bf), w_up.astype(bf), w_down.astype(bf), gate[..., None])

    TB, R, U = COMBINE_TILE, STRIP_ROWS, COMBINE_SUBTILES
    n_tiles = T // TB
    total_rows = E * cap
    lo = jnp.concatenate([row_start[:, ::TB // LANES], jnp.full((E, 1), cap, jnp.int32)], axis=1)
    first = lo[:, :-1] + (jnp.arange(E, dtype=jnp.int32) * cap)[:, None]
    cnt = lo[:, 1:] - lo[:, :-1]
    base = jnp.minimum(first // STRIP_ALIGN * STRIP_ALIGN, total_rows - R)
    off = first - base

    P = p.shape[1]
    tile = lambda i, *_: (i, 0)
    const = lambda i, *_: (0, 0)
    return pl.pallas_call(
        functools.partial(_moe_combine_ple_kernel, n_tiles=n_tiles, total_rows=total_rows,
                          final_norm=final_norm),
        grid_spec=pltpu.PrefetchScalarGridSpec(
            num_scalar_prefetch=3,
            grid=(n_tiles // U,),
            in_specs=[pl.BlockSpec((U * TB, D), tile), pl.BlockSpec((U * TB, P), tile),
                      pl.BlockSpec((1, D), const), pl.BlockSpec((D, D), const), pl.BlockSpec((P, D), const),
                      pl.BlockSpec((1, D), const),
                      pl.BlockSpec(memory_space=pl.ANY), pl.BlockSpec(memory_space=pl.ANY)],
            out_specs=pl.BlockSpec((U * TB, D), tile),
            scratch_shapes=[pltpu.VMEM((2, U * E * R, D), bf), pltpu.VMEM((2, U * E * R, LANES), jnp.int32),
                            pltpu.SemaphoreType.DMA((2, 2, U * E)),
                            pltpu.VMEM((R, D), bf), pltpu.VMEM((R, LANES), jnp.int32),
                            pltpu.SemaphoreType.DMA((2,)),
                            pltpu.VMEM((U, TB, D), jnp.float32)]),
        out_shape=jax.ShapeDtypeStruct((T, D), jnp.float32),
        compiler_params=pltpu.CompilerParams(dimension_semantics=("arbitrary",),
                                             vmem_limit_bytes=VMEM_LIMIT_BYTES),
        name="moe_combine_ple",
    )(base.reshape(-1), off.reshape(-1), cnt.reshape(-1),
      h, p, ple_norm_g.reshape(1, D), ple_w_gate.astype(bf), ple_w_proj.astype(bf), final_g.reshape(1, D),
      ye.reshape(total_rows, D), tok.reshape(total_rows, LANES))


def encoder_trunk(x, p, W):
    h = x
    for i in range(DEPTH):
        j = i // N_MIXERS
        if i % N_MIXERS == 0:
            h = mla_layer(h, W['norm_mix_g'][i], W['mla_w_in'][j], W['mla_q_norm_g'][j], W['mla_w_qb'][j],
                          W['mla_kv_norm_g'][j], W['mla_w_kvb'][j], W['mla_w_out'][j])
        else:
            h = gdn_layer(h, W['norm_mix_g'][i], W['gdn_w_in'][j], W['gdn_conv_w'][j], W['gdn_A_log'][j],
                          W['gdn_dt_bias'][j], W['gdn_norm_g'][j], W['gdn_w_out'][j])
        B, S, D = h.shape
        ht = moe_ple_layer(h.reshape(B * S, D), p[i].reshape(B * S, PLE_DIM), W['norm_ffn_g'][i],
                           W['moe_w_router'][i], W['moe_w_gate'][i], W['moe_w_up'][i], W['moe_w_down'][i],
                           W['norm_ple_g'][i], W['ple_w_gate'][i], W['ple_w_proj'][i],
                           W['final_norm_g'], final_norm=(i == DEPTH - 1))
        h = ht.reshape(B, S, D)
    return h


def kernel(x_prompt, x_sample, p_prompt, p_sample, norm_mix_g, norm_ffn_g, norm_ple_g, final_norm_g,
           mla_w_in, mla_q_norm_g, mla_w_qb, mla_kv_norm_g, mla_w_kvb, mla_w_out,
           gdn_w_in, gdn_conv_w, gdn_A_log, gdn_dt_bias, gdn_norm_g, gdn_w_out,
           moe_w_router, moe_w_gate, moe_w_up, moe_w_down, ple_w_proj, ple_w_gate):
    W = {
        'norm_mix_g': norm_mix_g, 'norm_ffn_g': norm_ffn_g, 'norm_ple_g': norm_ple_g,
        'final_norm_g': final_norm_g,
        'mla_w_in': mla_w_in, 'mla_q_norm_g': mla_q_norm_g, 'mla_w_qb': mla_w_qb,
        'mla_kv_norm_g': mla_kv_norm_g, 'mla_w_kvb': mla_w_kvb, 'mla_w_out': mla_w_out,
        'gdn_w_in': gdn_w_in, 'gdn_conv_w': gdn_conv_w, 'gdn_A_log': gdn_A_log,
        'gdn_dt_bias': gdn_dt_bias, 'gdn_norm_g': gdn_norm_g, 'gdn_w_out': gdn_w_out,
        'moe_w_router': moe_w_router, 'moe_w_gate': moe_w_gate, 'moe_w_up': moe_w_up,
        'moe_w_down': moe_w_down, 'ple_w_proj': ple_w_proj, 'ple_w_gate': ple_w_gate,
    }
    y_prompt = encoder_trunk(x_prompt, p_prompt, W)
    y_sample = encoder_trunk(x_sample, p_sample, W)
    return (y_prompt, y_sample)
```

```python
import functools

import jax
import jax.numpy as jnp
from jax import lax
from jax.experimental import pallas as pl
from jax.experimental.pallas import tpu as pltpu

D_MODEL = 1024
DEPTH = 4
PLE_DIM = 256
N_MIXERS = 2
MLA_HEADS = 8
MLA_NOPE_DIM = 128
MLA_ROPE_DIM = 64
MLA_V_DIM = 128
MLA_Q_LORA = 256
MLA_KV_LORA = 128
ROPE_THETA = 10000.0
Q_BLOCK = 128
GDN_HEADS = 8
GDN_DK = 128
GDN_DV = 128
GDN_CONV = 5
GDN_CHUNK = 64
GDN_QKV_DIM = 2 * GDN_HEADS * GDN_DK + GDN_HEADS * GDN_DV
N_EXPERTS = 16
EC_CAPACITY_FACTOR = 2
NORM_EPS = 1e-6


def rope_tables(seq):
    pos = jnp.arange(seq, dtype=jnp.float32)
    inv = ROPE_THETA ** (-jnp.arange(0, MLA_ROPE_DIM, 2, dtype=jnp.float32) / MLA_ROPE_DIM)
    ang = pos[:, None] * inv[None, :]
    return jnp.cos(ang), jnp.sin(ang)


LANES = 128
ROW_TILE = 512
FFN_ROW_TILE = 1024
GDN_CHUNKS_PER_STEP = 2
ATTN_Q_TILE = 512
ATTN_HEADS_PER_STEP = 4
VMEM_LIMIT_BYTES = 56 * 1024 * 1024
MLA_QK_PAD = 2 * LANES


def _rms(x, g):
    return x * lax.rsqrt(jnp.mean(x * x, axis=-1, keepdims=True) + NORM_EPS) * g


def _mla_proj_kernel(h_ref, g_ref, w_in_ref, qg_ref, w_qb_ref, kvg_ref, w_k_ref, w_v_ref,
                     cos_ref, sin_ref, q_ref, k_ref, v_ref):
    H, NP = MLA_HEADS, MLA_NOPE_DIM
    scale = (MLA_NOPE_DIM + MLA_ROPE_DIM) ** -0.5
    hn = _rms(h_ref[...], g_ref[...])
    lat = _bdot(hn, w_in_ref[...])
    q_lat = lat[:, :MLA_Q_LORA]
    kv_lat = lat[:, MLA_Q_LORA:MLA_Q_LORA + MLA_KV_LORA]
    o = MLA_Q_LORA + MLA_KV_LORA
    cos, sin = cos_ref[...], sin_ref[...]
    k_rope = (lat[:, o:o + LANES] * cos + lat[:, o + LANES:o + 2 * LANES] * sin).astype(jnp.bfloat16)
    q = _bdot(_rms(q_lat, qg_ref[...]), w_qb_ref[...]) * scale
    kvn = _rms(kv_lat, kvg_ref[...])
    k_nope = _bdot(kvn, w_k_ref[...]).astype(jnp.bfloat16)
    v_ref[...] = _bdot(kvn, w_v_ref[...]).astype(jnp.bfloat16)
    for hd in range(H):
        a = q[:, (H + hd) * NP:(H + hd + 1) * NP]
        b = q[:, (2 * H + hd) * NP:(2 * H + hd + 1) * NP]
        q_ref[:, hd * MLA_QK_PAD:hd * MLA_QK_PAD + NP] = q[:, hd * NP:(hd + 1) * NP].astype(jnp.bfloat16)
        q_ref[:, hd * MLA_QK_PAD + NP:(hd + 1) * MLA_QK_PAD] = (a * cos + b * sin).astype(jnp.bfloat16)
        k_ref[:, hd * MLA_QK_PAD:hd * MLA_QK_PAD + NP] = k_nope[:, hd * NP:(hd + 1) * NP]
        k_ref[:, hd * MLA_QK_PAD + NP:(hd + 1) * MLA_QK_PAD] = k_rope


def _mla_attn_kernel(q_ref, k_ref, vt_ref, o_ref):
    heads = range(ATTN_HEADS_PER_STEP)
    qk = lambda hd: slice(hd * MLA_QK_PAD, (hd + 1) * MLA_QK_PAD)
    vv = lambda hd: slice(hd * MLA_V_DIM, (hd + 1) * MLA_V_DIM)
    s_t = [_bdot_nt(k_ref[0, :, qk(hd)], q_ref[0, :, qk(hd)]) for hd in heads]
    p_t = [jnp.exp(x - jnp.max(x, axis=0, keepdims=True)) for x in s_t]
    denom = [jnp.sum(x, axis=0, keepdims=True) for x in p_t]
    o_t = [_bdot(vt_ref[0, vv(hd), :], p_t[hd]) / denom[hd] for hd in heads]
    for hd in heads:
        o_ref[0, :, vv(hd)] = o_t[hd].T.astype(o_ref.dtype)


def _matmul_residual_kernel(x_ref, w_ref, h_ref, o_ref):
    o_ref[...] = h_ref[...] + _bdot(x_ref[...], w_ref[...])


def matmul_residual(x, w, h, name):
    T, K = x.shape
    D = w.shape[1]
    return pl.pallas_call(
        _matmul_residual_kernel,
        grid=(T // ROW_TILE,),
        in_specs=[pl.BlockSpec((ROW_TILE, K), lambda i: (i, 0)),
                  pl.BlockSpec((K, D), lambda i: (0, 0)),
                  pl.BlockSpec((ROW_TILE, D), lambda i: (i, 0))],
        out_specs=pl.BlockSpec((ROW_TILE, D), lambda i: (i, 0)),
        out_shape=jax.ShapeDtypeStruct((T, D), jnp.float32),
        name=name,
    )(x, w, h)


def _rope_swapped_columns(w):
    half = MLA_ROPE_DIM // 2
    pad = jnp.zeros(w.shape[:-1] + (LANES - MLA_ROPE_DIM,), w.dtype)
    plain = jnp.concatenate([w, pad], axis=-1)
    swapped = jnp.concatenate([w[..., half:], w[..., :half], pad], axis=-1)
    return plain, swapped


def mla_layer(h, norm_g, w_in, q_norm_g, w_qb, kv_norm_g, w_kvb, w_out):
    B, S, D = h.shape
    T = B * S
    H, NP, R = MLA_HEADS, MLA_NOPE_DIM, MLA_ROPE_DIM
    bf = jnp.bfloat16
    o = MLA_Q_LORA + MLA_KV_LORA
    kr_plain, kr_swapped = _rope_swapped_columns(w_in[:, o:])
    w_in_ext = jnp.concatenate([w_in[:, :o], kr_plain, kr_swapped], axis=1).astype(bf)
    w_qb_h = w_qb.reshape(MLA_Q_LORA, H, NP + R)
    qr_plain, qr_swapped = _rope_swapped_columns(w_qb_h[:, :, NP:])
    w_qb_ext = jnp.concatenate([w_qb_h[:, :, :NP].reshape(MLA_Q_LORA, H * NP),
                                qr_plain.reshape(MLA_Q_LORA, H * LANES),
                                qr_swapped.reshape(MLA_Q_LORA, H * LANES)], axis=1).astype(bf)
    w_kvb_h = w_kvb.reshape(MLA_KV_LORA, H, NP + MLA_V_DIM)
    w_k = w_kvb_h[:, :, :NP].reshape(MLA_KV_LORA, H * NP).astype(bf)
    w_v = w_kvb_h[:, :, NP:].reshape(MLA_KV_LORA, H * MLA_V_DIM).astype(bf)
    cos, sin = rope_tables(S)
    zeros = jnp.zeros((S, LANES - R), jnp.float32)
    cos_pad = jnp.concatenate([cos, cos, zeros], axis=1)
    sin_pad = jnp.concatenate([-sin, sin, zeros], axis=1)

    tiles_per_seq = S // ROW_TILE
    row = lambda i: (i, 0)
    fixed = lambda i: (0, 0)
    pos = lambda i: (i % tiles_per_seq, 0)
    q_cat, k_cat, v = pl.pallas_call(
        _mla_proj_kernel,
        grid=(T // ROW_TILE,),
        in_specs=[pl.BlockSpec((ROW_TILE, D), row),
                  pl.BlockSpec((1, D), fixed),
                  pl.BlockSpec(w_in_ext.shape, fixed),
                  pl.BlockSpec((1, MLA_Q_LORA), fixed),
                  pl.BlockSpec(w_qb_ext.shape, fixed),
                  pl.BlockSpec((1, MLA_KV_LORA), fixed),
                  pl.BlockSpec(w_k.shape, fixed),
                  pl.BlockSpec(w_v.shape, fixed),
                  pl.BlockSpec((ROW_TILE, LANES), pos),
                  pl.BlockSpec((ROW_TILE, LANES), pos)],
        out_specs=[pl.BlockSpec((ROW_TILE, H * MLA_QK_PAD), row),
                   pl.BlockSpec((ROW_TILE, H * MLA_QK_PAD), row),
                   pl.BlockSpec((ROW_TILE, H * MLA_V_DIM), row)],
        out_shape=[jax.ShapeDtypeStruct((T, H * MLA_QK_PAD), bf),
                   jax.ShapeDtypeStruct((T, H * MLA_QK_PAD), bf),
                   jax.ShapeDtypeStruct((T, H * MLA_V_DIM), bf)],
        name="mla_proj",
    )(h.reshape(T, D), norm_g.reshape(1, D), w_in_ext, q_norm_g.reshape(1, -1), w_qb_ext,
      kv_norm_g.reshape(1, -1), w_k, w_v, cos_pad, sin_pad)

    G = ATTN_HEADS_PER_STEP
    o_attn = pl.pallas_call(
        _mla_attn_kernel,
        grid=(B, H // G, S // ATTN_Q_TILE),
        in_specs=[pl.BlockSpec((1, ATTN_Q_TILE, G * MLA_QK_PAD), lambda b, hd, i: (b, i, hd)),
                  pl.BlockSpec((1, S, G * MLA_QK_PAD), lambda b, hd, i: (b, 0, hd)),
                  pl.BlockSpec((1, G * MLA_V_DIM, S), lambda b, hd, i: (b, hd, 0))],
        out_specs=pl.BlockSpec((1, ATTN_Q_TILE, G * MLA_V_DIM), lambda b, hd, i: (b, i, hd)),
        out_shape=jax.ShapeDtypeStruct((B, S, H * MLA_V_DIM), bf),
        compiler_params=pltpu.CompilerParams(
            dimension_semantics=("parallel", "parallel", "arbitrary"),
            vmem_limit_bytes=VMEM_LIMIT_BYTES),
        name="mla_attention",
    )(q_cat.reshape(B, S, -1), k_cat.reshape(B, S, -1), v.reshape(B, S, -1).transpose(0, 2, 1))

    h_new = matmul_residual(o_attn.reshape(T, H * MLA_V_DIM), w_out.astype(bf), h.reshape(T, D),
                            name="mla_out_proj")
    return h_new.reshape(B, S, D)


def _bdot(a, b):
    return jnp.dot(a.astype(jnp.bfloat16), b.astype(jnp.bfloat16), preferred_element_type=jnp.float32)


def _bdot_nt(a, b):
    return lax.dot_general(a.astype(jnp.bfloat16), b.astype(jnp.bfloat16),
                           (((1,), (1,)), ((), ())), preferred_element_type=jnp.float32)


def _unit_triangular_inverses(Ls, row, col):
    C = Ls[0].shape[0]
    eye = (row == col).astype(jnp.float32)

    def same_block(s):
        return (row // s) == (col // s)

    blk8 = same_block(8)
    L8 = [jnp.where(blk8, L, 0.0) for L in Ls]
    P2 = [_bdot(a, a) for a in L8]
    P4 = [_bdot(a, a) for a in P2]
    X = [eye - a for a in L8]
    X = [x + _bdot(x, p) for x, p in zip(X, P2)]
    X = [x + _bdot(x, p) for x, p in zip(X, P4)]
    s = 8
    while s < C:
        pair = same_block(2 * s) & jnp.logical_not(same_block(s))
        XC = [_bdot(x, jnp.where(pair, L, 0.0)) for x, L in zip(X, Ls)]
        X = [x - _bdot(xc, x) for x, xc in zip(X, XC)]
        s *= 2
    return X


def _gdn_chunk_kernel(qf_ref, kf_ref, vf_ref, gcf_ref, bcf_ref, grf_ref,
                      qb_ref, kb_ref, vb_ref, gcb_ref, bcb_ref, grb_ref,
                      of_ref, ob_ref, state_ref):
    H, C, DK, DV = GDN_HEADS, GDN_CHUNK, GDN_DK, GDN_DV

    @pl.when(pl.program_id(1) == 0)
    def _():
        state_ref[...] = jnp.zeros_like(state_ref)

    row = lax.broadcasted_iota(jnp.int32, (C, C), 0)
    col = lax.broadcasted_iota(jnp.int32, (C, C), 1)
    dirs = ((0, qf_ref, kf_ref, vf_ref, gcf_ref, bcf_ref, grf_ref, of_ref),
            (1, qb_ref, kb_ref, vb_ref, gcb_ref, bcb_ref, grb_ref, ob_ref))
    chains = [[] for _ in range(GDN_CHUNKS_PER_STEP)]
    for d, q_ref, k_ref, v_ref, gc_ref, bc_ref, gr_ref, o_ref in dirs:
        if d == 0:
            incl, strict, last = row >= col, row > col, C - 1
        else:
            incl, strict, last = row <= col, row < col, 0
        tri = incl.astype(jnp.float32)
        for i in range(GDN_CHUNKS_PER_STEP):
            sub = i if d == 0 else GDN_CHUNKS_PER_STEP - 1 - i
            rows = slice(sub * C, (sub + 1) * C)
            gc_cols = jnp.dot(tri, gc_ref[0, rows, :], precision=lax.Precision.HIGHEST,
                              preferred_element_type=jnp.float32)
            gc_rows = jnp.dot(gr_ref[0, sub], tri.T, precision=lax.Precision.HIGHEST,
                              preferred_element_type=jnp.float32)
            beta_cols = bc_ref[0, rows, :]
            for h in range(H):
                c = d * H + h
                chains[i].append(dict(
                    c=c, incl=incl, strict=strict, o_ref=o_ref, rows=rows, sl=slice(h * DK, (h + 1) * DK),
                    q_ref=q_ref, k_ref=k_ref, v_ref=v_ref,
                    beta=beta_cols[:, c:c + 1], gcc=gc_cols[:, c:c + 1], gcr=gc_rows[c:c + 1, :],
                    gl=gc_cols[last:last + 1, c:c + 1]))

    every = [ch for group in chains for ch in group]
    for ch in every:
        ch['decay'] = jnp.where(ch['incl'], jnp.exp(jnp.where(ch['incl'], ch['gcc'] - ch['gcr'], 0.0)), 0.0)
        ch['eg'] = jnp.exp(ch['gcc'])
    for ch in every:
        k = ch['k_ref'][0, ch['rows'], ch['sl']]
        q = ch['q_ref'][0, ch['rows'], ch['sl']]
        kb = k * ch['beta']
        ch['kbeg'] = kb * ch['eg']
        ch['s1'] = _bdot_nt(jnp.concatenate([kb, q], axis=0), k)
    Ls = [jnp.where(ch['strict'], ch['s1'][:C] * ch['decay'], 0.0) for ch in every]
    Ts = _unit_triangular_inverses(Ls, row, col)
    for ch, T in zip(every, Ts):
        v = ch['v_ref'][0, ch['rows'], ch['sl']]
        ch['uw'] = _bdot(T, jnp.concatenate([v * ch['beta'], ch['kbeg']], axis=1))

    for group in chains:
        for ch in group:
            q = ch['q_ref'][0, ch['rows'], ch['sl']]
            w = ch['uw'][:, DV:]
            ch['ws'] = _bdot(jnp.concatenate([w, q * ch['eg']], axis=0), state_ref[ch['c']])
        for ch in group:
            ch['v_new'] = ch['uw'][:, :DV] - ch['ws'][:C]
            a_intra = ch['s1'][C:] * ch['decay']
            ch['o_ref'][0, ch['rows'], ch['sl']] = ch['ws'][C:] + _bdot(a_intra, ch['v_new'])
        for ch in group:
            k = ch['k_ref'][0, ch['rows'], ch['sl']]
            k_g = k * jnp.exp(ch['gl'] - ch['gcc'])
            c = ch['c']
            state_ref[c] = state_ref[c] * jnp.exp(ch['gl']) + _bdot(k_g.T, ch['v_new'])


def gdn_bidirectional_delta(q, k, v, beta, g):
    B, S, HD = q.shape
    C, H, P = GDN_CHUNK, GDN_HEADS, GDN_CHUNKS_PER_STEP
    N = S // (C * P)
    g_rows = g.reshape(B, N * P, C, 2 * H).transpose(0, 1, 3, 2)

    def fwd3(b, n):
        return (b, n, 0)

    def bwd3(b, n):
        return (b, N - 1 - n, 0)

    def fwd4(b, n):
        return (b, n, 0, 0)

    def bwd4(b, n):
        return (b, N - 1 - n, 0, 0)

    def specs(i3, i4):
        big = pl.BlockSpec((1, P * C, HD), i3)
        small = pl.BlockSpec((1, P * C, 2 * H), i3)
        return [big, big, big, small, small, pl.BlockSpec((1, P, 2 * H, C), i4)]

    out = jax.ShapeDtypeStruct((B, S, HD), jnp.float32)
    return pl.pallas_call(
        _gdn_chunk_kernel,
        grid=(B, N),
        in_specs=specs(fwd3, fwd4) + specs(bwd3, bwd4),
        out_specs=[pl.BlockSpec((1, P * C, HD), fwd3), pl.BlockSpec((1, P * C, HD), bwd3)],
        out_shape=[out, out],
        scratch_shapes=[pltpu.VMEM((2 * H, GDN_DK, GDN_DV), jnp.float32)],
        compiler_params=pltpu.CompilerParams(dimension_semantics=("parallel", "arbitrary")),
        name="gdn_chunk_delta",
    )(q, k, v, g, beta, g_rows, q, k, v, g, beta, g_rows)


GDN_ROW_TILE = 512
HALO = 8


def _gdn_in_kernel(h_ref, hp_ref, hn_ref, g_ref, wqkv_ref, wz_ref, wba_ref, cw_ref, alog_ref, dtb_ref,
                   q_ref, k_ref, v_ref, z_ref, beta_ref, gl_ref, pre_ref, *, tiles_per_seq):
    H, DK = GDN_HEADS, GDN_DK
    TM = h_ref.shape[0]
    i = pl.program_id(0) % tiles_per_seq
    keep_prev = jnp.where(i == 0, 0.0, 1.0)
    keep_next = jnp.where(i == tiles_per_seq - 1, 0.0, 1.0)
    hx = jnp.concatenate([hp_ref[...] * keep_prev, h_ref[...], hn_ref[...] * keep_next], axis=0)
    hn = _rms(hx, g_ref[...]).astype(jnp.bfloat16)
    pre = _bdot(hn, wqkv_ref[...])
    pre_ref[...] = pre
    pad = GDN_CONV // 2
    acc = None
    for j in range(GDN_CONV):
        term = pre_ref[pl.ds(HALO + j - pad, TM), :] * cw_ref[j:j + 1, :]
        acc = term if acc is None else acc + term
    qkv = _silu(acc)
    for hd in range(H):
        for off, ref, scale in ((0, q_ref, DK ** -0.5), (H * DK, k_ref, 1.0)):
            x = qkv[:, off + hd * DK:off + (hd + 1) * DK]
            ref[:, hd * DK:(hd + 1) * DK] = x * (lax.rsqrt(jnp.sum(x * x, axis=-1, keepdims=True) + NORM_EPS) * scale)
    v_ref[...] = qkv[:, 2 * H * DK:]
    hc = hn[HALO:HALO + TM]
    z_ref[...] = _bdot(hc, wz_ref[...])
    ba = _bdot(hc, wba_ref[...])
    beta_ref[...] = _sigmoid(ba[:, :2 * H])
    a = ba[:, 2 * H:] + dtb_ref[...]
    softplus = jnp.maximum(a, 0.0) + jnp.log(1.0 + jnp.exp(-jnp.abs(a)))
    gl_ref[...] = -jnp.exp(alog_ref[...]) * softplus


def _gdn_out_kernel(of_ref, ob_ref, z_ref, g_ref, w_ref, h_ref, o_ref):
    H, DV = GDN_HEADS, GDN_DV
    o = of_ref[...] + ob_ref[...]
    g = g_ref[...]
    parts = []
    for hd in range(H):
        x = o[:, hd * DV:(hd + 1) * DV]
        parts.append(_rms(x, g))
    y = jnp.concatenate(parts, axis=1) * _silu(z_ref[...])
    o_ref[...] = h_ref[...] + _bdot(y, w_ref[...])


def gdn_layer(h, norm_g, w_in, conv_w, A_log, dt_bias, out_norm_g, w_out):
    B, S, D = h.shape
    T = B * S
    H = GDN_HEADS
    HD = H * GDN_DK
    bf = jnp.bfloat16
    TM = GDN_ROW_TILE
    tiles_per_seq = S // TM
    hb = TM // HALO
    n_halo = T // HALO
    o = GDN_QKV_DIM + H * GDN_DV
    w_qkv = w_in[:, :GDN_QKV_DIM].astype(bf)
    w_z = w_in[:, GDN_QKV_DIM:o].astype(bf)
    w_ba = w_in[:, o:].astype(bf)
    cw = jnp.concatenate([conv_w, jnp.zeros((HALO - GDN_CONV, GDN_QKV_DIM), conv_w.dtype)], axis=0)
    row = lambda i: (i, 0)
    fixed = lambda i: (0, 0)
    f32 = jnp.float32
    ht = h.reshape(T, D)
    q, k, v, z, beta, gl = pl.pallas_call(
        functools.partial(_gdn_in_kernel, tiles_per_seq=tiles_per_seq),
        grid=(T // TM,),
        in_specs=[pl.BlockSpec((TM, D), row),
                  pl.BlockSpec((HALO, D), lambda i: (jnp.maximum(i * hb - 1, 0), 0)),
                  pl.BlockSpec((HALO, D), lambda i: (jnp.minimum((i + 1) * hb, n_halo - 1), 0)),
                  pl.BlockSpec((1, D), fixed),
                  pl.BlockSpec(w_qkv.shape, fixed), pl.BlockSpec(w_z.shape, fixed),
                  pl.BlockSpec(w_ba.shape, fixed), pl.BlockSpec(cw.shape, fixed),
                  pl.BlockSpec((1, 2 * H), fixed), pl.BlockSpec((1, 2 * H), fixed)],
        out_specs=[pl.BlockSpec((TM, HD), row), pl.BlockSpec((TM, HD), row), pl.BlockSpec((TM, HD), row),
                   pl.BlockSpec((TM, HD), row), pl.BlockSpec((TM, 2 * H), row), pl.BlockSpec((TM, 2 * H), row)],
        out_shape=[jax.ShapeDtypeStruct((T, HD), f32)] * 4 + [jax.ShapeDtypeStruct((T, 2 * H), f32)] * 2,
        scratch_shapes=[pltpu.VMEM((TM + 2 * HALO, GDN_QKV_DIM), f32)],
        compiler_params=pltpu.CompilerParams(vmem_limit_bytes=VMEM_LIMIT_BYTES),
        name="gdn_in_proj",
    )(ht, ht, ht, norm_g.reshape(1, D), w_qkv, w_z, w_ba, cw,
      A_log.reshape(1, 2 * H).astype(f32), dt_bias.reshape(1, 2 * H).astype(f32))

    o_fwd, o_bwd = gdn_bidirectional_delta(q.reshape(B, S, HD), k.reshape(B, S, HD), v.reshape(B, S, HD),
                                           beta.reshape(B, S, 2 * H), gl.reshape(B, S, 2 * H))

    h_new = pl.pallas_call(
        _gdn_out_kernel,
        grid=(T // ROW_TILE,),
        in_specs=[pl.BlockSpec((ROW_TILE, HD), row)] * 3 +
                 [pl.BlockSpec((1, GDN_DV), fixed), pl.BlockSpec((HD, D), fixed), pl.BlockSpec((ROW_TILE, D), row)],
        out_specs=pl.BlockSpec((ROW_TILE, D), row),
        out_shape=jax.ShapeDtypeStruct((T, D), f32),
        name="gdn_out_proj",
    )(o_fwd.reshape(T, HD), o_bwd.reshape(T, HD), z, out_norm_g.reshape(1, GDN_DV), w_out.astype(bf), ht)
    return h_new.reshape(B, S, D)


def _sigmoid(x):
    return 0.5 * jnp.tanh(0.5 * x) + 0.5


def _silu(x):
    return x * _sigmoid(x)


def _moe_router_kernel(h_ref, g_ref, w_ref, xn_ref, aff_ref):
    xn = _rms(h_ref[...], g_ref[...])
    xn_ref[...] = xn.astype(xn_ref.dtype)
    logits = _bdot(xn, w_ref[...])
    e = jnp.exp(logits - jnp.max(logits, axis=-1, keepdims=True))
    aff_ref[...] = e / jnp.sum(e, axis=-1, keepdims=True)


def _moe_ffn_kernel(x_ref, wg_ref, wu_ref, wd_ref, gate_ref, y_ref):
    x = x_ref[0]
    hid = _silu(_bdot(x, wg_ref[0])) * _bdot(x, wu_ref[0])
    y_ref[0] = (_bdot(hid, wd_ref[0]) * gate_ref[0]).astype(y_ref.dtype)


COMBINE_TILE = 256
COMBINE_SUBTILES = 2
STRIP_ROWS = 64
STRIP_ALIGN = 16
STRIPS_PER_DOT = 4


def _moe_combine_ple_kernel(base_ref, off_ref, cnt_ref,
                            h_ref, p_ref, g_ref, wg_ref, wp_ref, fg_ref, ye_ref, tok_ref,
                            o_ref, ybuf, tbuf, sems, yextra, textra, extra_sems, acc_ref,
                            *, n_tiles, total_rows, final_norm):
    E, TB, R, U = N_EXPERTS, COMBINE_TILE, STRIP_ROWS, COMBINE_SUBTILES
    K = STRIPS_PER_DOT * R
    n_steps = n_tiles // U
    j = pl.program_id(0)
    cur = j % 2

    def strip_copies(step, u, e, b):
        start = pl.multiple_of(base_ref[e * n_tiles + step * U + u], STRIP_ALIGN)
        rows = pl.ds((u * E + e) * R, R)
        return (pltpu.make_async_copy(ye_ref.at[pl.ds(start, R), :], ybuf.at[b, rows, :],
                                      sems.at[0, b, u * E + e]),
                pltpu.make_async_copy(tok_ref.at[pl.ds(start, R), :], tbuf.at[b, rows, :],
                                      sems.at[1, b, u * E + e]))

    def all_strip_copies(step, b):
        return [cp for u in range(U) for e in range(E) for cp in strip_copies(step, u, e, b)]

    @pl.when(j == 0)
    def _():
        for cp in all_strip_copies(0, 0):
            cp.start()

    @pl.when(j + 1 < n_steps)
    def _():
        for cp in all_strip_copies(j + 1, 1 - cur):
            cp.start()

    for cp in all_strip_copies(j, cur):
        cp.wait()

    lane = lax.broadcasted_iota(jnp.int32, (K, TB), 1)
    srow = lax.broadcasted_iota(jnp.int32, (R, 1), 0)
    groups = [(u, grp) for u in range(U) for grp in range(E // STRIPS_PER_DOT)]
    hots = []
    for u, grp in groups:
        tile = j * U + u
        wanted = []
        for e in range(grp * STRIPS_PER_DOT, (grp + 1) * STRIPS_PER_DOT):
            lo = off_ref[e * n_tiles + tile]
            wanted.append((srow >= lo) & (srow < lo + cnt_ref[e * n_tiles + tile]))
        rows = pl.ds((u * E + grp * STRIPS_PER_DOT) * R, K)
        tok = tbuf[cur, rows, 0:1] - tile * TB
        hots.append(jnp.where(jnp.concatenate(wanted, axis=0) & (lane == tok), 1.0, 0.0))
    parts = [_bdot(hot.T, ybuf[cur, pl.ds((u * E + grp * STRIPS_PER_DOT) * R, K), :])
             for hot, (u, grp) in zip(hots, groups)]
    for u in range(U):
        mine = [part for part, (uu, _) in zip(parts, groups) if uu == u]
        acc_ref[u] = functools.reduce(lambda a, b: a + b, mine)

    lane_r = lax.broadcasted_iota(jnp.int32, (R, TB), 1)
    for u in range(U):
        tile = j * U + u
        for e in range(E):
            base = base_ref[e * n_tiles + tile]
            first = base + off_ref[e * n_tiles + tile]
            end = first + cnt_ref[e * n_tiles + tile]
            n_sub = (end - base + R - 1) // R

            def body(k, carry, u=u, tile=tile, base=base, first=first, end=end):
                want0 = base + k * R
                start = pl.multiple_of(jnp.minimum(want0, total_rows - R), STRIP_ALIGN)
                cy = pltpu.make_async_copy(ye_ref.at[pl.ds(start, R), :], yextra, extra_sems.at[0])
                ct = pltpu.make_async_copy(tok_ref.at[pl.ds(start, R), :], textra, extra_sems.at[1])
                cy.start()
                ct.start()
                cy.wait()
                ct.wait()
                grow = srow + start
                wanted = (grow >= jnp.maximum(want0, first)) & (grow < jnp.minimum(want0 + R, end))
                hot_t = jnp.where(wanted & (lane_r == textra[:, 0:1] - tile * TB), 1.0, 0.0)
                acc_ref[u] += _bdot(hot_t.T, yextra[...])
                return carry

            lax.fori_loop(1, n_sub, body, 0)

    tiles = [pl.ds(u * TB, TB) for u in range(U)]
    hm = [h_ref[t, :] + acc_ref[u] for u, t in enumerate(tiles)]
    gate = [_sigmoid(_bdot(_rms(x, g_ref[...]), wg_ref[...])) for x in hm]
    out = [x + _bdot(p_ref[t, :], wp_ref[...]) * gt for x, gt, t in zip(hm, gate, tiles)]
    for x, t in zip(out, tiles):
        o_ref[t, :] = _rms(x, fg_ref[...]) if final_norm else x


F32_INF_BITS = 0x7F800000


def _topk_select_kernel(aff_ref, wcum_ref, *, cap):
    a = aff_ref[0]
    R = a.shape[0]
    bits = lax.bitcast_convert_type(a, jnp.int32)

    def count(mask):
        m = mask.astype(jnp.float32)
        return jnp.sum(jnp.sum(m, axis=0, keepdims=True), axis=1, keepdims=True)

    def bisect(_, carry):
        lo, hi = carry
        mid = lo + jnp.right_shift(hi - lo + 1, 1)
        ok = count(bits >= mid) >= cap
        return jnp.where(ok, mid, lo), jnp.where(ok, hi, mid - 1)

    thr, _ = lax.fori_loop(0, 31, bisect, (jnp.zeros((1, 1), jnp.int32),
                                           jnp.full((1, 1), F32_INF_BITS, jnp.int32)))
    gt = bits > thr
    eq = bits == thr
    need = cap - count(gt)

    lane_r = lax.broadcasted_iota(jnp.int32, (LANES, LANES), 0)
    lane_c = lax.broadcasted_iota(jnp.int32, (LANES, LANES), 1)
    upper = (lane_r <= lane_c).astype(jnp.bfloat16)
    row_r = lax.broadcasted_iota(jnp.int32, (R, R), 0)
    row_c = lax.broadcasted_iota(jnp.int32, (R, R), 1)
    earlier_rows = (row_r > row_c).astype(jnp.bfloat16)

    eq_in_row = jnp.dot(eq.astype(jnp.bfloat16), upper, preferred_element_type=jnp.float32)
    eq_row_total = jnp.broadcast_to(eq_in_row[:, LANES - 1:LANES], (R, LANES))
    eq_before = jnp.dot(earlier_rows, eq_row_total.astype(jnp.bfloat16), preferred_element_type=jnp.float32)
    sel = gt | (eq & (eq_before + eq_in_row <= need))
    wcum = jnp.dot(sel.astype(jnp.bfloat16), upper, preferred_element_type=jnp.float32)
    wcum_ref[0] = wcum.astype(jnp.int32)


SLOT_TILE = 512


def _slot_token_kernel(row_end_ref, wcum_ref, tok_ref):
    SB = tok_ref.shape[1]
    row_end = row_end_ref[0]
    R = row_end.shape[1]
    slot = pl.program_id(1) * SB + lax.broadcasted_iota(jnp.int32, (SB, 1), 0)
    done = row_end <= slot
    row = jnp.sum(done.astype(jnp.int32), axis=1, keepdims=True)
    row_start = jnp.max(jnp.where(done, row_end, 0), axis=1, keepdims=True)
    one_hot = (lax.broadcasted_iota(jnp.int32, (SB, R), 1) == row).astype(jnp.bfloat16)
    wrow = jnp.dot(one_hot, wcum_ref[0].astype(jnp.float32).astype(jnp.bfloat16),
                   preferred_element_type=jnp.float32)
    local = (slot - row_start).astype(jnp.float32)
    lane = jnp.sum((wrow <= local).astype(jnp.int32), axis=1, keepdims=True)
    tok_ref[0] = jnp.broadcast_to(row * LANES + lane, (SB, LANES))


def expert_choice_indices(aff_t, cap):
    E, T = aff_t.shape
    R = T // LANES
    wcum = pl.pallas_call(
        functools.partial(_topk_select_kernel, cap=cap),
        grid=(E,),
        in_specs=[pl.BlockSpec((1, R, LANES), lambda e: (e, 0, 0))],
        out_specs=pl.BlockSpec((1, R, LANES), lambda e: (e, 0, 0)),
        out_shape=jax.ShapeDtypeStruct((E, R, LANES), jnp.int32),
        name="moe_topk_select",
    )(aff_t.reshape(E, R, LANES))
    row_end = jnp.cumsum(wcum[:, :, LANES - 1], axis=1)
    row_start = row_end - wcum[:, :, LANES - 1]
    tok = pl.pallas_call(
        _slot_token_kernel,
        grid=(E, cap // SLOT_TILE),
        in_specs=[pl.BlockSpec((1, 1, R), lambda e, j: (e, 0, 0)),
                  pl.BlockSpec((1, R, LANES), lambda e, j: (e, 0, 0))],
        out_specs=pl.BlockSpec((1, SLOT_TILE, LANES), lambda e, j: (e, j, 0)),
        out_shape=jax.ShapeDtypeStruct((E, cap, LANES), jnp.int32),
        name="moe_slot_token",
    )(row_end.reshape(E, 1, R), wcum)
    return tok, row_start


def moe_ple_layer(h, p, norm_g, w_router, w_gate, w_up, w_down, ple_norm_g, ple_w_gate, ple_w_proj,
                  final_g, final_norm):
    T, D = h.shape
    E, F = N_EXPERTS, w_gate.shape[-1]
    cap = EC_CAPACITY_FACTOR * T // E
    bf = jnp.bfloat16
    row = lambda i: (i, 0)
    fixed = lambda i: (0, 0)
    xn, aff = pl.pallas_call(
        _moe_router_kernel,
        grid=(T // ROW_TILE,),
        in_specs=[pl.BlockSpec((ROW_TILE, D), row), pl.BlockSpec((1, D), fixed),
                  pl.BlockSpec((D, E), fixed)],
        out_specs=[pl.BlockSpec((ROW_TILE, D), row), pl.BlockSpec((ROW_TILE, E), row)],
        out_shape=[jax.ShapeDtypeStruct((T, D), bf), jax.ShapeDtypeStruct((T, E), jnp.float32)],
        name="moe_router",
    )(h, norm_g.reshape(1, D), w_router.astype(bf))
    aff_t = aff.T
    tok, row_start = expert_choice_indices(aff_t, cap)
    idx = tok[:, :, 0]
    gate = jnp.take_along_axis(aff_t, idx, axis=1)
    xe = xn.at[idx].get(mode='promise_in_bounds')
    ye = pl.pallas_call(
        _moe_ffn_kernel,
        grid=(E, cap // FFN_ROW_TILE),
        in_specs=[pl.BlockSpec((1, FFN_ROW_TILE, D), lambda e, i: (e, i, 0)),
                  pl.BlockSpec((1, D, F), lambda e, i: (e, 0, 0)),
                  pl.BlockSpec((1, D, F), lambda e, i: (e, 0, 0)),
                  pl.BlockSpec((1, F, D), lambda e, i: (e, 0, 0)),
                  pl.BlockSpec((1, FFN_ROW_TILE, 1), lambda e, i: (e, i, 0))],
        out_specs=pl.BlockSpec((1, FFN_ROW_TILE, D), lambda e, i: (e, i, 0)),
        out_shape=jax.ShapeDtypeStruct((E, cap, D), bf),
        compiler_params=pltpu.CompilerParams(dimension_semantics=("parallel", "arbitrary"),
                                             vmem_limit_bytes=VMEM_LIMIT_BYTES),
        name="moe_expert_ffn",
    )(xe, w_gate.astype(bf), w_up.astype(bf), w_down.astype(bf), gate[..., None])

    TB, R, U = COMBINE_TILE, STRIP_ROWS, COMBINE_SUBTILES
    n_tiles = T // TB
    total_rows = E * cap
    lo = jnp.concatenate([row_start[:, ::TB // LANES], jnp.full((E, 1), cap, jnp.int32)], axis=1)
    first = lo[:, :-1] + (jnp.arange(E, dtype=jnp.int32) * cap)[:, None]
    cnt = lo[:, 1:] - lo[:, :-1]
    base = jnp.minimum(first // STRIP_ALIGN * STRIP_ALIGN, total_rows - R)
    off = first - base

    P = p.shape[1]
    tile = lambda i, *_: (i, 0)
    const = lambda i, *_: (0, 0)
    return pl.pallas_call(
        functools.partial(_moe_combine_ple_kernel, n_tiles=n_tiles, total_rows=total_rows,
                          final_norm=final_norm),
        grid_spec=pltpu.PrefetchScalarGridSpec(
            num_scalar_prefetch=3,
            grid=(n_tiles // U,),
            in_specs=[pl.BlockSpec((U * TB, D), tile), pl.BlockSpec((U * TB, P), tile),
                      pl.BlockSpec((1, D), const), pl.BlockSpec((D, D), const), pl.BlockSpec((P, D), const),
                      pl.BlockSpec((1, D), const),
                      pl.BlockSpec(memory_space=pl.ANY), pl.BlockSpec(memory_space=pl.ANY)],
            out_specs=pl.BlockSpec((U * TB, D), tile),
            scratch_shapes=[pltpu.VMEM((2, U * E * R, D), bf), pltpu.VMEM((2, U * E * R, LANES), jnp.int32),
                            pltpu.SemaphoreType.DMA((2, 2, U * E)),
                            pltpu.VMEM((R, D), bf), pltpu.VMEM((R, LANES), jnp.int32),
                            pltpu.SemaphoreType.DMA((2,)),
                            pltpu.VMEM((U, TB, D), jnp.float32)]),
        out_shape=jax.ShapeDtypeStruct((T, D), jnp.float32),
        compiler_params=pltpu.CompilerParams(dimension_semantics=("arbitrary",),
                                             vmem_limit_bytes=VMEM_LIMIT_BYTES),
        name="moe_combine_ple",
    )(base.reshape(-1), off.reshape(-1), cnt.reshape(-1),
      h, p, ple_norm_g.reshape(1, D), ple_w_gate.astype(bf), ple_w_proj.astype(bf), final_g.reshape(1, D),
      ye.reshape(total_rows, D), tok.reshape(total_rows, LANES))


def encoder_trunk(x, p, W):
    h = x
    for i in range(DEPTH):
        j = i // N_MIXERS
        if i % N_MIXERS == 0:
            h = mla_layer(h, W['norm_mix_g'][i], W['mla_w_in'][j], W['mla_q_norm_g'][j], W['mla_w_qb'][j],
                          W['mla_kv_norm_g'][j], W['mla_w_kvb'][j], W['mla_w_out'][j])
        else:
            h = gdn_layer(h, W['norm_mix_g'][i], W['gdn_w_in'][j], W['gdn_conv_w'][j], W['gdn_A_log'][j],
                          W['gdn_dt_bias'][j], W['gdn_norm_g'][j], W['gdn_w_out'][j])
        B, S, D = h.shape
        ht = moe_ple_layer(h.reshape(B * S, D), p[i].reshape(B * S, PLE_DIM), W['norm_ffn_g'][i],
                           W['moe_w_router'][i], W['moe_w_gate'][i], W['moe_w_up'][i], W['moe_w_down'][i],
                           W['norm_ple_g'][i], W['ple_w_gate'][i], W['ple_w_proj'][i],
                           W['final_norm_g'], final_norm=(i == DEPTH - 1))
        h = ht.reshape(B, S, D)
    return h


def kernel(x_prompt, x_sample, p_prompt, p_sample, norm_mix_g, norm_ffn_g, norm_ple_g, final_norm_g,
           mla_w_in, mla_q_norm_g, mla_w_qb, mla_kv_norm_g, mla_w_kvb, mla_w_out,
           gdn_w_in, gdn_conv_w, gdn_A_log, gdn_dt_bias, gdn_norm_g, gdn_w_out,
           moe_w_router, moe_w_gate, moe_w_up, moe_w_down, ple_w_proj, ple_w_gate):
    W = {
        'norm_mix_g': norm_mix_g, 'norm_ffn_g': norm_ffn_g, 'norm_ple_g': norm_ple_g,
        'final_norm_g': final_norm_g,
        'mla_w_in': mla_w_in, 'mla_q_norm_g': mla_q_norm_g, 'mla_w_qb': mla_w_qb,
        'mla_kv_norm_g': mla_kv_norm_g, 'mla_w_kvb': mla_w_kvb, 'mla_w_out': mla_w_out,
        'gdn_w_in': gdn_w_in, 'gdn_conv_w': gdn_conv_w, 'gdn_A_log': gdn_A_log,
        'gdn_dt_bias': gdn_dt_bias, 'gdn_norm_g': gdn_norm_g, 'gdn_w_out': gdn_w_out,
        'moe_w_router': moe_w_router, 'moe_w_gate': moe_w_gate, 'moe_w_up': moe_w_up,
        'moe_w_down': moe_w_down, 'ple_w_proj': ple_w_proj, 'ple_w_gate': ple_w_gate,
    }
    y_prompt = encoder_trunk(x_prompt, p_prompt, W)
    y_sample = encoder_trunk(x_sample, p_sample, W)
    return (y_prompt, y_sample)
```
